```python
import math
import jax, jax.numpy as jnp
from jax import lax
import numpy as np

D_MODEL = 1024
BATCH = 32
SEQ = 2048
DEPTH = 1

GRID_W = 64
CTX_LEN = 256
MLA_HEADS = 8
QK_NOPE = 64
QK_ROPE = 32
QK_DIM = QK_NOPE + QK_ROPE
V_DIM = 64
Q_LORA = 384
KV_LORA = 256
ROPE_THETA = 10000.0
Q_BLOCK = 128
F_GROUPS = 4
F_GROUP_DIM = 128
D_F = F_GROUPS * F_GROUP_DIM
OFF_Q = 0
OFF_KV = OFF_Q + Q_LORA
OFF_KPE = OFF_KV + KV_LORA
OFF_F = OFF_KPE + QK_ROPE
OFF_GA = OFF_F + D_F
OFF_GB = OFF_GA + D_MODEL
N_IN = OFF_GB + D_MODEL
N_EXPERTS = 16
D_EXPERT = 512
CAPACITY_FACTOR = 2
EPS = 1e-6

kernel_name = "hybrid_mla_fnet_ecmoe_dit_layer"


def _rms(x, g):
    xf = x.astype(jnp.float32)
    y = xf * lax.rsqrt(jnp.mean(xf * xf, axis=-1, keepdims=True) + EPS)
    return (y * g.astype(jnp.float32)).astype(x.dtype)


def _modulate(h, shift, scale):
    return h * (1.0 + scale) + shift


def _axial_tables(pos_row, pos_col, dtype):
    half = QK_ROPE // 2
    n_freq = half // 2
    inv_freq = 1.0 / (ROPE_THETA ** (jnp.arange(n_freq, dtype=jnp.float32) / n_freq))

    def tab(pos):
        ang = pos.astype(jnp.float32)[:, None] * inv_freq[None, :]
        ang = jnp.concatenate([ang, ang], axis=-1)[None, :, None, :]
        return jnp.cos(ang).astype(dtype), jnp.sin(ang).astype(dtype)

    cr, sr = tab(pos_row)
    cc, scol = tab(pos_col)
    return cr, sr, cc, scol


def _rope_axis(x, cos, sin):
    x1, x2 = jnp.split(x, 2, axis=-1)
    return x * cos + jnp.concatenate([-x2, x1], axis=-1) * sin


def _rope_2d(x, tabs):
    cr, sr, cc, scol = tabs
    xr, xc = jnp.split(x, 2, axis=-1)
    return jnp.concatenate([_rope_axis(xr, cr, sr), _rope_axis(xc, cc, scol)], axis=-1)


def _mla_qkv(p, q_a_g, kv_a_g, w_q_up, w_kv_up, q_g, k_g, tabs):
    B, T, _ = p.shape
    cq = _rms(p[..., OFF_Q:OFF_KV], q_a_g)
    ckv = _rms(p[..., OFF_KV:OFF_KPE], kv_a_g)
    k_pe = p[..., OFF_KPE:OFF_F]
    q = (cq @ w_q_up).reshape(B, T, MLA_HEADS, QK_DIM)
    kv = (ckv @ w_kv_up).reshape(B, T, MLA_HEADS, QK_NOPE + V_DIM)
    k_nope, v = kv[..., :QK_NOPE], kv[..., QK_NOPE:]
    k = jnp.concatenate([k_nope, jnp.broadcast_to(k_pe[:, :, None, :], (B, T, MLA_HEADS, QK_ROPE))], axis=-1)
    q = _rms(q, q_g)
    k = _rms(k, k_g)
    if tabs is not None:
        q = jnp.concatenate([q[..., :QK_NOPE], _rope_2d(q[..., QK_NOPE:], tabs)], axis=-1)
        k = jnp.concatenate([k[..., :QK_NOPE], _rope_2d(k[..., QK_NOPE:], tabs)], axis=-1)
    return q, k, v


def _attend_dense(q, k, v):
    s = jnp.einsum('bqhd,bkhd->bhqk', q, k, preferred_element_type=jnp.float32) * (QK_DIM ** -0.5)
    pr = jax.nn.softmax(s, axis=-1).astype(v.dtype)
    return jnp.einsum('bhqk,bkhd->bqhd', pr, v)


def _attend_latent(q, k_lat, v_lat, k_ctx, v_ctx):
    B, T, H, _ = q.shape
    k_all = jnp.concatenate([k_lat, k_ctx], axis=1)
    v_all = jnp.concatenate([v_lat, v_ctx], axis=1)
    nb = T // Q_BLOCK
    qb = jnp.moveaxis(q.reshape(B, nb, Q_BLOCK, H, QK_DIM), 1, 0)
    o = lax.map(lambda qi: _attend_dense(qi, k_all, v_all), qb)
    return jnp.moveaxis(o, 0, 1).reshape(B, T, H * V_DIM)


def _fourier(f):
    B, T, _ = f.shape
    fg = f.reshape(B, T, F_GROUPS, F_GROUP_DIM).astype(jnp.float32)
    y = jnp.fft.fft2(fg, axes=(1, 3), norm="ortho").real
    return y.reshape(B, T, D_F).astype(f.dtype)


def _expert_choice(h, w_r, w_g, w_u, w_d):
    B, T, D = h.shape
    cap = (CAPACITY_FACTOR * T) // N_EXPERTS
    logits = jnp.einsum('btd,de->bte', h, w_r, preferred_element_type=jnp.float32)
    aff = jax.nn.softmax(logits, axis=-1)
    gate, idx = lax.top_k(jnp.swapaxes(aff, 1, 2), cap)
    xe = jax.vmap(lambda hb, ib: hb[ib])(h, idx)
    a = jnp.einsum('becd,edf->becf', xe, w_g)
    u = jnp.einsum('becd,edf->becf', xe, w_u)
    ye = jnp.einsum('becf,efd->becd', jax.nn.silu(a) * u, w_d) * gate[..., None].astype(h.dtype)
    return jax.vmap(lambda yb, ib: jnp.zeros((T, D), yb.dtype).at[ib.reshape(-1)].add(yb.reshape(-1, D)))(ye, idx)


def _merge(p, attn_o, four_o, w_o_attn, w_fourier, w_out):
    ga = jax.nn.sigmoid(p[..., OFF_GA:OFF_GB])
    gb = jax.nn.sigmoid(p[..., OFF_GB:N_IN])
    return (ga * (attn_o @ w_o_attn) + gb * (four_o @ w_fourier)) @ w_out


def setup_inputs(seed: int = 0) -> dict:
    key = jax.random.key(seed)
    ks = jax.random.split(key, 24)
    L, D = DEPTH, D_MODEL

    def nrm(k, shape, fan_in, mult=1.0):
        return jax.random.normal(k, shape, jnp.float32) * (mult * fan_in ** -0.5)

    def gain(k, shape):
        return 1.0 + 0.02 * jax.random.normal(k, shape, jnp.float32)

    return {
        "x": jax.random.normal(ks[0], (BATCH, SEQ, D), jnp.float32),
        "c": jax.random.normal(ks[1], (BATCH, D), jnp.float32),
        "ctx": jax.random.normal(ks[2], (BATCH, CTX_LEN, D), jnp.float32),
        "c_ctx": 0.5 * jax.random.normal(ks[3], (D,), jnp.float32),
        "w_mod": nrm(ks[4], (L, D, 6 * D), D, 0.5),
        "b_mod": 0.02 * jax.random.normal(ks[5], (L, 6 * D), jnp.float32),
        "norm1_g": gain(ks[6], (L, D)),
        "w_in": nrm(ks[7], (L, D, N_IN), D),
        "q_a_norm_g": gain(ks[8], (L, Q_LORA)),
        "kv_a_norm_g": gain(ks[9], (L, KV_LORA)),
        "w_q_up": nrm(ks[10], (L, Q_LORA, MLA_HEADS * QK_DIM), Q_LORA),
        "w_kv_up": nrm(ks[11], (L, KV_LORA, MLA_HEADS * (QK_NOPE + V_DIM)), KV_LORA),
        "q_norm_g": gain(ks[12], (L, QK_DIM)),
        "k_norm_g": gain(ks[13], (L, QK_DIM)),
        "w_o_attn": nrm(ks[14], (L, MLA_HEADS * V_DIM, D), MLA_HEADS * V_DIM),
        "w_fourier": nrm(ks[15], (L, D_F, D), D_F),
        "w_out": nrm(ks[16], (L, D, D), D),
        "norm2_g": gain(ks[17], (L, D)),
        "w_router": nrm(ks[18], (L, D, N_EXPERTS), D),
        "w_e_gate": nrm(ks[19], (L, N_EXPERTS, D, D_EXPERT), D),
        "w_e_up": nrm(ks[20], (L, N_EXPERTS, D, D_EXPERT), D),
        "w_e_down": nrm(ks[21], (L, N_EXPERTS, D_EXPERT, D), D_EXPERT),
    }


def reference(x, c, ctx, c_ctx, w_mod, b_mod, norm1_g, w_in, q_a_norm_g, kv_a_norm_g,
              w_q_up, w_kv_up, q_norm_g, k_norm_g, w_o_attn, w_fourier, w_out, norm2_g,
              w_router, w_e_gate, w_e_up, w_e_down):
    T = x.shape[1]
    rows = T // GRID_W
    pos_row = jnp.repeat(jnp.arange(rows, dtype=jnp.int32), GRID_W)
    pos_col = jnp.tile(jnp.arange(GRID_W, dtype=jnp.int32), rows)
    tabs = _axial_tables(pos_row, pos_col, x.dtype)
    s_lat = jax.nn.silu(c)
    s_ctx = jax.nn.silu(c_ctx)

    for l in range(DEPTH):
        last = l == DEPTH - 1
        mod_l = (s_lat @ w_mod[l] + b_mod[l])[:, None, :]
        mod_c = s_ctx @ w_mod[l] + b_mod[l]
        sh1, sc1, g1, sh2, sc2, g2 = jnp.split(mod_l, 6, axis=-1)
        csh1, csc1, cg1, csh2, csc2, cg2 = jnp.split(mod_c, 6, axis=-1)

        h = _modulate(_rms(x, norm1_g[l]), sh1, sc1)
        hc = _modulate(_rms(ctx, norm1_g[l]), csh1, csc1)
        p = h @ w_in[l]
        pc = hc @ w_in[l]
        mla_w = (q_a_norm_g[l], kv_a_norm_g[l], w_q_up[l], w_kv_up[l], q_norm_g[l], k_norm_g[l])
        q, k, v = _mla_qkv(p, *mla_w, tabs)
        qc, kc, vc = _mla_qkv(pc, *mla_w, None)
        attn_lat = _attend_latent(q, k, v, kc, vc)
        four_lat = _fourier(p[..., OFF_F:OFF_GA])
        x = x + g1 * _merge(p, attn_lat, four_lat, w_o_attn[l], w_fourier[l], w_out[l])

        if not last:
            B = ctx.shape[0]
            attn_ctx = _attend_dense(qc, kc, vc).reshape(B, CTX_LEN, MLA_HEADS * V_DIM)
            four_ctx = _fourier(pc[..., OFF_F:OFF_GA])
            ctx = ctx + cg1 * _merge(pc, attn_ctx, four_ctx, w_o_attn[l], w_fourier[l], w_out[l])

        moe_w = (w_router[l], w_e_gate[l], w_e_up[l], w_e_down[l])
        h2 = _modulate(_rms(x, norm2_g[l]), sh2, sc2)
        x = x + g2 * _expert_choice(h2, *moe_w)
        if not last:
            hc2 = _modulate(_rms(ctx, norm2_g[l]), csh2, csc2)
            ctx = ctx + cg2 * _expert_choice(hc2, *moe_w)
    return x
```

```python
import functools

import numpy as np
import jax
import jax.numpy as jnp
from jax import lax
from jax.experimental import pallas as pl
from jax.experimental.pallas import tpu as pltpu

F32 = jnp.float32
BF16 = jnp.bfloat16

D_MODEL = 1024
GRID_W = 64
MLA_HEADS = 8
QK_NOPE = 64
QK_ROPE = 32
QK_DIM = QK_NOPE + QK_ROPE
V_DIM = 64
Q_LORA = 384
KV_LORA = 256
ROPE_THETA = 10000.0
F_GROUPS = 4
F_GROUP_DIM = 128
D_F = F_GROUPS * F_GROUP_DIM
OFF_Q = 0
OFF_KV = OFF_Q + Q_LORA
OFF_KPE = OFF_KV + KV_LORA
OFF_F = OFF_KPE + QK_ROPE
OFF_GA = OFF_F + D_F
OFF_GB = OFF_GA + D_MODEL
N_IN = OFF_GB + D_MODEL
N_EXPERTS = 16
D_EXPERT = 512
CAPACITY_FACTOR = 2
EPS = 1e-6

LANES = 128
SUBLANES = 8
HEAD_SLOT = LANES
SHIFT_LANE = QK_DIM
BOUND_SLACK = 1.02
DEN_FLOOR = 2.0 ** -60
LOG2E = 1.4426950408889634
FFT_R = 8
FFT_N = 256
KEY_CHUNK = 256
VMEM_LIMIT = 56 * 1024 * 1024

PC_Q = 0
PC_KV = PC_Q + Q_LORA
PC_PE = PC_KV + KV_LORA
PC_F = PC_PE + HEAD_SLOT
PC_GA = PC_F + D_F
PC_GB = PC_GA + D_MODEL
PC_END = PC_GB + D_MODEL


def _dot(a, b):
    return jnp.dot(a, b, preferred_element_type=F32)


def _dot_nt(a, b):
    return lax.dot_general(a, b, (((1,), (1,)), ((), ())), preferred_element_type=F32)


def _split2(a):
    hi = a.astype(BF16)
    lo = (a - hi.astype(F32)).astype(BF16)
    return hi, lo


def _split3(a):
    hi = a.astype(BF16)
    r = a - hi.astype(F32)
    mid = r.astype(BF16)
    lo = (r - mid.astype(F32)).astype(BF16)
    return hi, mid, lo


def _dot3(a, b):
    ah, al = _split2(a)
    bh, bl = _split2(b)
    return _dot(ah, bh) + (_dot(ah, bl) + _dot(al, bh))


def _sigmoid(x):
    return 1.0 / (1.0 + jnp.exp(-x))


def _params(sem):
    return pltpu.CompilerParams(dimension_semantics=sem, vmem_limit_bytes=VMEM_LIMIT)


def _mod_body(c_ref, w_ref, b_ref, o_ref):
    c = c_ref[...]
    s = c * _sigmoid(c)
    o_ref[...] = _dot3(s, w_ref[...]) + b_ref[...]


def _mod_call(cc, w_mod, b_mod):
    rows = cc.shape[0]
    n = w_mod.shape[1]
    tn = 1024
    return pl.pallas_call(
        _mod_body,
        grid=(n // tn,),
        in_specs=[
            pl.BlockSpec((rows, D_MODEL), lambda j: (0, 0)),
            pl.BlockSpec((D_MODEL, tn), lambda j: (0, j)),
            pl.BlockSpec((1, tn), lambda j: (0, j)),
        ],
        out_specs=pl.BlockSpec((rows, tn), lambda j: (0, j)),
        out_shape=jax.ShapeDtypeStruct((rows, n), F32),
        compiler_params=_params(("arbitrary",)),
        name="mod",
    )(cc, w_mod, b_mod)


def _rms_rows(x, n):
    return x * lax.rsqrt(jnp.sum(x * x, axis=-1, keepdims=True) * (1.0 / n) + EPS)


def _rope(x, cos, sin_prev, sin_next):
    return x * cos + pltpu.roll(x, 8, 1) * sin_prev + pltpu.roll(x, LANES - 8, 1) * sin_next


def _proj_body(*refs, latent):
    if latent:
        (x_ref, g_ref, sc_ref, sh_ref, win_ref, qag_ref, kvag_ref, wq_ref, wk_ref, wv_ref,
         qg_ref, kg_ref, kb_ref, cos_ref, sp_ref, sn_ref,
         q_out, k_out, v_out, f_out, ga_out, gb_out) = refs
    else:
        (x_ref, g_ref, sc_ref, sh_ref, win_ref, kvag_ref, wk_ref, wv_ref, kg_ref,
         k_out, v_out) = refs
    x = x_ref[0]
    h = _rms_rows(x, D_MODEL) * g_ref[...]
    h = h * (1.0 + sc_ref[0]) + sh_ref[0]
    hb = h.astype(BF16)
    shift_lane = lax.broadcasted_iota(jnp.int32, (x.shape[0], HEAD_SLOT), 1) == SHIFT_LANE

    if latent:
        cos = cos_ref[...]
        sp = sp_ref[...]
        sn = sn_ref[...]

    ckv = _rms_rows(_dot(hb, win_ref[:, PC_KV:PC_PE]), KV_LORA) * kvag_ref[...]
    ckvb = ckv.astype(BF16)
    pe = _dot(hb, win_ref[:, PC_PE:PC_F])
    kall = _dot(ckvb, wk_ref[...])
    kg = kg_ref[...]
    for hd in range(MLA_HEADS):
        kh = kall[:, hd * HEAD_SLOT:(hd + 1) * HEAD_SLOT] + pe
        kh = _rms_rows(kh, QK_DIM) * kg
        if latent:
            kh = _rope(kh, cos, sp, sn)
        k_out[0, hd] = jnp.where(shift_lane, 1.0, kh).astype(BF16)
    v = _dot(ckvb, wv_ref[...])
    for hp in range(MLA_HEADS // 2):
        v_out[0, hp] = v[:, hp * LANES:(hp + 1) * LANES].T.astype(BF16)

    if latent:
        cq = _rms_rows(_dot(hb, win_ref[:, PC_Q:PC_KV]), Q_LORA) * qag_ref[...]
        qall = _dot(cq.astype(BF16), wq_ref[...])
        qg = qg_ref[...]
        kb = kb_ref[...]
        for hd in range(MLA_HEADS):
            qh = qall[:, hd * HEAD_SLOT:(hd + 1) * HEAD_SLOT]
            qh = _rope(_rms_rows(qh, QK_DIM) * qg, cos, sp, sn)
            bound = jnp.sqrt(jnp.sum(qh * qh, axis=-1, keepdims=True)) * kb
            q_out[0, hd] = jnp.where(shift_lane, -bound, qh).astype(BF16)
        f_out[0] = _dot(hb, win_ref[:, PC_F:PC_GA]).astype(BF16)
        ga_out[0] = _sigmoid(_dot(hb, win_ref[:, PC_GA:PC_GB])).astype(BF16)
        gb_out[0] = _sigmoid(_dot(hb, win_ref[:, PC_GB:PC_END])).astype(BF16)


def _full(shape):
    nd = len(shape)
    return pl.BlockSpec(shape, lambda *_: (0,) * nd)


def _proj_lat_call(x, g1n, sc1, sh1, w_in_p, qag, kvag, wq_p, wk_p, wv_p, qg_p, kg_p, kb, cos_t, sp_t, sn_t, tm):
    B, T, _ = x.shape
    H = MLA_HEADS
    tok = lambda w: pl.BlockSpec((1, tm, w), lambda b, j: (b, j, 0))
    per_b = pl.BlockSpec((1, 1, D_MODEL), lambda b, j: (b, 0, 0))
    tab = pl.BlockSpec((tm, HEAD_SLOT), lambda b, j: (j, 0))
    return pl.pallas_call(
        functools.partial(_proj_body, latent=True),
        grid=(B, T // tm),
        in_specs=[tok(D_MODEL), _full(g1n.shape), per_b, per_b, _full(w_in_p.shape), _full(qag.shape),
                  _full(kvag.shape), _full(wq_p.shape), _full(wk_p.shape), _full(wv_p.shape),
                  _full(qg_p.shape), _full(kg_p.shape), _full(kb.shape), tab, tab, tab],
        out_specs=[
            pl.BlockSpec((1, H, tm, HEAD_SLOT), lambda b, j: (b, 0, j, 0)),
            pl.BlockSpec((1, H, tm, HEAD_SLOT), lambda b, j: (b, 0, j, 0)),
            pl.BlockSpec((1, H // 2, LANES, tm), lambda b, j: (b, 0, 0, j)),
            tok(D_F), tok(D_MODEL), tok(D_MODEL),
        ],
        out_shape=[
            jax.ShapeDtypeStruct((B, H, T, HEAD_SLOT), BF16),
            jax.ShapeDtypeStruct((B, H, T, HEAD_SLOT), BF16),
            jax.ShapeDtypeStruct((B, H // 2, LANES, T), BF16),
            jax.ShapeDtypeStruct((B, T, D_F), BF16),
            jax.ShapeDtypeStruct((B, T, D_MODEL), BF16),
            jax.ShapeDtypeStruct((B, T, D_MODEL), BF16),
        ],
        compiler_params=_params(("parallel", "parallel")),
        name="proj_lat",
    )(x, g1n, sc1, sh1, w_in_p, qag, kvag, wq_p, wk_p, wv_p, qg_p, kg_p, kb, cos_t, sp_t, sn_t)


def _proj_ctx_call(ctx, g1n, csc1, csh1, w_in_p, kvag, wk_p, wv_p, kg_p):
    B, TC, _ = ctx.shape
    H = MLA_HEADS
    shared = pl.BlockSpec((1, 1, D_MODEL), lambda b: (0, 0, 0))
    return pl.pallas_call(
        functools.partial(_proj_body, latent=False),
        grid=(B,),
        in_specs=[pl.BlockSpec((1, TC, D_MODEL), lambda b: (b, 0, 0)), _full(g1n.shape), shared, shared,
                  _full(w_in_p.shape), _full(kvag.shape), _full(wk_p.shape), _full(wv_p.shape),
                  _full(kg_p.shape)],
        out_specs=[
            pl.BlockSpec((1, H, TC, HEAD_SLOT), lambda b: (b, 0, 0, 0)),
            pl.BlockSpec((1, H // 2, LANES, TC), lambda b: (b, 0, 0, 0)),
        ],
        out_shape=[
            jax.ShapeDtypeStruct((B, H, TC, HEAD_SLOT), BF16),
            jax.ShapeDtypeStruct((B, H // 2, LANES, TC), BF16),
        ],
        compiler_params=_params(("parallel",)),
        name="proj_ctx",
    )(ctx, g1n, csc1, csh1, w_in_p, kvag, wk_p, wv_p, kg_p)


def _attn_body(q_ref, kl_ref, kc_ref, vl_ref, vc_ref, o_ref):
    tq = q_ref.shape[2]
    T = kl_ref.shape[2]
    nck = T // KEY_CHUNK
    row = lax.broadcasted_iota(jnp.int32, (LANES, tq), 0)

    def probs(hd):
        q = q_ref[0, hd]
        den = jnp.zeros((1, tq), F32)
        ps = []
        for c in range(nck + 1):
            k = kc_ref[0, hd] if c == nck else kl_ref[0, hd, c * KEY_CHUNK:(c + 1) * KEY_CHUNK, :]
            e = jnp.exp2(_dot_nt(k, q))
            den = den + jnp.sum(e, axis=0, keepdims=True)
            ps.append(e.astype(BF16))
        return ps, den

    def weighted_values(hd, ps):
        hp = hd // 2
        o = _dot(vc_ref[0, hp], ps[nck])
        for c in range(nck):
            o = o + _dot(vl_ref[0, hp, :, c * KEY_CHUNK:(c + 1) * KEY_CHUNK], ps[c])
        return o

    outs, dens, prev = [], [], None
    for hd in range(MLA_HEADS + 1):
        cur = probs(hd) if hd < MLA_HEADS else None
        if prev is not None:
            outs.append(weighted_values(hd - 1, prev[0]) / prev[1])
            dens.append(prev[1])
        prev = cur
    for hp in range(MLA_HEADS // 2):
        o_ref[0, hp] = jnp.where(row < V_DIM, outs[2 * hp], outs[2 * hp + 1]).astype(BF16)
    min_den = functools.reduce(jnp.minimum, dens)

    @pl.when(jnp.min(min_den) < DEN_FLOOR)
    def _():
        def pair(hp, carry):
            res = []
            for hd in (2 * hp, 2 * hp + 1):
                q = q_ref[0, hd]
                sl = _dot_nt(kl_ref[0, hd], q)
                sc = _dot_nt(kc_ref[0, hd], q)
                m = jnp.maximum(jnp.max(sl, axis=0, keepdims=True), jnp.max(sc, axis=0, keepdims=True))
                el = jnp.exp2(sl - m)
                ec = jnp.exp2(sc - m)
                den = jnp.sum(el, axis=0, keepdims=True) + jnp.sum(ec, axis=0, keepdims=True)
                o = _dot(vl_ref[0, hp], el.astype(BF16)) + _dot(vc_ref[0, hp], ec.astype(BF16))
                res.append(o / den)
            o_ref[0, hp] = jnp.where(row < V_DIM, res[0], res[1]).astype(BF16)
            return carry

        lax.fori_loop(0, MLA_HEADS // 2, pair, 0)


def _attn_call(q, k_lat, k_ctx, vt_lat, vt_ctx, tq):
    B, H, T, _ = q.shape
    TC = k_ctx.shape[2]
    return pl.pallas_call(
        _attn_body,
        grid=(B, T // tq),
        in_specs=[
            pl.BlockSpec((1, H, tq, HEAD_SLOT), lambda b, j: (b, 0, j, 0)),
            pl.BlockSpec((1, H, T, HEAD_SLOT), lambda b, j: (b, 0, 0, 0)),
            pl.BlockSpec((1, H, TC, HEAD_SLOT), lambda b, j: (b, 0, 0, 0)),
            pl.BlockSpec((1, H // 2, LANES, T), lambda b, j: (b, 0, 0, 0)),
            pl.BlockSpec((1, H // 2, LANES, TC), lambda b, j: (b, 0, 0, 0)),
        ],
        out_specs=pl.BlockSpec((1, H // 2, LANES, tq), lambda b, j: (b, 0, 0, j)),
        out_shape=jax.ShapeDtypeStruct((B, H // 2, LANES, T), BF16),
        compiler_params=_params(("parallel", "parallel")),
        name="attn",
    )(q, k_lat, k_ctx, vt_lat, vt_ctx)


def _cadd(a, b):
    return a[0] + b[0], a[1] + b[1]


def _csub(a, b):
    return a[0] - b[0], a[1] - b[1]


def _cmul_neg_i(a):
    return a[1], -a[0]


def _cmul_pos_i(a):
    return -a[1], a[0]


def _fft8(u):
    r = 0.7071067811865476
    a0, a1 = _cadd(u[0], u[4]), _csub(u[0], u[4])
    a2, a3 = _cadd(u[2], u[6]), _csub(u[2], u[6])
    a4, a5 = _cadd(u[1], u[5]), _csub(u[1], u[5])
    a6, a7 = _cadd(u[3], u[7]), _csub(u[3], u[7])
    e0, e2 = _cadd(a0, a2), _csub(a0, a2)
    e1, e3 = _cadd(a1, _cmul_neg_i(a3)), _cadd(a1, _cmul_pos_i(a3))
    o0, o2 = _cadd(a4, a6), _csub(a4, a6)
    o1, o3 = _cadd(a5, _cmul_neg_i(a7)), _cadd(a5, _cmul_pos_i(a7))
    wo1 = (r * (o1[0] + o1[1]), r * (o1[1] - o1[0]))
    wo3 = (r * (o3[1] - o3[0]), -r * (o3[0] + o3[1]))
    return [_cadd(e0, o0), _cadd(e1, wo1), _cadd(e2, _cmul_neg_i(o2)), _cadd(e3, wo3),
            _csub(e0, o0), _csub(e1, wo1), _cadd(e2, _cmul_pos_i(o2)), _csub(e3, wo3)]


def _fourier_body(f_ref, cc_ref, twc_ref, tws_ref, m2_ref, o_ref, b_ref, y_ref):
    for g in range(F_GROUPS):
        cols = slice(g * F_GROUP_DIM, (g + 1) * F_GROUP_DIM)
        y = _dot(f_ref[0, :, cols], cc_ref[...])
        u = [(y[j * FFT_N:(j + 1) * FFT_N, :F_GROUP_DIM], y[j * FFT_N:(j + 1) * FFT_N, F_GROUP_DIM:])
             for j in range(FFT_R)]
        a = _fft8(u)
        for k1 in range(FFT_R):
            ar, ai = a[k1]
            if k1 > 0:
                c = twc_ref[k1 * FFT_N:(k1 + 1) * FFT_N, :]
                s = tws_ref[k1 * FFT_N:(k1 + 1) * FFT_N, :]
                ar, ai = ar * c + ai * s, ai * c - ar * s
            b_ref[k1, 0:FFT_N, cols] = ar.astype(BF16)
            b_ref[k1, FFT_N:2 * FFT_N, cols] = ai.astype(BF16)
    for k1 in range(FFT_R):
        y = _dot(m2_ref[...], b_ref[k1])
        for g in range(F_GROUPS):
            y_ref[g, pl.ds(k1, FFT_N, stride=FFT_R), :] = y[:, g * F_GROUP_DIM:(g + 1) * F_GROUP_DIM]
    for g in range(F_GROUPS):
        o_ref[0, :, g * F_GROUP_DIM:(g + 1) * F_GROUP_DIM] = y_ref[g].astype(BF16)


def _fourier_call(f, cc, twc, tws, m2):
    B, T, _ = f.shape
    return pl.pallas_call(
        _fourier_body,
        grid=(B,),
        in_specs=[pl.BlockSpec((1, T, D_F), lambda b: (b, 0, 0)), _full(cc.shape), _full(twc.shape),
                  _full(tws.shape), _full(m2.shape)],
        out_specs=pl.BlockSpec((1, T, D_F), lambda b: (b, 0, 0)),
        out_shape=jax.ShapeDtypeStruct((B, T, D_F), BF16),
        scratch_shapes=[pltpu.VMEM((FFT_R, 2 * FFT_N, D_F), BF16), pltpu.VMEM((F_GROUPS, T, F_GROUP_DIM), F32)],
        compiler_params=_params(("parallel",)),
        name="fourier",
    )(f, cc, twc, tws, m2)


def _merge_body(at_ref, fo_ref, ga_ref, gb_ref, x_ref, g1_ref, g2n_ref, sc2_ref, sh2_ref,
                wo_ref, wf_ref, wout_ref, wrh_ref, wrl_ref, x1_out, h2_out, aff_out):
    tm = x_ref.shape[1]
    attn_t = jnp.concatenate([at_ref[0, hp] for hp in range(MLA_HEADS // 2)], axis=0)
    a = lax.dot_general(attn_t, wo_ref[...], (((0,), (0,)), ((), ())), preferred_element_type=F32)
    fo = _dot(fo_ref[0], wf_ref[...])
    mix = ga_ref[0].astype(F32) * a + gb_ref[0].astype(F32) * fo
    y = _dot(mix.astype(BF16), wout_ref[...])
    x1 = x_ref[0] + g1_ref[0] * y
    x1_out[0] = x1
    h2 = _rms_rows(x1, D_MODEL) * g2n_ref[...]
    h2 = h2 * (1.0 + sc2_ref[0]) + sh2_ref[0]
    for c in range(D_MODEL // LANES):
        h2_out[0, :, c] = h2[:, c * LANES:(c + 1) * LANES].reshape(tm // SUBLANES, SUBLANES, LANES)
    hh, hl = _split2(h2)
    logits = _dot(hh, wrh_ref[...]) + (_dot(hh, wrl_ref[...]) + _dot(hl, wrh_ref[...]))
    lane = lax.broadcasted_iota(jnp.int32, logits.shape, 1)
    logits = jnp.where(lane < N_EXPERTS, logits, -1e30)
    ex = jnp.exp(logits - jnp.max(logits, axis=-1, keepdims=True))
    aff_out[0] = ex / jnp.sum(ex, axis=-1, keepdims=True)


def _merge_call(attn, four, ga, gb, x, g1, g2n, sc2, sh2, wo, wf, wout, wrh, wrl, tm):
    B, T, _ = x.shape
    H = MLA_HEADS
    tok = lambda w: pl.BlockSpec((1, tm, w), lambda b, j: (b, j, 0))
    per_b = pl.BlockSpec((1, 1, D_MODEL), lambda b, j: (b, 0, 0))
    nchunk = D_MODEL // LANES
    return pl.pallas_call(
        _merge_body,
        grid=(B, T // tm),
        in_specs=[
            pl.BlockSpec((1, H // 2, LANES, tm), lambda b, j: (b, 0, 0, j)),
            tok(D_F), tok(D_MODEL), tok(D_MODEL), tok(D_MODEL), per_b, _full(g2n.shape), per_b, per_b,
            _full(wo.shape), _full(wf.shape), _full(wout.shape), _full(wrh.shape), _full(wrl.shape),
        ],
        out_specs=[
            tok(D_MODEL),
            pl.BlockSpec((1, tm // SUBLANES, nchunk, SUBLANES, LANES), lambda b, j: (b, j, 0, 0, 0)),
            tok(LANES),
        ],
        out_shape=[
            jax.ShapeDtypeStruct((B, T, D_MODEL), F32),
            jax.ShapeDtypeStruct((B, T // SUBLANES, nchunk, SUBLANES, LANES), F32),
            jax.ShapeDtypeStruct((B, T, LANES), F32),
        ],
        compiler_params=_params(("parallel", "parallel")),
        name="merge",
    )(attn, four, ga, gb, x, g1, g2n, sc2, sh2, wo, wf, wout, wrh, wrl)


def _prefix_excl(m, tri):
    outs = []
    run = jnp.zeros((m.shape[0], 1), F32)
    for blk in range(m.shape[1] // LANES):
        mb = m[:, blk * LANES:(blk + 1) * LANES]
        inc = _dot(mb.astype(BF16), tri)
        outs.append(inc - mb + run)
        run = run + inc[:, LANES - 1:LANES]
    return jnp.concatenate(outs, axis=1)


def _select_body(aff_ref, tri_ref, pos_ref, idx_out, gate_out, *, cap):
    T = aff_ref.shape[1]
    aff = aff_ref[0].T[0:N_EXPERTS]
    bits = pltpu.bitcast(aff, jnp.int32)

    def search(i, thr):
        cand = thr | (jnp.int32(1) << (30 - i))
        cnt = jnp.sum((bits >= cand).astype(jnp.int32), axis=-1, keepdims=True)
        return jnp.where(cnt >= cap, cand, thr)

    thr = lax.fori_loop(0, 31, search, jnp.zeros((N_EXPERTS, 1), jnp.int32))
    gt = (bits > thr).astype(F32)
    eq = (bits == thr).astype(F32)
    need = cap - jnp.sum(gt, axis=-1, keepdims=True)
    tri = tri_ref[...]
    sel = gt + eq * (_prefix_excl(eq, tri) < need).astype(F32)
    slot = jnp.where(sel > 0.0, _prefix_excl(sel, tri), -1.0)

    pos = pos_ref[...]
    srow = lax.broadcasted_iota(jnp.int32, (cap, T), 0).astype(F32)
    zero = jnp.zeros((3, T), BF16)
    for e in range(N_EXPERTS):
        onehot = jnp.where(srow == slot[e:e + 1, :], 1.0, 0.0).astype(BF16)
        ah, am, al = _split3(aff[e:e + 1, :])
        vals = jnp.concatenate([pos, ah, am, al, zero], axis=0)
        res = _dot_nt(vals, onehot)
        idx_out[0, e:e + 1, :] = (res[0:1] * float(LANES) + res[1:2]).astype(jnp.int32)
        gate_out[0, e:e + 1, :] = res[2:3] + (res[3:4] + res[4:5])


def _select_call(aff, tri, pos, cap):
    B, T, _ = aff.shape
    return pl.pallas_call(
        functools.partial(_select_body, cap=cap),
        grid=(B,),
        in_specs=[pl.BlockSpec((1, T, LANES), lambda b: (b, 0, 0)), _full(tri.shape), _full(pos.shape)],
        out_specs=[pl.BlockSpec((1, N_EXPERTS, cap), lambda b: (b, 0, 0)),
                   pl.BlockSpec((1, N_EXPERTS, cap), lambda b: (b, 0, 0))],
        out_shape=[jax.ShapeDtypeStruct((B, N_EXPERTS, cap), jnp.int32),
                   jax.ShapeDtypeStruct((B, N_EXPERTS, cap), F32)],
        compiler_params=_params(("parallel",)),
        name="select",
    )(aff, tri, pos)


def _moe_body(idx_ref, gate_ref, h2_ref, wg_ref, wu_ref, wd_ref, acc_ref, xg_ref, ye_ref,
              *, cap, stride):
    e = pl.program_id(1)
    nchunk = D_MODEL // LANES
    group = 8

    @pl.when(e == 0)
    def _():
        acc_ref[...] = jnp.zeros_like(acc_ref)

    for s in range(cap):
        base = idx_ref[0, e, s]
        xg_ref[pl.ds(s, nchunk, stride=stride), :] = h2_ref[0, pl.ds(base, nchunk, stride=SUBLANES), :]
    xb = jnp.concatenate([xg_ref[c * stride:c * stride + cap, :] for c in range(nchunk)], axis=1).astype(BF16)
    a = _dot(xb, wg_ref[0])
    u_ = _dot(xb, wu_ref[0])
    hmid = (a * _sigmoid(a) * u_).astype(BF16)
    ye = _dot(hmid, wd_ref[0])
    for c in range(nchunk):
        ye_ref[c * stride:c * stride + cap, :] = ye[:, c * LANES:(c + 1) * LANES]

    for s0 in range(0, cap, group):
        bases, news = [], []
        for s in range(s0, s0 + group):
            base = idx_ref[0, e, s]
            row = ye_ref[pl.ds(s, nchunk, stride=stride), :]
            cur = acc_ref[0, pl.ds(base, nchunk, stride=SUBLANES), :]
            bases.append(base)
            news.append(cur + gate_ref[0, e, s] * row)
        for base, new in zip(bases, news):
            acc_ref[0, pl.ds(base, nchunk, stride=SUBLANES), :] = new


def _moe_call(idx, gate, h2t, wg, wu, wd, T):
    B, E, cap = idx.shape
    nchunk = D_MODEL // LANES
    stride = cap + SUBLANES
    return pl.pallas_call(
        functools.partial(_moe_body, cap=cap, stride=stride),
        grid=(B, E),
        in_specs=[
            pl.BlockSpec((1, E, cap), lambda b, e: (b, 0, 0), memory_space=pltpu.SMEM),
            pl.BlockSpec((1, E, cap), lambda b, e: (b, 0, 0), memory_space=pltpu.SMEM),
            pl.BlockSpec((1, T * nchunk, LANES), lambda b, e: (b, 0, 0)),
            pl.BlockSpec((1, D_MODEL, D_EXPERT), lambda b, e: (e, 0, 0)),
            pl.BlockSpec((1, D_MODEL, D_EXPERT), lambda b, e: (e, 0, 0)),
            pl.BlockSpec((1, D_EXPERT, D_MODEL), lambda b, e: (e, 0, 0)),
        ],
        out_specs=pl.BlockSpec((1, T * nchunk, LANES), lambda b, e: (b, 0, 0)),
        out_shape=jax.ShapeDtypeStruct((B, T * nchunk, LANES), F32),
        scratch_shapes=[
            pltpu.VMEM((nchunk * stride, LANES), F32),
            pltpu.VMEM((nchunk * stride, LANES), F32),
        ],
        compiler_params=_params(("parallel", "arbitrary")),
        name="moe",
    )(idx, gate, h2t, wg, wu, wd)


def _final_body(x1_ref, y_ref, g2_ref, o_ref):
    tm = x1_ref.shape[1]
    for c in range(D_MODEL // LANES):
        cols = slice(c * LANES, (c + 1) * LANES)
        y = y_ref[0, :, c].reshape(tm, LANES)
        o_ref[0, :, cols] = x1_ref[0, :, cols] + g2_ref[0, :, cols] * y


def _final_call(x1, yt, g2, tm):
    B, T, _ = x1.shape
    nchunk = D_MODEL // LANES
    tok = pl.BlockSpec((1, tm, D_MODEL), lambda b, j: (b, j, 0))
    y = yt.reshape(B, T // SUBLANES, nchunk, SUBLANES, LANES)
    return pl.pallas_call(
        _final_body,
        grid=(B, T // tm),
        in_specs=[tok,
                  pl.BlockSpec((1, tm // SUBLANES, nchunk, SUBLANES, LANES), lambda b, j: (b, j, 0, 0, 0)),
                  pl.BlockSpec((1, 1, D_MODEL), lambda b, j: (b, 0, 0))],
        out_specs=tok,
        out_shape=jax.ShapeDtypeStruct((B, T, D_MODEL), F32),
        compiler_params=_params(("parallel", "parallel")),
        name="final",
    )(x1, y, g2)


def _rope_tables(T):
    half = QK_ROPE // 2
    n_freq = half // 2
    inv_freq = 1.0 / (ROPE_THETA ** (np.arange(n_freq, dtype=np.float32) / n_freq))
    t = np.arange(T)
    cos = np.ones((T, HEAD_SLOT), np.float32)
    sin_prev = np.zeros((T, HEAD_SLOT), np.float32)
    sin_next = np.zeros((T, HEAD_SLOT), np.float32)
    for axis, pos in enumerate((t // GRID_W, t % GRID_W)):
        ang = pos.astype(np.float32)[:, None] * inv_freq[None, :].astype(np.float32)
        c, s = np.cos(ang).astype(np.float32), np.sin(ang).astype(np.float32)
        lo = QK_NOPE + axis * half
        cos[:, lo:lo + n_freq] = c
        cos[:, lo + n_freq:lo + half] = c
        sin_next[:, lo:lo + n_freq] = -s
        sin_prev[:, lo + n_freq:lo + half] = s
    return jnp.asarray(cos), jnp.asarray(sin_prev), jnp.asarray(sin_next)


def _dft_tables(T):
    assert T == FFT_R * FFT_N
    n = np.arange(F_GROUP_DIM)
    ang_c = 2.0 * np.pi * ((n[:, None] * n[None, :]) % F_GROUP_DIM) / F_GROUP_DIM
    cc = np.concatenate([np.cos(ang_c), -np.sin(ang_c)], axis=1) / np.sqrt(F_GROUP_DIM)
    k1 = np.repeat(np.arange(FFT_R), FFT_N)
    n2 = np.tile(np.arange(FFT_N), FFT_R)
    ang_w = 2.0 * np.pi * (k1 * n2) / T
    twc = np.broadcast_to(np.cos(ang_w)[:, None], (T, F_GROUP_DIM))
    tws = np.broadcast_to(np.sin(ang_w)[:, None], (T, F_GROUP_DIM))
    m = np.arange(FFT_N)
    ang_2 = 2.0 * np.pi * ((m[:, None] * m[None, :]) % FFT_N) / FFT_N
    m2 = np.concatenate([np.cos(ang_2), np.sin(ang_2)], axis=1) / np.sqrt(T)
    return (jnp.asarray(cc, F32).astype(BF16), jnp.asarray(twc, F32), jnp.asarray(tws, F32),
            jnp.asarray(m2, F32).astype(BF16))


def kernel(x, c, ctx, c_ctx, w_mod, b_mod, norm1_g, w_in, q_a_norm_g, kv_a_norm_g, w_q_up, w_kv_up,
           q_norm_g, k_norm_g, w_o_attn, w_fourier, w_out, norm2_g, w_router, w_e_gate, w_e_up, w_e_down):
    B, T, D = x.shape
    assert w_mod.shape[0] == 1 and D == D_MODEL and T % 256 == 0
    H = MLA_HEADS
    cap = (CAPACITY_FACTOR * T) // N_EXPERTS

    rows = -(-(B + 1) // SUBLANES) * SUBLANES
    cc_in = jnp.concatenate([c, c_ctx[None, :], jnp.zeros((rows - B - 1, D), F32)], axis=0)
    mod = _mod_call(cc_in, w_mod[0], b_mod)
    sh1, sc1, g1, sh2, sc2, g2 = [mod[:B, i * D:(i + 1) * D].reshape(B, 1, D) for i in range(6)]
    csh1 = mod[B:B + 1, 0:D].reshape(1, 1, D)
    csc1 = mod[B:B + 1, D:2 * D].reshape(1, 1, D)

    wi = w_in[0]
    pe_cols = jnp.zeros((D, HEAD_SLOT), F32).at[:, QK_NOPE:QK_DIM].set(wi[:, OFF_KPE:OFF_F])
    w_in_p = jnp.concatenate([wi[:, OFF_Q:OFF_KPE], pe_cols, wi[:, OFF_F:N_IN]], axis=1).astype(BF16)
    wq_p = jnp.pad(w_q_up[0].reshape(Q_LORA, H, QK_DIM), ((0, 0), (0, 0), (0, HEAD_SLOT - QK_DIM)))
    wq_p = wq_p.reshape(Q_LORA, H * HEAD_SLOT).astype(BF16)
    wkv = w_kv_up[0].reshape(KV_LORA, H, QK_NOPE + V_DIM)
    wk_p = jnp.pad(wkv[:, :, :QK_NOPE], ((0, 0), (0, 0), (0, HEAD_SLOT - QK_NOPE)))
    wk_p = wk_p.reshape(KV_LORA, H * HEAD_SLOT).astype(BF16)
    wv_p = wkv[:, :, QK_NOPE:].reshape(KV_LORA, H * V_DIM).astype(BF16)
    qg_p = jnp.pad(q_norm_g[0] * (QK_DIM ** -0.5 * LOG2E), (0, HEAD_SLOT - QK_DIM)).reshape(1, HEAD_SLOT)
    kg_p = jnp.pad(k_norm_g[0], (0, HEAD_SLOT - QK_DIM)).reshape(1, HEAD_SLOT)
    kb = jnp.full((1, HEAD_SLOT), QK_DIM ** 0.5 * BOUND_SLACK, F32) * jnp.max(jnp.abs(k_norm_g[0]))
    qag = q_a_norm_g[0].reshape(1, Q_LORA)
    kvag = kv_a_norm_g[0].reshape(1, KV_LORA)
    g1n = norm1_g[0].reshape(1, D)
    g2n = norm2_g[0].reshape(1, D)
    cos_t, sp_t, sn_t = _rope_tables(T)

    q, k_lat, vt_lat, f, ga, gb = _proj_lat_call(x, g1n, sc1, sh1, w_in_p, qag, kvag, wq_p, wk_p, wv_p,
                                                 qg_p, kg_p, kb, cos_t, sp_t, sn_t, tm=256)
    k_ctx, vt_ctx = _proj_ctx_call(ctx, g1n, csc1, csh1, w_in_p, kvag, wk_p, wv_p, kg_p)
    attn = _attn_call(q, k_lat, k_ctx, vt_lat, vt_ctx, tq=512)
    four = _fourier_call(f, *_dft_tables(T))

    wr = jnp.pad(w_router[0], ((0, 0), (0, LANES - N_EXPERTS)))
    wrh = wr.astype(BF16)
    wrl = (wr - wrh.astype(F32)).astype(BF16)
    x1, h2t, aff = _merge_call(attn, four, ga, gb, x, g1, g2n, sc2, sh2, w_o_attn[0].astype(BF16),
                               w_fourier[0].astype(BF16), w_out[0].astype(BF16), wrh, wrl, tm=256)

    tri = jnp.asarray(np.triu(np.ones((LANES, LANES), np.float32)), BF16)
    t_ids = np.arange(T)
    addr = 64 * (t_ids // 8) + (t_ids % 8)
    pos = jnp.asarray(np.stack([addr // LANES, addr % LANES]).astype(np.float32), BF16)
    idx, gate = _select_call(aff, tri, pos, cap)
    h2t = h2t.reshape(B, T * (D // LANES), LANES)
    y = _moe_call(idx, gate, h2t, w_e_gate[0].astype(BF16), w_e_up[0].astype(BF16),
                  w_e_down[0].astype(BF16), T)
    return _final_call(x1, y, g2, tm=512)
```

```python
import functools

import numpy as np
import jax
import jax.numpy as jnp
from jax import lax
from jax.experimental import pallas as pl
from jax.experimental.pallas import tpu as pltpu

F32 = jnp.float32
BF16 = jnp.bfloat16

D_MODEL = 1024
GRID_W = 64
MLA_HEADS = 8
QK_NOPE = 64
QK_ROPE = 32
QK_DIM = QK_NOPE + QK_ROPE
V_DIM = 64
Q_LORA = 384
KV_LORA = 256
ROPE_THETA = 10000.0
F_GROUPS = 4
F_GROUP_DIM = 128
D_F = F_GROUPS * F_GROUP_DIM
OFF_Q = 0
OFF_KV = OFF_Q + Q_LORA
OFF_KPE = OFF_KV + KV_LORA
OFF_F = OFF_KPE + QK_ROPE
OFF_GA = OFF_F + D_F
OFF_GB = OFF_GA + D_MODEL
N_IN = OFF_GB + D_MODEL
N_EXPERTS = 16
D_EXPERT = 512
CAPACITY_FACTOR = 2
EPS = 1e-6

LANES = 128
SUBLANES = 8
MXU_TILE = 256
HEAD_SLOT = LANES
SHIFT_LANE = QK_DIM
BOUND_SLACK = 1.02
DEN_FLOOR = 2.0 ** -60
LOG2E = 1.4426950408889634
FFT_R = 8
FFT_N = 256
SEARCH_BITS = 3
SEARCH_DONE = 2.0 ** -30
SEARCH_MAX_ROUNDS = 56
KEY_CHUNK = 256
VMEM_LIMIT = 56 * 1024 * 1024

PC_Q = 0
PC_KV = PC_Q + Q_LORA
PC_PE = PC_KV + KV_LORA
PC_F = PC_PE + HEAD_SLOT
PC_GA = PC_F + D_F
PC_GB = PC_GA + D_MODEL
PC_END = PC_GB + D_MODEL


def _dot(a, b):
    return jnp.dot(a, b, preferred_element_type=F32)


def _dot_nt(a, b):
    return lax.dot_general(a, b, (((1,), (1,)), ((), ())), preferred_element_type=F32)


def _split2(a):
    hi = a.astype(BF16)
    lo = (a - hi.astype(F32)).astype(BF16)
    return hi, lo


def _split3(a):
    hi = a.astype(BF16)
    r = a - hi.astype(F32)
    mid = r.astype(BF16)
    lo = (r - mid.astype(F32)).astype(BF16)
    return hi, mid, lo


def _dot3(a, b):
    ah, al = _split2(a)
    bh, bl = _split2(b)
    return _dot(ah, bh) + (_dot(ah, bl) + _dot(al, bh))


def _sigmoid(x):
    return 1.0 / (1.0 + jnp.exp(-x))


def _params(sem):
    return pltpu.CompilerParams(dimension_semantics=sem, vmem_limit_bytes=VMEM_LIMIT)


def _mod_body(c_ref, w_ref, b_ref, o_ref):
    c = c_ref[...]
    s = c * _sigmoid(c)
    o_ref[...] = _dot3(s, w_ref[...]) + b_ref[...]


def _mod_call(cc, w_mod, b_mod):
    rows = cc.shape[0]
    n = w_mod.shape[1]
    tn = 1024
    return pl.pallas_call(
        _mod_body,
        grid=(n // tn,),
        in_specs=[
            pl.BlockSpec((rows, D_MODEL), lambda j: (0, 0)),
            pl.BlockSpec((D_MODEL, tn), lambda j: (0, j)),
            pl.BlockSpec((1, tn), lambda j: (0, j)),
        ],
        out_specs=pl.BlockSpec((rows, tn), lambda j: (0, j)),
        out_shape=jax.ShapeDtypeStruct((rows, n), F32),
        compiler_params=_params(("arbitrary",)),
        name="mod",
    )(cc, w_mod, b_mod)


def _rms_rows(x, n):
    return x * lax.rsqrt(jnp.sum(x * x, axis=-1, keepdims=True) * (1.0 / n) + EPS)


def _rope(x, cos, sin_prev, sin_next):
    return x * cos + pltpu.roll(x, 8, 1) * sin_prev + pltpu.roll(x, LANES - 8, 1) * sin_next


def _proj_body(*refs, latent):
    if latent:
        (x_ref, g_ref, sc_ref, sh_ref, win_ref, qag_ref, kvag_ref, wq_ref, wk_ref, wv_ref,
         qg_ref, kg_ref, kb_ref, cos_ref, sp_ref, sn_ref,
         q_out, k_out, v_out, f_out, ga_out, gb_out) = refs
    else:
        (x_ref, g_ref, sc_ref, sh_ref, win_ref, kvag_ref, wk_ref, wv_ref, kg_ref,
         k_out, v_out) = refs
    x = x_ref[0]
    h = _rms_rows(x, D_MODEL) * g_ref[...]
    h = h * (1.0 + sc_ref[0]) + sh_ref[0]
    hb = h.astype(BF16)
    shift_lane = lax.broadcasted_iota(jnp.int32, (x.shape[0], HEAD_SLOT), 1) == SHIFT_LANE

    if latent:
        cos = cos_ref[...]
        sp = sp_ref[...]
        sn = sn_ref[...]

    ckv = _rms_rows(_dot(hb, win_ref[:, PC_KV:PC_PE]), KV_LORA) * kvag_ref[...]
    ckvb = ckv.astype(BF16)
    pe = _dot(hb, win_ref[:, PC_PE:PC_F])
    kall = _dot(ckvb, wk_ref[...])
    kg = kg_ref[...]
    for hd in range(MLA_HEADS):
        kh = kall[:, hd * HEAD_SLOT:(hd + 1) * HEAD_SLOT] + pe
        kh = _rms_rows(kh, QK_DIM) * kg
        if latent:
            kh = _rope(kh, cos, sp, sn)
        k_out[0, hd] = jnp.where(shift_lane, 1.0, kh).astype(BF16)
    v = _dot(ckvb, wv_ref[...])
    for hp in range(MLA_HEADS // 2):
        v_out[0, hp] = v[:, hp * LANES:(hp + 1) * LANES].T.astype(BF16)

    if latent:
        cq = _rms_rows(_dot(hb, win_ref[:, PC_Q:PC_KV]), Q_LORA) * qag_ref[...]
        qall = _dot(cq.astype(BF16), wq_ref[...])
        qg = qg_ref[...]
        kb = kb_ref[...]
        for hd in range(MLA_HEADS):
            qh = qall[:, hd * HEAD_SLOT:(hd + 1) * HEAD_SLOT]
            qh = _rope(_rms_rows(qh, QK_DIM) * qg, cos, sp, sn)
            bound = jnp.sqrt(jnp.sum(qh * qh, axis=-1, keepdims=True)) * kb
            q_out[0, hd] = jnp.where(shift_lane, -bound, qh).astype(BF16)
        f_out[0] = _dot(hb, win_ref[:, PC_F:PC_GA]).astype(BF16)
        ga_out[0] = _sigmoid(_dot(hb, win_ref[:, PC_GA:PC_GB])).astype(BF16)
        gb_out[0] = _sigmoid(_dot(hb, win_ref[:, PC_GB:PC_END])).astype(BF16)


def _full(shape):
    nd = len(shape)
    return pl.BlockSpec(shape, lambda *_: (0,) * nd)


def _proj_lat_call(x, g1n, sc1, sh1, w_in_p, qag, kvag, wq_p, wk_p, wv_p, qg_p, kg_p, kb, cos_t, sp_t, sn_t, tm):
    B, T, _ = x.shape
    H = MLA_HEADS
    tok = lambda w: pl.BlockSpec((1, tm, w), lambda b, j: (b, j, 0))
    per_b = pl.BlockSpec((1, 1, D_MODEL), lambda b, j: (b, 0, 0))
    tab = pl.BlockSpec((tm, HEAD_SLOT), lambda b, j: (j, 0))
    return pl.pallas_call(
        functools.partial(_proj_body, latent=True),
        grid=(B, T // tm),
        in_specs=[tok(D_MODEL), _full(g1n.shape), per_b, per_b, _full(w_in_p.shape), _full(qag.shape),
                  _full(kvag.shape), _full(wq_p.shape), _full(wk_p.shape), _full(wv_p.shape),
                  _full(qg_p.shape), _full(kg_p.shape), _full(kb.shape), tab, tab, tab],
        out_specs=[
            pl.BlockSpec((1, H, tm, HEAD_SLOT), lambda b, j: (b, 0, j, 0)),
            pl.BlockSpec((1, H, tm, HEAD_SLOT), lambda b, j: (b, 0, j, 0)),
            pl.BlockSpec((1, H // 2, LANES, tm), lambda b, j: (b, 0, 0, j)),
            tok(D_F), tok(D_MODEL), tok(D_MODEL),
        ],
        out_shape=[
            jax.ShapeDtypeStruct((B, H, T, HEAD_SLOT), BF16),
            jax.ShapeDtypeStruct((B, H, T, HEAD_SLOT), BF16),
            jax.ShapeDtypeStruct((B, H // 2, LANES, T), BF16),
            jax.ShapeDtypeStruct((B, T, D_F), BF16),
            jax.ShapeDtypeStruct((B, T, D_MODEL), BF16),
            jax.ShapeDtypeStruct((B, T, D_MODEL), BF16),
        ],
        compiler_params=_params(("parallel", "parallel")),
        name="proj_lat",
    )(x, g1n, sc1, sh1, w_in_p, qag, kvag, wq_p, wk_p, wv_p, qg_p, kg_p, kb, cos_t, sp_t, sn_t)


def _proj_ctx_call(ctx, g1n, csc1, csh1, w_in_p, kvag, wk_p, wv_p, kg_p):
    B, TC, _ = ctx.shape
    H = MLA_HEADS
    shared = pl.BlockSpec((1, 1, D_MODEL), lambda b: (0, 0, 0))
    return pl.pallas_call(
        functools.partial(_proj_body, latent=False),
        grid=(B,),
        in_specs=[pl.BlockSpec((1, TC, D_MODEL), lambda b: (b, 0, 0)), _full(g1n.shape), shared, shared,
                  _full(w_in_p.shape), _full(kvag.shape), _full(wk_p.shape), _full(wv_p.shape),
                  _full(kg_p.shape)],
        out_specs=[
            pl.BlockSpec((1, H, TC, HEAD_SLOT), lambda b: (b, 0, 0, 0)),
            pl.BlockSpec((1, H // 2, LANES, TC), lambda b: (b, 0, 0, 0)),
        ],
        out_shape=[
            jax.ShapeDtypeStruct((B, H, TC, HEAD_SLOT), BF16),
            jax.ShapeDtypeStruct((B, H // 2, LANES, TC), BF16),
        ],
        compiler_params=_params(("parallel",)),
        name="proj_ctx",
    )(ctx, g1n, csc1, csh1, w_in_p, kvag, wk_p, wv_p, kg_p)


def _attn_body(q_ref, kl_ref, kc_ref, vl_ref, vc_ref, o_ref):
    tq = q_ref.shape[2]
    T = kl_ref.shape[2]
    nck = T // KEY_CHUNK
    row = lax.broadcasted_iota(jnp.int32, (LANES, tq), 0)

    def probs(hd):
        q = q_ref[0, hd]
        den = jnp.zeros((1, tq), F32)
        ps = []
        for c in range(nck + 1):
            k = kc_ref[0, hd] if c == nck else kl_ref[0, hd, c * KEY_CHUNK:(c + 1) * KEY_CHUNK, :]
            e = jnp.exp2(_dot_nt(k, q))
            den = den + jnp.sum(e, axis=0, keepdims=True)
            ps.append(e.astype(BF16))
        return ps, den

    def weighted_values(hd, ps):
        hp = hd // 2
        o = _dot(vc_ref[0, hp], ps[nck])
        for c in range(nck):
            o = o + _dot(vl_ref[0, hp, :, c * KEY_CHUNK:(c + 1) * KEY_CHUNK], ps[c])
        return o

    outs, dens, prev = [], [], None
    for hd in range(MLA_HEADS + 1):
        cur = probs(hd) if hd < MLA_HEADS else None
        if prev is not None:
            outs.append(weighted_values(hd - 1, prev[0]) / prev[1])
            dens.append(prev[1])
        prev = cur
    for hp in range(MLA_HEADS // 2):
        o_ref[0, hp] = jnp.where(row < V_DIM, outs[2 * hp], outs[2 * hp + 1]).astype(BF16)
    min_den = functools.reduce(jnp.minimum, dens)

    @pl.when(jnp.min(min_den) < DEN_FLOOR)
    def _():
        def pair(hp, carry):
            res = []
            for hd in (2 * hp, 2 * hp + 1):
                q = q_ref[0, hd]
                sl = _dot_nt(kl_ref[0, hd], q)
                sc = _dot_nt(kc_ref[0, hd], q)
                m = jnp.maximum(jnp.max(sl, axis=0, keepdims=True), jnp.max(sc, axis=0, keepdims=True))
                el = jnp.exp2(sl - m)
                ec = jnp.exp2(sc - m)
                den = jnp.sum(el, axis=0, keepdims=True) + jnp.sum(ec, axis=0, keepdims=True)
                o = _dot(vl_ref[0, hp], el.astype(BF16)) + _dot(vc_ref[0, hp], ec.astype(BF16))
                res.append(o / den)
            o_ref[0, hp] = jnp.where(row < V_DIM, res[0], res[1]).astype(BF16)
            return carry

        lax.fori_loop(0, MLA_HEADS // 2, pair, 0)


def _attn_call(q, k_lat, k_ctx, vt_lat, vt_ctx, tq):
    B, H, T, _ = q.shape
    TC = k_ctx.shape[2]
    return pl.pallas_call(
        _attn_body,
        grid=(B, T // tq),
        in_specs=[
            pl.BlockSpec((1, H, tq, HEAD_SLOT), lambda b, j: (b, 0, j, 0)),
            pl.BlockSpec((1, H, T, HEAD_SLOT), lambda b, j: (b, 0, 0, 0)),
            pl.BlockSpec((1, H, TC, HEAD_SLOT), lambda b, j: (b, 0, 0, 0)),
            pl.BlockSpec((1, H // 2, LANES, T), lambda b, j: (b, 0, 0, 0)),
            pl.BlockSpec((1, H // 2, LANES, TC), lambda b, j: (b, 0, 0, 0)),
        ],
        out_specs=pl.BlockSpec((1, H // 2, LANES, tq), lambda b, j: (b, 0, 0, j)),
        out_shape=jax.ShapeDtypeStruct((B, H // 2, LANES, T), BF16),
        compiler_params=_params(("parallel", "parallel")),
        name="attn",
    )(q, k_lat, k_ctx, vt_lat, vt_ctx)


def _cadd(a, b):
    return a[0] + b[0], a[1] + b[1]


def _csub(a, b):
    return a[0] - b[0], a[1] - b[1]


def _cmul_neg_i(a):
    return a[1], -a[0]


def _cmul_pos_i(a):
    return -a[1], a[0]


def _fft8(u):
    r = 0.7071067811865476
    a0, a1 = _cadd(u[0], u[4]), _csub(u[0], u[4])
    a2, a3 = _cadd(u[2], u[6]), _csub(u[2], u[6])
    a4, a5 = _cadd(u[1], u[5]), _csub(u[1], u[5])
    a6, a7 = _cadd(u[3], u[7]), _csub(u[3], u[7])
    e0, e2 = _cadd(a0, a2), _csub(a0, a2)
    e1, e3 = _cadd(a1, _cmul_neg_i(a3)), _cadd(a1, _cmul_pos_i(a3))
    o0, o2 = _cadd(a4, a6), _csub(a4, a6)
    o1, o3 = _cadd(a5, _cmul_neg_i(a7)), _cadd(a5, _cmul_pos_i(a7))
    wo1 = (r * (o1[0] + o1[1]), r * (o1[1] - o1[0]))
    wo3 = (r * (o3[1] - o3[0]), -r * (o3[0] + o3[1]))
    return [_cadd(e0, o0), _cadd(e1, wo1), _cadd(e2, _cmul_neg_i(o2)), _cadd(e3, wo3),
            _csub(e0, o0), _csub(e1, wo1), _cadd(e2, _cmul_pos_i(o2)), _csub(e3, wo3)]


def _fourier_body(f_ref, cc_ref, twc_ref, tws_ref, m2_ref, o_ref, b_ref, y_ref):
    for g in range(F_GROUPS):
        cols = slice(g * F_GROUP_DIM, (g + 1) * F_GROUP_DIM)
        y = _dot(f_ref[0, :, cols], cc_ref[...])
        u = [(y[j * FFT_N:(j + 1) * FFT_N, :F_GROUP_DIM], y[j * FFT_N:(j + 1) * FFT_N, F_GROUP_DIM:])
             for j in range(FFT_R)]
        a = _fft8(u)
        for k1 in range(FFT_R):
            ar, ai = a[k1]
            if k1 > 0:
                c = twc_ref[k1 * FFT_N:(k1 + 1) * FFT_N, :]
                s = tws_ref[k1 * FFT_N:(k1 + 1) * FFT_N, :]
                ar, ai = ar * c + ai * s, ai * c - ar * s
            b_ref[k1, 0:FFT_N, cols] = ar.astype(BF16)
            b_ref[k1, FFT_N:2 * FFT_N, cols] = ai.astype(BF16)
    for k1 in range(FFT_R):
        y = _dot(m2_ref[...], b_ref[k1])
        for g in range(F_GROUPS):
            y_ref[g, pl.ds(k1, FFT_N, stride=FFT_R), :] = y[:, g * F_GROUP_DIM:(g + 1) * F_GROUP_DIM]
    for g in range(F_GROUPS):
        o_ref[0, :, g * F_GROUP_DIM:(g + 1) * F_GROUP_DIM] = y_ref[g].astype(BF16)


def _fourier_call(f, cc, twc, tws, m2):
    B, T, _ = f.shape
    return pl.pallas_call(
        _fourier_body,
        grid=(B,),
        in_specs=[pl.BlockSpec((1, T, D_F), lambda b: (b, 0, 0)), _full(cc.shape), _full(twc.shape),
                  _full(tws.shape), _full(m2.shape)],
        out_specs=pl.BlockSpec((1, T, D_F), lambda b: (b, 0, 0)),
        out_shape=jax.ShapeDtypeStruct((B, T, D_F), BF16),
        scratch_shapes=[pltpu.VMEM((FFT_R, 2 * FFT_N, D_F), BF16), pltpu.VMEM((F_GROUPS, T, F_GROUP_DIM), F32)],
        compiler_params=_params(("parallel",)),
        name="fourier",
    )(f, cc, twc, tws, m2)


def _merge_body(at_ref, fo_ref, ga_ref, gb_ref, x_ref, g1_ref, g2n_ref, sc2_ref, sh2_ref,
                wo_ref, wf_ref, wout_ref, wrh_ref, wrl_ref, x1_out, h2_out, aff_out):
    tm = x_ref.shape[1]
    attn_t = jnp.concatenate([at_ref[0, hp] for hp in range(MLA_HEADS // 2)], axis=0)
    a = lax.dot_general(attn_t, wo_ref[...], (((0,), (0,)), ((), ())), preferred_element_type=F32)
    fo = _dot(fo_ref[0], wf_ref[...])
    mix = ga_ref[0].astype(F32) * a + gb_ref[0].astype(F32) * fo
    y = _dot(mix.astype(BF16), wout_ref[...])
    x1 = x_ref[0] + g1_ref[0] * y
    x1_out[0] = x1
    h2 = _rms_rows(x1, D_MODEL) * g2n_ref[...]
    h2 = h2 * (1.0 + sc2_ref[0]) + sh2_ref[0]
    for c in range(D_MODEL // LANES):
        h2_out[0, pl.ds(c, tm, stride=D_MODEL // LANES), :] = h2[:, c * LANES:(c + 1) * LANES]
    hh, hl = _split2(h2)
    logits = _dot(hh, wrh_ref[...]) + (_dot(hh, wrl_ref[...]) + _dot(hl, wrh_ref[...]))
    lane = lax.broadcasted_iota(jnp.int32, logits.shape, 1)
    logits = jnp.where(lane < N_EXPERTS, logits, -1e30)
    ex = jnp.exp(logits - jnp.max(logits, axis=-1, keepdims=True))
    aff_out[0] = ex / jnp.sum(ex, axis=-1, keepdims=True)


def _merge_call(attn, four, ga, gb, x, g1, g2n, sc2, sh2, wo, wf, wout, wrh, wrl, tm):
    B, T, _ = x.shape
    H = MLA_HEADS
    tok = lambda w: pl.BlockSpec((1, tm, w), lambda b, j: (b, j, 0))
    per_b = pl.BlockSpec((1, 1, D_MODEL), lambda b, j: (b, 0, 0))
    nchunk = D_MODEL // LANES
    return pl.pallas_call(
        _merge_body,
        grid=(B, T // tm),
        in_specs=[
            pl.BlockSpec((1, H // 2, LANES, tm), lambda b, j: (b, 0, 0, j)),
            tok(D_F), tok(D_MODEL), tok(D_MODEL), tok(D_MODEL), per_b, _full(g2n.shape), per_b, per_b,
            _full(wo.shape), _full(wf.shape), _full(wout.shape), _full(wrh.shape), _full(wrl.shape),
        ],
        out_specs=[
            tok(D_MODEL),
            pl.BlockSpec((1, tm * nchunk, LANES), lambda b, j: (b, j, 0)),
            tok(LANES),
        ],
        out_shape=[
            jax.ShapeDtypeStruct((B, T, D_MODEL), F32),
            jax.ShapeDtypeStruct((B, T * nchunk, LANES), F32),
            jax.ShapeDtypeStruct((B, T, LANES), F32),
        ],
        compiler_params=_params(("parallel", "parallel")),
        name="merge",
    )(attn, four, ga, gb, x, g1, g2n, sc2, sh2, wo, wf, wout, wrh, wrl)


def _prefix_excl(m, tri):
    outs = []
    run = jnp.zeros((m.shape[0], 1), F32)
    for blk in range(m.shape[1] // LANES):
        mb = m[:, blk * LANES:(blk + 1) * LANES]
        inc = _dot(mb.astype(BF16), tri)
        outs.append(inc - mb + run)
        run = run + inc[:, LANES - 1:LANES]
    return jnp.concatenate(outs, axis=1)


def _select_body(aff_ref, tri_ref, pos_ref, idx_out, gate_out, *, cap):
    T = aff_ref.shape[1]
    aff = aff_ref[0].T[0:N_EXPERTS]

    def search(carry):
        base, step, rounds, _ = carry
        thr = base
        for j in range(1, 2 ** SEARCH_BITS):
            cand = base + float(j) * step
            cnt = jnp.sum((aff >= cand).astype(F32), axis=-1, keepdims=True)
            thr = jnp.where(cnt >= float(cap), cand, thr)
        settled = jnp.where((thr > 0.0) & (step < thr * SEARCH_DONE), 1.0, 0.0)
        return thr, step * (0.5 ** SEARCH_BITS), rounds + 1, (jnp.min(settled) < 1.0).astype(jnp.int32)

    init = (jnp.zeros((N_EXPERTS, 1), F32), jnp.full((N_EXPERTS, 1), 2.0 * 0.5 ** SEARCH_BITS, F32),
            jnp.int32(0), jnp.int32(1))
    thr = lax.while_loop(lambda c: (c[3] > 0) & (c[2] < SEARCH_MAX_ROUNDS), search, init)[0]
    gt = (aff > thr).astype(F32)
    eq = (aff == thr).astype(F32)
    need = cap - jnp.sum(gt, axis=-1, keepdims=True)
    tri = tri_ref[...]
    sel = gt + eq * (_prefix_excl(eq, tri) < need).astype(F32)
    slot = jnp.where(sel > 0.0, _prefix_excl(sel, tri), -1.0)

    pos = pos_ref[...]
    srow = lax.broadcasted_iota(jnp.int32, (cap, T), 0).astype(F32)
    zero = jnp.zeros((3, T), BF16)
    for e in range(N_EXPERTS):
        onehot = jnp.where(srow == slot[e:e + 1, :], 1.0, 0.0).astype(BF16)
        ah, am, al = _split3(aff[e:e + 1, :])
        vals = jnp.concatenate([pos, ah, am, al, zero], axis=0)
        res = _dot_nt(vals, onehot)
        idx_out[0, e:e + 1, :] = (res[0:1] * float(LANES) + res[1:2]).astype(jnp.int32)
        gate_out[0, e:e + 1, :] = res[2:3] + (res[3:4] + res[4:5])


def _select_call(aff, tri, pos, cap):
    B, T, _ = aff.shape
    return pl.pallas_call(
        functools.partial(_select_body, cap=cap),
        grid=(B,),
        in_specs=[pl.BlockSpec((1, T, LANES), lambda b: (b, 0, 0)), _full(tri.shape), _full(pos.shape)],
        out_specs=[pl.BlockSpec((1, N_EXPERTS, cap), lambda b: (b, 0, 0)),
                   pl.BlockSpec((1, N_EXPERTS, cap), lambda b: (b, 0, 0))],
        out_shape=[jax.ShapeDtypeStruct((B, N_EXPERTS, cap), jnp.int32),
                   jax.ShapeDtypeStruct((B, N_EXPERTS, cap), F32)],
        compiler_params=_params(("parallel",)),
        name="select",
    )(aff, tri, pos)


def _moe_body(ip_ref, i0_ref, i1_ref, in_ref, g0_ref, g1_ref, h2_ref, wg_ref, wu_ref, wd_ref, acc_ref,
              xg_a, xg_b, ye_a, ye_b, *, cap, stride):
    k = pl.program_id(1)
    nchunk = D_MODEL // LANES
    group = 8

    def gather(i_ref, xg_ref):
        def rows(s0):
            for s in range(s0, s0 + group):
                base = pl.multiple_of(i_ref[0, 0, 0, s], SUBLANES)
                tile = h2_ref[0, pl.ds(base, nchunk), :]
                xg_ref[pl.ds(s, nchunk, stride=stride), :] = tile
            return tile
        return [functools.partial(rows, s0) for s0 in range(0, cap, group)]

    def scatter(i_ref, ye_ref):
        def rows(s0):
            bases, news = [], []
            for s in range(s0, s0 + group):
                base = pl.multiple_of(i_ref[0, 0, 0, s], SUBLANES)
                bases.append(base)
                news.append(acc_ref[0, pl.ds(base, nchunk), :] + ye_ref[pl.ds(s, nchunk, stride=stride), :])
            for base, new in zip(bases, news):
                acc_ref[0, pl.ds(base, nchunk), :] = new
            return news[-1]
        return [functools.partial(rows, s0) for s0 in range(0, cap, group)]

    def expert(j, g_ref, xg_ref, ye_ref):
        state = {}
        n_mid = D_EXPERT // MXU_TILE
        n_out = D_MODEL // MXU_TILE

        def load(zero):
            xb = jnp.concatenate([xg_ref[c * stride:c * stride + cap, :] for c in range(nchunk)], axis=1)
            state["x"] = xb.astype(BF16)
            state["gate"] = jnp.broadcast_to(g_ref[0, 0], (LANES, cap)).T[:, 0:1]
            state["h"] = []

        def mid(n, zero):
            cols = slice(n * MXU_TILE, (n + 1) * MXU_TILE)
            a = _dot(state["x"], wg_ref[j, :, cols]) + zero
            u = _dot(state["x"], wu_ref[j, :, cols])
            state["h"].append((a * _sigmoid(a) * u).astype(BF16))

        def out(n, zero):
            cols = slice(n * MXU_TILE, (n + 1) * MXU_TILE)
            ye = _dot(state["h"][0], wd_ref[j, 0:MXU_TILE, cols])
            for m in range(1, n_mid):
                ye = ye + _dot(state["h"][m], wd_ref[j, m * MXU_TILE:(m + 1) * MXU_TILE, cols])
            ye = ye * state["gate"] + zero
            for c in range(MXU_TILE // LANES):
                cc = n * (MXU_TILE // LANES) + c
                ye_ref[cc * stride:cc * stride + cap, :] = ye[:, c * LANES:(c + 1) * LANES]

        return ([load] + [functools.partial(mid, n) for n in range(n_mid)]
                + [functools.partial(out, n) for n in range(n_out)])

    def paced(main, side):
        zero = jnp.zeros((1, MXU_TILE), F32)
        done = 0
        for i, piece in enumerate(main):
            piece(zero)
            upto = (len(side) * (i + 1)) // (len(main) - 1) if i + 1 < len(main) else len(side)
            tiles = [s() for s in side[done:upto]]
            done = upto
            if tiles:
                bits = pltpu.bitcast(functools.reduce(jnp.add, tiles), jnp.uint32)
                z = pltpu.bitcast(lax.shift_right_logical(bits, jnp.uint32(32)), F32)[0:1, :]
                zero = jnp.concatenate([z] * (MXU_TILE // LANES), axis=1)

    def zipped(a, b):
        return [f for pair in zip(a, b) for f in pair]

    @pl.when(k == 0)
    def _():
        acc_ref[...] = jnp.zeros_like(acc_ref)
        ye_b[...] = jnp.zeros_like(ye_b)
        for piece in gather(i0_ref, xg_a):
            piece()

    paced(expert(0, g0_ref, xg_a, ye_a), zipped(scatter(ip_ref, ye_b), gather(i1_ref, xg_b)))
    paced(expert(1, g1_ref, xg_b, ye_b), zipped(scatter(i0_ref, ye_a), gather(in_ref, xg_a)))

    @pl.when(k == pl.num_programs(1) - 1)
    def _():
        for piece in scatter(i1_ref, ye_b):
            piece()


def _moe_call(idx, gate, h2t, wg, wu, wd, T):
    B, E, cap = idx.shape
    nchunk = D_MODEL // LANES
    stride = cap + SUBLANES
    buf = pltpu.VMEM((nchunk * stride, LANES), F32)
    idx4 = idx.reshape(B, E, 1, cap)
    gate4 = gate.reshape(B, E, 1, cap)

    def rows(expert_of_step, memory_space=None):
        spec = pl.BlockSpec((1, 1, 1, cap), lambda b, k: (b, expert_of_step(k), 0, 0))
        return spec if memory_space is None else pl.BlockSpec(
            (1, 1, 1, cap), lambda b, k: (b, expert_of_step(k), 0, 0), memory_space=memory_space)

    return pl.pallas_call(
        functools.partial(_moe_body, cap=cap, stride=stride),
        grid=(B, E // 2),
        in_specs=[
            rows(lambda k: jnp.maximum(2 * k - 1, 0), pltpu.SMEM),
            rows(lambda k: 2 * k, pltpu.SMEM),
            rows(lambda k: 2 * k + 1, pltpu.SMEM),
            rows(lambda k: jnp.minimum(2 * k + 2, E - 1), pltpu.SMEM),
            rows(lambda k: 2 * k),
            rows(lambda k: 2 * k + 1),
            pl.BlockSpec((1, T * nchunk, LANES), lambda b, k: (b, 0, 0)),
            pl.BlockSpec((2, D_MODEL, D_EXPERT), lambda b, k: (k, 0, 0)),
            pl.BlockSpec((2, D_MODEL, D_EXPERT), lambda b, k: (k, 0, 0)),
            pl.BlockSpec((2, D_EXPERT, D_MODEL), lambda b, k: (k, 0, 0)),
        ],
        out_specs=pl.BlockSpec((1, T * nchunk, LANES), lambda b, k: (b, 0, 0)),
        out_shape=jax.ShapeDtypeStruct((B, T * nchunk, LANES), F32),
        scratch_shapes=[buf, buf, buf, buf],
        compiler_params=_params(("parallel", "arbitrary")),
        name="moe",
    )(idx4, idx4, idx4, idx4, gate4, gate4, h2t, wg, wu, wd)


def _final_body(x1_ref, y_ref, g2_ref, o_ref):
    tm = x1_ref.shape[1]
    for c in range(D_MODEL // LANES):
        cols = slice(c * LANES, (c + 1) * LANES)
        y = y_ref[0, pl.ds(c, tm, stride=D_MODEL // LANES), :]
        o_ref[0, :, cols] = x1_ref[0, :, cols] + g2_ref[0, :, cols] * y


def _final_call(x1, y, g2, tm):
    B, T, _ = x1.shape
    nchunk = D_MODEL // LANES
    tok = pl.BlockSpec((1, tm, D_MODEL), lambda b, j: (b, j, 0))
    return pl.pallas_call(
        _final_body,
        grid=(B, T // tm),
        in_specs=[tok,
                  pl.BlockSpec((1, tm * nchunk, LANES), lambda b, j: (b, j, 0)),
                  pl.BlockSpec((1, 1, D_MODEL), lambda b, j: (b, 0, 0))],
        out_specs=tok,
        out_shape=jax.ShapeDtypeStruct((B, T, D_MODEL), F32),
        compiler_params=_params(("parallel", "parallel")),
        name="final",
    )(x1, y, g2)


def _rope_tables(T):
    half = QK_ROPE // 2
    n_freq = half // 2
    inv_freq = 1.0 / (ROPE_THETA ** (np.arange(n_freq, dtype=np.float32) / n_freq))
    t = np.arange(T)
    cos = np.ones((T, HEAD_SLOT), np.float32)
    sin_prev = np.zeros((T, HEAD_SLOT), np.float32)
    sin_next = np.zeros((T, HEAD_SLOT), np.float32)
    for axis, pos in enumerate((t // GRID_W, t % GRID_W)):
        ang = pos.astype(np.float32)[:, None] * inv_freq[None, :].astype(np.float32)
        c, s = np.cos(ang).astype(np.float32), np.sin(ang).astype(np.float32)
        lo = QK_NOPE + axis * half
        cos[:, lo:lo + n_freq] = c
        cos[:, lo + n_freq:lo + half] = c
        sin_next[:, lo:lo + n_freq] = -s
        sin_prev[:, lo + n_freq:lo + half] = s
    return jnp.asarray(cos), jnp.asarray(sin_prev), jnp.asarray(sin_next)


def _dft_tables(T):
    assert T == FFT_R * FFT_N
    n = np.arange(F_GROUP_DIM)
    ang_c = 2.0 * np.pi * ((n[:, None] * n[None, :]) % F_GROUP_DIM) / F_GROUP_DIM
    cc = np.concatenate([np.cos(ang_c), -np.sin(ang_c)], axis=1) / np.sqrt(F_GROUP_DIM)
    k1 = np.repeat(np.arange(FFT_R), FFT_N)
    n2 = np.tile(np.arange(FFT_N), FFT_R)
    ang_w = 2.0 * np.pi * (k1 * n2) / T
    twc = np.broadcast_to(np.cos(ang_w)[:, None], (T, F_GROUP_DIM))
    tws = np.broadcast_to(np.sin(ang_w)[:, None], (T, F_GROUP_DIM))
    m = np.arange(FFT_N)
    ang_2 = 2.0 * np.pi * ((m[:, None] * m[None, :]) % FFT_N) / FFT_N
    m2 = np.concatenate([np.cos(ang_2), np.sin(ang_2)], axis=1) / np.sqrt(T)
    return (jnp.asarray(cc, F32).astype(BF16), jnp.asarray(twc, F32), jnp.asarray(tws, F32),
            jnp.asarray(m2, F32).astype(BF16))


def kernel(x, c, ctx, c_ctx, w_mod, b_mod, norm1_g, w_in, q_a_norm_g, kv_a_norm_g, w_q_up, w_kv_up,
           q_norm_g, k_norm_g, w_o_attn, w_fourier, w_out, norm2_g, w_router, w_e_gate, w_e_up, w_e_down):
    B, T, D = x.shape
    assert w_mod.shape[0] == 1 and D == D_MODEL and T % 256 == 0
    H = MLA_HEADS
    cap = (CAPACITY_FACTOR * T) // N_EXPERTS

    rows = -(-(B + 1) // SUBLANES) * SUBLANES
    cc_in = jnp.concatenate([c, c_ctx[None, :], jnp.zeros((rows - B - 1, D), F32)], axis=0)
    mod = _mod_call(cc_in, w_mod[0], b_mod)
    sh1, sc1, g1, sh2, sc2, g2 = [mod[:B, i * D:(i + 1) * D].reshape(B, 1, D) for i in range(6)]
    csh1 = mod[B:B + 1, 0:D].reshape(1, 1, D)
    csc1 = mod[B:B + 1, D:2 * D].reshape(1, 1, D)

    wi = w_in[0]
    pe_cols = jnp.zeros((D, HEAD_SLOT), F32).at[:, QK_NOPE:QK_DIM].set(wi[:, OFF_KPE:OFF_F])
    w_in_p = jnp.concatenate([wi[:, OFF_Q:OFF_KPE], pe_cols, wi[:, OFF_F:N_IN]], axis=1).astype(BF16)
    wq_p = jnp.pad(w_q_up[0].reshape(Q_LORA, H, QK_DIM), ((0, 0), (0, 0), (0, HEAD_SLOT - QK_DIM)))
    wq_p = wq_p.reshape(Q_LORA, H * HEAD_SLOT).astype(BF16)
    wkv = w_kv_up[0].reshape(KV_LORA, H, QK_NOPE + V_DIM)
    wk_p = jnp.pad(wkv[:, :, :QK_NOPE], ((0, 0), (0, 0), (0, HEAD_SLOT - QK_NOPE)))
    wk_p = wk_p.reshape(KV_LORA, H * HEAD_SLOT).astype(BF16)
    wv_p = wkv[:, :, QK_NOPE:].reshape(KV_LORA, H * V_DIM).astype(BF16)
    qg_p = jnp.pad(q_norm_g[0] * (QK_DIM ** -0.5 * LOG2E), (0, HEAD_SLOT - QK_DIM)).reshape(1, HEAD_SLOT)
    kg_p = jnp.pad(k_norm_g[0], (0, HEAD_SLOT - QK_DIM)).reshape(1, HEAD_SLOT)
    kb = jnp.full((1, HEAD_SLOT), QK_DIM ** 0.5 * BOUND_SLACK, F32) * jnp.max(jnp.abs(k_norm_g[0]))
    qag = q_a_norm_g[0].reshape(1, Q_LORA)
    kvag = kv_a_norm_g[0].reshape(1, KV_LORA)
    g1n = norm1_g[0].reshape(1, D)
    g2n = norm2_g[0].reshape(1, D)
    cos_t, sp_t, sn_t = _rope_tables(T)

    q, k_lat, vt_lat, f, ga, gb = _proj_lat_call(x, g1n, sc1, sh1, w_in_p, qag, kvag, wq_p, wk_p, wv_p,
                                                 qg_p, kg_p, kb, cos_t, sp_t, sn_t, tm=256)
    k_ctx, vt_ctx = _proj_ctx_call(ctx, g1n, csc1, csh1, w_in_p, kvag, wk_p, wv_p, kg_p)
    attn = _attn_call(q, k_lat, k_ctx, vt_lat, vt_ctx, tq=512)
    four = _fourier_call(f, *_dft_tables(T))

    wr = jnp.pad(w_router[0], ((0, 0), (0, LANES - N_EXPERTS)))
    wrh = wr.astype(BF16)
    wrl = (wr - wrh.astype(F32)).astype(BF16)
    x1, h2t, aff = _merge_call(attn, four, ga, gb, x, g1, g2n, sc2, sh2, w_o_attn[0].astype(BF16),
                               w_fourier[0].astype(BF16), w_out[0].astype(BF16), wrh, wrl, tm=256)

    tri = jnp.asarray(np.triu(np.ones((LANES, LANES), np.float32)), BF16)
    addr = np.arange(T) * (D // LANES)
    pos = jnp.asarray(np.stack([addr // LANES, addr % LANES]).astype(np.float32), BF16)
    idx, gate = _select_call(aff, tri, pos, cap)
    y = _moe_call(idx, gate, h2t, w_e_gate[0].astype(BF16), w_e_up[0].astype(BF16),
                  w_e_down[0].astype(BF16), T)
    return _final_call(x1, y, g2, tm=512)
```

```python
import functools

import numpy as np
import jax
import jax.numpy as jnp
from jax import lax
from jax.experimental import pallas as pl
from jax.experimental.pallas import tpu as pltpu

F32 = jnp.float32
BF16 = jnp.bfloat16

D_MODEL = 1024
GRID_W = 64
MLA_HEADS = 8
QK_NOPE = 64
QK_ROPE = 32
QK_DIM = QK_NOPE + QK_ROPE
V_DIM = 64
Q_LORA = 384
KV_LORA = 256
ROPE_THETA = 10000.0
F_GROUPS = 4
F_GROUP_DIM = 128
D_F = F_GROUPS * F_GROUP_DIM
OFF_Q = 0
OFF_KV = OFF_Q + Q_LORA
OFF_KPE = OFF_KV + KV_LORA
OFF_F = OFF_KPE + QK_ROPE
OFF_GA = OFF_F + D_F
OFF_GB = OFF_GA + D_MODEL
N_IN = OFF_GB + D_MODEL
N_EXPERTS = 16
D_EXPERT = 512
CAPACITY_FACTOR = 2
EPS = 1e-6

LANES = 128
SUBLANES = 8
MXU_TILE = 256
HEAD_SLOT = LANES
SHIFT_LANE = 48
BOUND_SLACK = 1.02
DEN_FLOOR = 2.0 ** -60
LOG2E = 1.4426950408889634
FFT_R = 8
FFT_N = 256
SEARCH_BITS = 3
SEARCH_DONE = 2.0 ** -30
SEARCH_MAX_ROUNDS = 56
KEY_CHUNK = 256
VMEM_LIMIT = 56 * 1024 * 1024

PC_Q = 0
PC_KV = PC_Q + Q_LORA
PC_PE = PC_KV + KV_LORA
PC_F = PC_PE + HEAD_SLOT
PC_GA = PC_F + D_F
PC_GB = PC_GA + D_MODEL
PC_END = PC_GB + D_MODEL


def _dot(a, b):
    return jnp.dot(a, b, preferred_element_type=F32)


def _dot_nt(a, b):
    return lax.dot_general(a, b, (((1,), (1,)), ((), ())), preferred_element_type=F32)


def _split2(a):
    hi = a.astype(BF16)
    lo = (a - hi.astype(F32)).astype(BF16)
    return hi, lo


def _split3(a):
    hi = a.astype(BF16)
    r = a - hi.astype(F32)
    mid = r.astype(BF16)
    lo = (r - mid.astype(F32)).astype(BF16)
    return hi, mid, lo


def _dot3(a, b):
    ah, al = _split2(a)
    bh, bl = _split2(b)
    return _dot(ah, bh) + (_dot(ah, bl) + _dot(al, bh))


def _sigmoid(x):
    return 1.0 / (1.0 + jnp.exp(-x))


def _params(sem):
    return pltpu.CompilerParams(dimension_semantics=sem, vmem_limit_bytes=VMEM_LIMIT)


def _mod_body(c_ref, w_ref, b_ref, o_ref):
    c = c_ref[...]
    s = c * _sigmoid(c)
    o_ref[...] = _dot3(s, w_ref[...]) + b_ref[...]


def _mod_call(cc, w_mod, b_mod):
    rows = cc.shape[0]
    n = w_mod.shape[1]
    tn = 1024
    return pl.pallas_call(
        _mod_body,
        grid=(n // tn,),
        in_specs=[
            pl.BlockSpec((rows, D_MODEL), lambda j: (0, 0)),
            pl.BlockSpec((D_MODEL, tn), lambda j: (0, j)),
            pl.BlockSpec((1, tn), lambda j: (0, j)),
        ],
        out_specs=pl.BlockSpec((rows, tn), lambda j: (0, j)),
        out_shape=jax.ShapeDtypeStruct((rows, n), F32),
        compiler_params=_params(("arbitrary",)),
        name="mod",
    )(cc, w_mod, b_mod)


def _rms_rows(x, n):
    return x * lax.rsqrt(jnp.sum(x * x, axis=-1, keepdims=True) * (1.0 / n) + EPS)


def _rope(x, cos, sin):
    return x * cos + pltpu.roll(x, LANES // 2, 1) * sin


def _proj_body(*refs, latent):
    if latent:
        (x_ref, g_ref, sc_ref, sh_ref, win_ref, qag_ref, kvag_ref, wq_ref, wk_ref, wv_ref,
         qg_ref, kg_ref, qadd_ref, kadd_ref, cos_ref, sin_ref,
         q_out, k_out, v_out, f_out, ga_out, gb_out) = refs
    else:
        (x_ref, g_ref, sc_ref, sh_ref, win_ref, kvag_ref, wk_ref, wv_ref, kg_ref, kadd_ref,
         k_out, v_out) = refs
    x = x_ref[0]
    h = _rms_rows(x, D_MODEL) * (g_ref[...] * (1.0 + sc_ref[0])) + sh_ref[0]
    hb = h.astype(BF16)

    if latent:
        cos = cos_ref[...]
        sin = sin_ref[...]

    ckv = _rms_rows(_dot(hb, win_ref[:, PC_KV:PC_PE]), KV_LORA) * kvag_ref[...]
    ckvb = ckv.astype(BF16)
    pe = _dot(hb, win_ref[:, PC_PE:PC_F])
    kall = _dot(ckvb, wk_ref[...])
    kg = kg_ref[...]
    kadd = kadd_ref[...]
    for hd in range(MLA_HEADS):
        kh = kall[:, hd * HEAD_SLOT:(hd + 1) * HEAD_SLOT] + pe
        kh = _rms_rows(kh, QK_DIM) * kg
        if latent:
            kh = _rope(kh, cos, sin)
        k_out[0, hd] = (kh + kadd).astype(BF16)
    v = _dot(ckvb, wv_ref[...])
    for hp in range(MLA_HEADS // 2):
        v_out[0, hp] = v[:, hp * LANES:(hp + 1) * LANES].T.astype(BF16)

    if latent:
        cq = _rms_rows(_dot(hb, win_ref[:, PC_Q:PC_KV]), Q_LORA) * qag_ref[...]
        qall = _dot(cq.astype(BF16), wq_ref[...])
        qg = qg_ref[...]
        qadd = qadd_ref[...]
        for hd in range(MLA_HEADS):
            qh = qall[:, hd * HEAD_SLOT:(hd + 1) * HEAD_SLOT]
            q_out[0, hd] = (_rope(_rms_rows(qh, QK_DIM) * qg, cos, sin) + qadd).astype(BF16)
        f_out[0] = _dot(hb, win_ref[:, PC_F:PC_GA]).astype(BF16)
        ga_out[0] = _sigmoid(_dot(hb, win_ref[:, PC_GA:PC_GB])).astype(BF16)
        gb_out[0] = _sigmoid(_dot(hb, win_ref[:, PC_GB:PC_END])).astype(BF16)


def _full(shape):
    nd = len(shape)
    return pl.BlockSpec(shape, lambda *_: (0,) * nd)


def _proj_lat_call(x, g1n, sc1, sh1, w_in_p, qag, kvag, wq_p, wk_p, wv_p, qg_p, kg_p, qadd, kadd, cos_t, sin_t, tm):
    B, T, _ = x.shape
    H = MLA_HEADS
    tok = lambda w: pl.BlockSpec((1, tm, w), lambda b, j: (b, j, 0))
    per_b = pl.BlockSpec((1, 1, D_MODEL), lambda b, j: (b, 0, 0))
    tab = pl.BlockSpec((tm, HEAD_SLOT), lambda b, j: (j, 0))
    return pl.pallas_call(
        functools.partial(_proj_body, latent=True),
        grid=(B, T // tm),
        in_specs=[tok(D_MODEL), _full(g1n.shape), per_b, per_b, _full(w_in_p.shape), _full(qag.shape),
                  _full(kvag.shape), _full(wq_p.shape), _full(wk_p.shape), _full(wv_p.shape),
                  _full(qg_p.shape), _full(kg_p.shape), _full(qadd.shape), _full(kadd.shape), tab, tab],
        out_specs=[
            pl.BlockSpec((1, H, tm, HEAD_SLOT), lambda b, j: (b, 0, j, 0)),
            pl.BlockSpec((1, H, tm, HEAD_SLOT), lambda b, j: (b, 0, j, 0)),
            pl.BlockSpec((1, H // 2, LANES, tm), lambda b, j: (b, 0, 0, j)),
            tok(D_F), tok(D_MODEL), tok(D_MODEL),
        ],
        out_shape=[
            jax.ShapeDtypeStruct((B, H, T, HEAD_SLOT), BF16),
            jax.ShapeDtypeStruct((B, H, T, HEAD_SLOT), BF16),
            jax.ShapeDtypeStruct((B, H // 2, LANES, T), BF16),
            jax.ShapeDtypeStruct((B, T, D_F), BF16),
            jax.ShapeDtypeStruct((B, T, D_MODEL), BF16),
            jax.ShapeDtypeStruct((B, T, D_MODEL), BF16),
        ],
        compiler_params=_params(("parallel", "parallel")),
        name="proj_lat",
    )(x, g1n, sc1, sh1, w_in_p, qag, kvag, wq_p, wk_p, wv_p, qg_p, kg_p, qadd, kadd, cos_t, sin_t)


def _proj_ctx_call(ctx, g1n, csc1, csh1, w_in_p, kvag, wk_p, wv_p, kg_p, kadd):
    B, TC, _ = ctx.shape
    H = MLA_HEADS
    shared = pl.BlockSpec((1, 1, D_MODEL), lambda b: (0, 0, 0))
    return pl.pallas_call(
        functools.partial(_proj_body, latent=False),
        grid=(B,),
        in_specs=[pl.BlockSpec((1, TC, D_MODEL), lambda b: (b, 0, 0)), _full(g1n.shape), shared, shared,
                  _full(w_in_p.shape), _full(kvag.shape), _full(wk_p.shape), _full(wv_p.shape),
                  _full(kg_p.shape), _full(kadd.shape)],
        out_specs=[
            pl.BlockSpec((1, H, TC, HEAD_SLOT), lambda b: (b, 0, 0, 0)),
            pl.BlockSpec((1, H // 2, LANES, TC), lambda b: (b, 0, 0, 0)),
        ],
        out_shape=[
            jax.ShapeDtypeStruct((B, H, TC, HEAD_SLOT), BF16),
            jax.ShapeDtypeStruct((B, H // 2, LANES, TC), BF16),
        ],
        compiler_params=_params(("parallel",)),
        name="proj_ctx",
    )(ctx, g1n, csc1, csh1, w_in_p, kvag, wk_p, wv_p, kg_p, kadd)


def _attn_body(q_ref, kl_ref, kc_ref, vl_ref, vc_ref, o_ref):
    tq = q_ref.shape[2]
    T = kl_ref.shape[2]
    nck = T // KEY_CHUNK
    row = lax.broadcasted_iota(jnp.int32, (LANES, tq), 0)

    def probs(hd):
        q = q_ref[0, hd]
        den = jnp.zeros((1, tq), F32)
        ps = []
        for c in range(nck + 1):
            k = kc_ref[0, hd] if c == nck else kl_ref[0, hd, c * KEY_CHUNK:(c + 1) * KEY_CHUNK, :]
            e = jnp.exp2(_dot_nt(k, q))
            den = den + jnp.sum(e, axis=0, keepdims=True)
            ps.append(e.astype(BF16))
        return ps, den

    def weighted_values(hd, ps):
        hp = hd // 2
        o = _dot(vc_ref[0, hp], ps[nck])
        for c in range(nck):
            o = o + _dot(vl_ref[0, hp, :, c * KEY_CHUNK:(c + 1) * KEY_CHUNK], ps[c])
        return o

    outs, dens, prev = [], [], None
    for hd in range(MLA_HEADS + 1):
        cur = probs(hd) if hd < MLA_HEADS else None
        if prev is not None:
            outs.append(weighted_values(hd - 1, prev[0]) / prev[1])
            dens.append(prev[1])
        prev = cur
    for hp in range(MLA_HEADS // 2):
        o_ref[0, hp] = jnp.where(row < V_DIM, outs[2 * hp], outs[2 * hp + 1]).astype(BF16)
    min_den = functools.reduce(jnp.minimum, dens)

    @pl.when(jnp.min(min_den) < DEN_FLOOR)
    def _():
        def pair(hp, carry):
            res = []
            for hd in (2 * hp, 2 * hp + 1):
                q = q_ref[0, hd]
                sl = _dot_nt(kl_ref[0, hd], q)
                sc = _dot_nt(kc_ref[0, hd], q)
                m = jnp.maximum(jnp.max(sl, axis=0, keepdims=True), jnp.max(sc, axis=0, keepdims=True))
                el = jnp.exp2(sl - m)
                ec = jnp.exp2(sc - m)
                den = jnp.sum(el, axis=0, keepdims=True) + jnp.sum(ec, axis=0, keepdims=True)
                o = _dot(vl_ref[0, hp], el.astype(BF16)) + _dot(vc_ref[0, hp], ec.astype(BF16))
                res.append(o / den)
            o_ref[0, hp] = jnp.where(row < V_DIM, res[0], res[1]).astype(BF16)
            return carry

        lax.fori_loop(0, MLA_HEADS // 2, pair, 0)


def _attn_call(q, k_lat, k_ctx, vt_lat, vt_ctx, tq):
    B, H, T, _ = q.shape
    TC = k_ctx.shape[2]
    return pl.pallas_call(
        _attn_body,
        grid=(B, T // tq),
        in_specs=[
            pl.BlockSpec((1, H, tq, HEAD_SLOT), lambda b, j: (b, 0, j, 0)),
            pl.BlockSpec((1, H, T, HEAD_SLOT), lambda b, j: (b, 0, 0, 0)),
            pl.BlockSpec((1, H, TC, HEAD_SLOT), lambda b, j: (b, 0, 0, 0)),
            pl.BlockSpec((1, H // 2, LANES, T), lambda b, j: (b, 0, 0, 0)),
            pl.BlockSpec((1, H // 2, LANES, TC), lambda b, j: (b, 0, 0, 0)),
        ],
        out_specs=pl.BlockSpec((1, H // 2, LANES, tq), lambda b, j: (b, 0, 0, j)),
        out_shape=jax.ShapeDtypeStruct((B, H // 2, LANES, T), BF16),
        compiler_params=_params(("parallel", "parallel")),
        name="attn",
    )(q, k_lat, k_ctx, vt_lat, vt_ctx)


def _cadd(a, b):
    return a[0] + b[0], a[1] + b[1]


def _csub(a, b):
    return a[0] - b[0], a[1] - b[1]


def _cmul_neg_i(a):
    return a[1], -a[0]


def _cmul_pos_i(a):
    return -a[1], a[0]


def _fft8(u):
    r = 0.7071067811865476
    a0, a1 = _cadd(u[0], u[4]), _csub(u[0], u[4])
    a2, a3 = _cadd(u[2], u[6]), _csub(u[2], u[6])
    a4, a5 = _cadd(u[1], u[5]), _csub(u[1], u[5])
    a6, a7 = _cadd(u[3], u[7]), _csub(u[3], u[7])
    e0, e2 = _cadd(a0, a2), _csub(a0, a2)
    e1, e3 = _cadd(a1, _cmul_neg_i(a3)), _cadd(a1, _cmul_pos_i(a3))
    o0, o2 = _cadd(a4, a6), _csub(a4, a6)
    o1, o3 = _cadd(a5, _cmul_neg_i(a7)), _cadd(a5, _cmul_pos_i(a7))
    wo1 = (r * (o1[0] + o1[1]), r * (o1[1] - o1[0]))
    wo3 = (r * (o3[1] - o3[0]), -r * (o3[0] + o3[1]))
    return [_cadd(e0, o0), _cadd(e1, wo1), _cadd(e2, _cmul_neg_i(o2)), _cadd(e3, wo3),
            _csub(e0, o0), _csub(e1, wo1), _cadd(e2, _cmul_pos_i(o2)), _csub(e3, wo3)]


def _fourier_body(f_ref, cc_ref, twc_ref, tws_ref, m2_ref, o_ref, b_ref, y_ref):
    for g in range(F_GROUPS):
        cols = slice(g * F_GROUP_DIM, (g + 1) * F_GROUP_DIM)
        y = _dot(f_ref[0, :, cols], cc_ref[...])
        u = [(y[j * FFT_N:(j + 1) * FFT_N, :F_GROUP_DIM], y[j * FFT_N:(j + 1) * FFT_N, F_GROUP_DIM:])
             for j in range(FFT_R)]
        a = _fft8(u)
        for k1 in range(FFT_R):
            ar, ai = a[k1]
            if k1 > 0:
                c = twc_ref[k1 * FFT_N:(k1 + 1) * FFT_N, :]
                s = tws_ref[k1 * FFT_N:(k1 + 1) * FFT_N, :]
                ar, ai = ar * c + ai * s, ai * c - ar * s
            b_ref[k1, 0:FFT_N, cols] = ar.astype(BF16)
            b_ref[k1, FFT_N:2 * FFT_N, cols] = ai.astype(BF16)
    for k1 in range(FFT_R):
        y = _dot(m2_ref[...], b_ref[k1])
        for g in range(F_GROUPS):
            y_ref[g, pl.ds(k1, FFT_N, stride=FFT_R), :] = y[:, g * F_GROUP_DIM:(g + 1) * F_GROUP_DIM]
    for g in range(F_GROUPS):
        o_ref[0, :, g * F_GROUP_DIM:(g + 1) * F_GROUP_DIM] = y_ref[g].astype(BF16)


def _fourier_call(f, cc, twc, tws, m2):
    B, T, _ = f.shape
    return pl.pallas_call(
        _fourier_body,
        grid=(B,),
        in_specs=[pl.BlockSpec((1, T, D_F), lambda b: (b, 0, 0)), _full(cc.shape), _full(twc.shape),
                  _full(tws.shape), _full(m2.shape)],
        out_specs=pl.BlockSpec((1, T, D_F), lambda b: (b, 0, 0)),
        out_shape=jax.ShapeDtypeStruct((B, T, D_F), BF16),
        scratch_shapes=[pltpu.VMEM((FFT_R, 2 * FFT_N, D_F), BF16), pltpu.VMEM((F_GROUPS, T, F_GROUP_DIM), F32)],
        compiler_params=_params(("parallel",)),
        name="fourier",
    )(f, cc, twc, tws, m2)


def _merge_body(at_ref, fo_ref, ga_ref, gb_ref, x_ref, g1_ref, g2n_ref, sc2_ref, sh2_ref,
                wo_ref, wf_ref, wout_ref, wrh_ref, wrl_ref, x1_out, h2_out, aff_out):
    tm = x_ref.shape[1]
    attn_t = jnp.concatenate([at_ref[0, hp] for hp in range(MLA_HEADS // 2)], axis=0)
    a = lax.dot_general(attn_t, wo_ref[...], (((0,), (0,)), ((), ())), preferred_element_type=F32)
    fo = _dot(fo_ref[0], wf_ref[...])
    mix = ga_ref[0].astype(F32) * a + gb_ref[0].astype(F32) * fo
    y = _dot(mix.astype(BF16), wout_ref[...])
    x1 = x_ref[0] + g1_ref[0] * y
    x1_out[0] = x1
    h2 = _rms_rows(x1, D_MODEL) * g2n_ref[...]
    h2 = h2 * (1.0 + sc2_ref[0]) + sh2_ref[0]
    for c in range(D_MODEL // LANES):
        h2_out[0, pl.ds(c, tm, stride=D_MODEL // LANES), :] = h2[:, c * LANES:(c + 1) * LANES]
    hh, hl = _split2(h2)
    p_hi = _dot(hh, wrl_ref[...])
    logits = p_hi + pltpu.roll(p_hi, LANES - N_EXPERTS, 1) + _dot(hl, wrh_ref[...])
    lane = lax.broadcasted_iota(jnp.int32, logits.shape, 1)
    logits = jnp.where(lane < N_EXPERTS, logits, -1e30)
    ex = jnp.exp(logits - jnp.max(logits, axis=-1, keepdims=True))
    aff_out[0] = ex / jnp.sum(ex, axis=-1, keepdims=True)


def _merge_call(attn, four, ga, gb, x, g1, g2n, sc2, sh2, wo, wf, wout, wrh, wrl, tm):
    B, T, _ = x.shape
    H = MLA_HEADS
    tok = lambda w: pl.BlockSpec((1, tm, w), lambda b, j: (b, j, 0))
    per_b = pl.BlockSpec((1, 1, D_MODEL), lambda b, j: (b, 0, 0))
    nchunk = D_MODEL // LANES
    return pl.pallas_call(
        _merge_body,
        grid=(B, T // tm),
        in_specs=[
            pl.BlockSpec((1, H // 2, LANES, tm), lambda b, j: (b, 0, 0, j)),
            tok(D_F), tok(D_MODEL), tok(D_MODEL), tok(D_MODEL), per_b, _full(g2n.shape), per_b, per_b,
            _full(wo.shape), _full(wf.shape), _full(wout.shape), _full(wrh.shape), _full(wrl.shape),
        ],
        out_specs=[
            tok(D_MODEL),
            pl.BlockSpec((1, tm * nchunk, LANES), lambda b, j: (b, j, 0)),
            tok(LANES),
        ],
        out_shape=[
            jax.ShapeDtypeStruct((B, T, D_MODEL), F32),
            jax.ShapeDtypeStruct((B, T * nchunk, LANES), F32),
            jax.ShapeDtypeStruct((B, T, LANES), F32),
        ],
        compiler_params=_params(("parallel", "parallel")),
        name="merge",
    )(attn, four, ga, gb, x, g1, g2n, sc2, sh2, wo, wf, wout, wrh, wrl)


def _prefix_excl(m, tri):
    outs = []
    run = jnp.zeros((m.shape[0], 1), F32)
    for blk in range(m.shape[1] // LANES):
        mb = m[:, blk * LANES:(blk + 1) * LANES]
        inc = _dot(mb.astype(BF16), tri)
        outs.append(inc - mb + run)
        run = run + inc[:, LANES - 1:LANES]
    return jnp.concatenate(outs, axis=1)


def _select_body(aff_ref, tri_ref, pos_ref, idx_out, gate_out, *, cap):
    T = aff_ref.shape[1]
    aff = aff_ref[0].T[0:N_EXPERTS]

    def search(carry):
        base, step, rounds, _ = carry
        thr = base
        for j in range(1, 2 ** SEARCH_BITS):
            cand = base + float(j) * step
            cnt = jnp.sum((aff >= cand).astype(F32), axis=-1, keepdims=True)
            thr = jnp.where(cnt >= float(cap), cand, thr)
        settled = jnp.where((thr > 0.0) & (step < thr * SEARCH_DONE), 1.0, 0.0)
        return thr, step * (0.5 ** SEARCH_BITS), rounds + 1, (jnp.min(settled) < 1.0).astype(jnp.int32)

    init = (jnp.zeros((N_EXPERTS, 1), F32), jnp.full((N_EXPERTS, 1), 2.0 * 0.5 ** SEARCH_BITS, F32),
            jnp.int32(0), jnp.int32(1))
    thr = lax.while_loop(lambda c: (c[3] > 0) & (c[2] < SEARCH_MAX_ROUNDS), search, init)[0]
    gt = (aff > thr).astype(F32)
    eq = (aff == thr).astype(F32)
    need = cap - jnp.sum(gt, axis=-1, keepdims=True)
    tri = tri_ref[...]
    sel = gt + eq * (_prefix_excl(eq, tri) < need).astype(F32)
    slot = jnp.where(sel > 0.0, _prefix_excl(sel, tri), -1.0)

    pos = pos_ref[...]
    srow = lax.broadcasted_iota(jnp.int32, (cap, T), 0).astype(F32)
    zero = jnp.zeros((3, T), BF16)
    for e in range(N_EXPERTS):
        onehot = jnp.where(srow == slot[e:e + 1, :], 1.0, 0.0).astype(BF16)
        ah, am, al = _split3(aff[e:e + 1, :])
        vals = jnp.concatenate([pos, ah, am, al, zero], axis=0)
        res = _dot_nt(vals, onehot)
        idx_out[0, e:e + 1, :] = (res[0:1] * float(LANES) + res[1:2]).astype(jnp.int32)
        gate_out[0, e:e + 1, :] = res[2:3] + (res[3:4] + res[4:5])


def _select_call(aff, tri, pos, cap):
    B, T, _ = aff.shape
    return pl.pallas_call(
        functools.partial(_select_body, cap=cap),
        grid=(B,),
        in_specs=[pl.BlockSpec((1, T, LANES), lambda b: (b, 0, 0)), _full(tri.shape), _full(pos.shape)],
        out_specs=[pl.BlockSpec((1, N_EXPERTS, cap), lambda b: (b, 0, 0)),
                   pl.BlockSpec((1, N_EXPERTS, cap), lambda b: (b, 0, 0))],
        out_shape=[jax.ShapeDtypeStruct((B, N_EXPERTS, cap), jnp.int32),
                   jax.ShapeDtypeStruct((B, N_EXPERTS, cap), F32)],
        compiler_params=_params(("parallel",)),
        name="select",
    )(aff, tri, pos)


def _moe_body(ip_ref, i0_ref, i1_ref, in_ref, g0_ref, g1_ref, h2_ref, wg_ref, wu_ref, wd_ref, acc_ref,
              xg_a, xg_b, ye_a, ye_b, *, cap, stride):
    k = pl.program_id(1)
    nchunk = D_MODEL // LANES
    group = 8

    def gather(i_ref, xg_ref):
        def rows(s0):
            for s in range(s0, s0 + group):
                base = pl.multiple_of(i_ref[0, 0, 0, s], SUBLANES)
                tile = h2_ref[0, pl.ds(base, nchunk), :]
                xg_ref[pl.ds(s, nchunk, stride=stride), :] = tile
            return tile
        return [functools.partial(rows, s0) for s0 in range(0, cap, group)]

    def scatter(i_ref, ye_ref):
        def rows(s0):
            bases, news = [], []
            for s in range(s0, s0 + group):
                base = pl.multiple_of(i_ref[0, 0, 0, s], SUBLANES)
                bases.append(base)
                news.append(acc_ref[0, pl.ds(base, nchunk), :] + ye_ref[pl.ds(s, nchunk, stride=stride), :])
            for base, new in zip(bases, news):
                acc_ref[0, pl.ds(base, nchunk), :] = new
            return news[-1]
        return [functools.partial(rows, s0) for s0 in range(0, cap, group)]

    def expert(j, g_ref, xg_ref, ye_ref):
        state = {}
        n_mid = D_EXPERT // MXU_TILE
        n_out = D_MODEL // MXU_TILE

        def load(zero):
            xb = jnp.concatenate([xg_ref[c * stride:c * stride + cap, :] for c in range(nchunk)], axis=1)
            state["x"] = xb.astype(BF16)
            state["gate"] = jnp.broadcast_to(g_ref[0, 0], (LANES, cap)).T[:, 0:1]
            state["h"] = []

        def mid(n, zero):
            cols = slice(n * MXU_TILE, (n + 1) * MXU_TILE)
            a = _dot(state["x"], wg_ref[j, :, cols]) + zero
            u = _dot(state["x"], wu_ref[j, :, cols])
            state["h"].append((a * _sigmoid(a) * u).astype(BF16))

        def out(n, zero):
            cols = slice(n * MXU_TILE, (n + 1) * MXU_TILE)
            ye = _dot(state["h"][0], wd_ref[j, 0:MXU_TILE, cols])
            for m in range(1, n_mid):
                ye = ye + _dot(state["h"][m], wd_ref[j, m * MXU_TILE:(m + 1) * MXU_TILE, cols])
            ye = ye * state["gate"] + zero
            for c in range(MXU_TILE // LANES):
                cc = n * (MXU_TILE // LANES) + c
                ye_ref[cc * stride:cc * stride + cap, :] = ye[:, c * LANES:(c + 1) * LANES]

        return ([load] + [functools.partial(mid, n) for n in range(n_mid)]
                + [functools.partial(out, n) for n in range(n_out)])

    def paced(main, side):
        zero = jnp.zeros((1, MXU_TILE), F32)
        done = 0
        for i, piece in enumerate(main):
            piece(zero)
            upto = (len(side) * (i + 1)) // (len(main) - 1) if i + 1 < len(main) else len(side)
            tiles = [s() for s in side[done:upto]]
            done = upto
            if tiles:
                bits = pltpu.bitcast(functools.reduce(jnp.add, tiles), jnp.uint32)
                z = pltpu.bitcast(lax.shift_right_logical(bits, jnp.uint32(32)), F32)[0:1, :]
                zero = jnp.concatenate([z] * (MXU_TILE // LANES), axis=1)

    def zipped(a, b):
        return [f for pair in zip(a, b) for f in pair]

    @pl.when(k == 0)
    def _():
        acc_ref[...] = jnp.zeros_like(acc_ref)
        ye_b[...] = jnp.zeros_like(ye_b)
        for piece in gather(i0_ref, xg_a):
            piece()

    paced(expert(0, g0_ref, xg_a, ye_a), zipped(scatter(ip_ref, ye_b), gather(i1_ref, xg_b)))
    paced(expert(1, g1_ref, xg_b, ye_b), zipped(scatter(i0_ref, ye_a), gather(in_ref, xg_a)))

    @pl.when(k == pl.num_programs(1) - 1)
    def _():
        for piece in scatter(i1_ref, ye_b):
            piece()


def _moe_call(idx, gate, h2t, wg, wu, wd, T):
    B, E, cap = idx.shape
    nchunk = D_MODEL // LANES
    stride = cap + SUBLANES
    buf = pltpu.VMEM((nchunk * stride, LANES), F32)
    idx4 = idx.reshape(B, E, 1, cap)
    gate4 = gate.reshape(B, E, 1, cap)

    def rows(expert_of_step, memory_space=None):
        spec = pl.BlockSpec((1, 1, 1, cap), lambda b, k: (b, expert_of_step(k), 0, 0))
        return spec if memory_space is None else pl.BlockSpec(
            (1, 1, 1, cap), lambda b, k: (b, expert_of_step(k), 0, 0), memory_space=memory_space)

    return pl.pallas_call(
        functools.partial(_moe_body, cap=cap, stride=stride),
        grid=(B, E // 2),
        in_specs=[
            rows(lambda k: jnp.maximum(2 * k - 1, 0), pltpu.SMEM),
            rows(lambda k: 2 * k, pltpu.SMEM),
            rows(lambda k: 2 * k + 1, pltpu.SMEM),
            rows(lambda k: jnp.minimum(2 * k + 2, E - 1), pltpu.SMEM),
            rows(lambda k: 2 * k),
            rows(lambda k: 2 * k + 1),
            pl.BlockSpec((1, T * nchunk, LANES), lambda b, k: (b, 0, 0)),
            pl.BlockSpec((2, D_MODEL, D_EXPERT), lambda b, k: (k, 0, 0)),
            pl.BlockSpec((2, D_MODEL, D_EXPERT), lambda b, k: (k, 0, 0)),
            pl.BlockSpec((2, D_EXPERT, D_MODEL), lambda b, k: (k, 0, 0)),
        ],
        out_specs=pl.BlockSpec((1, T * nchunk, LANES), lambda b, k: (b, 0, 0)),
        out_shape=jax.ShapeDtypeStruct((B, T * nchunk, LANES), F32),
        scratch_shapes=[buf, buf, buf, buf],
        compiler_params=_params(("parallel", "arbitrary")),
        name="moe",
    )(idx4, idx4, idx4, idx4, gate4, gate4, h2t, wg, wu, wd)


def _final_body(x1_ref, y_ref, g2_ref, o_ref):
    tm = x1_ref.shape[1]
    for c in range(D_MODEL // LANES):
        cols = slice(c * LANES, (c + 1) * LANES)
        y = y_ref[0, pl.ds(c, tm, stride=D_MODEL // LANES), :]
        o_ref[0, :, cols] = x1_ref[0, :, cols] + g2_ref[0, :, cols] * y


def _final_call(x1, y, g2, tm):
    B, T, _ = x1.shape
    nchunk = D_MODEL // LANES
    tok = pl.BlockSpec((1, tm, D_MODEL), lambda b, j: (b, j, 0))
    return pl.pallas_call(
        _final_body,
        grid=(B, T // tm),
        in_specs=[tok,
                  pl.BlockSpec((1, tm * nchunk, LANES), lambda b, j: (b, j, 0)),
                  pl.BlockSpec((1, 1, D_MODEL), lambda b, j: (b, 0, 0))],
        out_specs=tok,
        out_shape=jax.ShapeDtypeStruct((B, T, D_MODEL), F32),
        compiler_params=_params(("parallel", "parallel")),
        name="final",
    )(x1, y, g2)


def _slot_layout():
    half = QK_ROPE // 2
    n_freq = half // 2
    src = np.full((HEAD_SLOT,), -1, np.int64)
    for axis in range(2):
        first = QK_NOPE + axis * half
        src[axis * n_freq:(axis + 1) * n_freq] = np.arange(first, first + n_freq)
        src[LANES // 2 + axis * n_freq:LANES // 2 + (axis + 1) * n_freq] = np.arange(first + n_freq, first + half)
    src[half:half + QK_NOPE // 2] = np.arange(0, QK_NOPE // 2)
    src[LANES // 2 + half:LANES // 2 + half + QK_NOPE // 2] = np.arange(QK_NOPE // 2, QK_NOPE)
    assert src[SHIFT_LANE] == -1 and sorted(src[src >= 0]) == list(range(QK_DIM))
    return src


_SLOT_SRC = _slot_layout()


def _rope_tables(T):
    half = QK_ROPE // 2
    n_freq = half // 2
    inv_freq = 1.0 / (ROPE_THETA ** (np.arange(n_freq, dtype=np.float32) / n_freq))
    t = np.arange(T)
    cos = np.ones((T, HEAD_SLOT), np.float32)
    sin = np.zeros((T, HEAD_SLOT), np.float32)
    for axis, pos in enumerate((t // GRID_W, t % GRID_W)):
        ang = pos.astype(np.float32)[:, None] * inv_freq[None, :].astype(np.float32)
        c, s = np.cos(ang).astype(np.float32), np.sin(ang).astype(np.float32)
        lo1 = axis * n_freq
        lo2 = LANES // 2 + axis * n_freq
        cos[:, lo1:lo1 + n_freq] = c
        cos[:, lo2:lo2 + n_freq] = c
        sin[:, lo1:lo1 + n_freq] = -s
        sin[:, lo2:lo2 + n_freq] = s
    return jnp.asarray(cos), jnp.asarray(sin)


def _dft_tables(T):
    assert T == FFT_R * FFT_N
    n = np.arange(F_GROUP_DIM)
    ang_c = 2.0 * np.pi * ((n[:, None] * n[None, :]) % F_GROUP_DIM) / F_GROUP_DIM
    cc = np.concatenate([np.cos(ang_c), -np.sin(ang_c)], axis=1) / np.sqrt(F_GROUP_DIM)
    k1 = np.repeat(np.arange(FFT_R), FFT_N)
    n2 = np.tile(np.arange(FFT_N), FFT_R)
    ang_w = 2.0 * np.pi * (k1 * n2) / T
    twc = np.broadcast_to(np.cos(ang_w)[:, None], (T, F_GROUP_DIM))
    tws = np.broadcast_to(np.sin(ang_w)[:, None], (T, F_GROUP_DIM))
    m = np.arange(FFT_N)
    ang_2 = 2.0 * np.pi * ((m[:, None] * m[None, :]) % FFT_N) / FFT_N
    m2 = np.concatenate([np.cos(ang_2), np.sin(ang_2)], axis=1) / np.sqrt(T)
    return (jnp.asarray(cc, F32).astype(BF16), jnp.asarray(twc, F32), jnp.asarray(tws, F32),
            jnp.asarray(m2, F32).astype(BF16))


def kernel(x, c, ctx, c_ctx, w_mod, b_mod, norm1_g, w_in, q_a_norm_g, kv_a_norm_g, w_q_up, w_kv_up,
           q_norm_g, k_norm_g, w_o_attn, w_fourier, w_out, norm2_g, w_router, w_e_gate, w_e_up, w_e_down):
    B, T, D = x.shape
    assert w_mod.shape[0] == 1 and D == D_MODEL and T % 256 == 0
    H = MLA_HEADS
    cap = (CAPACITY_FACTOR * T) // N_EXPERTS

    rows = -(-(B + 1) // SUBLANES) * SUBLANES
    cc_in = jnp.concatenate([c, c_ctx[None, :], jnp.zeros((rows - B - 1, D), F32)], axis=0)
    mod = _mod_call(cc_in, w_mod[0], b_mod)
    sh1, sc1, g1, sh2, sc2, g2 = [mod[:B, i * D:(i + 1) * D].reshape(B, 1, D) for i in range(6)]
    csh1 = mod[B:B + 1, 0:D].reshape(1, 1, D)
    csc1 = mod[B:B + 1, D:2 * D].reshape(1, 1, D)

    src = _SLOT_SRC
    used = jnp.asarray(src >= 0, F32)
    nope = jnp.asarray((src >= 0) & (src < QK_NOPE), F32)
    rope = jnp.asarray(src >= QK_NOPE, F32)
    lane_src = np.maximum(src, 0)
    wi = w_in[0]
    pe_cols = jnp.take(wi[:, OFF_KPE:OFF_F], np.maximum(src - QK_NOPE, 0), axis=1) * rope
    w_in_p = jnp.concatenate([wi[:, OFF_Q:OFF_KPE], pe_cols, wi[:, OFF_F:N_IN]], axis=1).astype(BF16)
    wq_p = jnp.take(w_q_up[0].reshape(Q_LORA, H, QK_DIM), lane_src, axis=2) * used
    wq_p = wq_p.reshape(Q_LORA, H * HEAD_SLOT).astype(BF16)
    wkv = w_kv_up[0].reshape(KV_LORA, H, QK_NOPE + V_DIM)
    wk_p = jnp.take(wkv[:, :, :QK_NOPE], np.minimum(lane_src, QK_NOPE - 1), axis=2) * nope
    wk_p = wk_p.reshape(KV_LORA, H * HEAD_SLOT).astype(BF16)
    wv_p = wkv[:, :, QK_NOPE:].reshape(KV_LORA, H * V_DIM).astype(BF16)
    qg_p = (jnp.take(q_norm_g[0], lane_src) * used * (QK_DIM ** -0.5 * LOG2E)).reshape(1, HEAD_SLOT)
    kg_p = (jnp.take(k_norm_g[0], lane_src) * used).reshape(1, HEAD_SLOT)
    bound = (QK_DIM * BOUND_SLACK) * jnp.max(jnp.abs(qg_p)) * jnp.max(jnp.abs(kg_p))
    on_shift = jnp.asarray(np.arange(HEAD_SLOT) == SHIFT_LANE, F32).reshape(1, HEAD_SLOT)
    qadd = on_shift * (-bound)
    kadd = on_shift
    qag = q_a_norm_g[0].reshape(1, Q_LORA)
    kvag = kv_a_norm_g[0].reshape(1, KV_LORA)
    g1n = norm1_g[0].reshape(1, D)
    g2n = norm2_g[0].reshape(1, D)
    cos_t, sin_t = _rope_tables(T)

    q, k_lat, vt_lat, f, ga, gb = _proj_lat_call(x, g1n, sc1, sh1, w_in_p, qag, kvag, wq_p, wk_p, wv_p,
                                                 qg_p, kg_p, qadd, kadd, cos_t, sin_t, tm=256)
    k_ctx, vt_ctx = _proj_ctx_call(ctx, g1n, csc1, csh1, w_in_p, kvag, wk_p, wv_p, kg_p, kadd)
    attn = _attn_call(q, k_lat, k_ctx, vt_lat, vt_ctx, tq=512)
    four = _fourier_call(f, *_dft_tables(T))

    wr = jnp.pad(w_router[0], ((0, 0), (0, LANES - N_EXPERTS)))
    wrh = wr.astype(BF16)
    wrl = wrh + jnp.roll((wr - wrh.astype(F32)).astype(BF16), N_EXPERTS, axis=1)
    x1, h2t, aff = _merge_call(attn, four, ga, gb, x, g1, g2n, sc2, sh2, w_o_attn[0].astype(BF16),
                               w_fourier[0].astype(BF16), w_out[0].astype(BF16), wrh, wrl, tm=256)

    tri = jnp.asarray(np.triu(np.ones((LANES, LANES), np.float32)), BF16)
    addr = np.arange(T) * (D // LANES)
    pos = jnp.asarray(np.stack([addr // LANES, addr % LANES]).astype(np.float32), BF16)
    idx, gate = _select_call(aff, tri, pos, cap)
    y = _moe_call(idx, gate, h2t, w_e_gate[0].astype(BF16), w_e_up[0].astype(BF16),
                  w_e_down[0].astype(BF16), T)
    return _final_call(x1, y, g2, tm=512)
```

```python
import functools

import numpy as np
import jax
import jax.numpy as jnp
from jax import lax
from jax.experimental import pallas as pl
from jax.experimental.pallas import tpu as pltpu

F32 = jnp.float32
BF16 = jnp.bfloat16

D_MODEL = 1024
GRID_W = 64
MLA_HEADS = 8
QK_NOPE = 64
QK_ROPE = 32
QK_DIM = QK_NOPE + QK_ROPE
V_DIM = 64
Q_LORA = 384
KV_LORA = 256
ROPE_THETA = 10000.0
F_GROUPS = 4
F_GROUP_DIM = 128
D_F = F_GROUPS * F_GROUP_DIM
OFF_Q = 0
OFF_KV = OFF_Q + Q_LORA
OFF_KPE = OFF_KV + KV_LORA
OFF_F = OFF_KPE + QK_ROPE
OFF_GA = OFF_F + D_F
OFF_GB = OFF_GA + D_MODEL
N_IN = OFF_GB + D_MODEL
N_EXPERTS = 16
D_EXPERT = 512
CAPACITY_FACTOR = 2
EPS = 1e-6

LANES = 128
SUBLANES = 8
MXU_TILE = 256
HEAD_SLOT = LANES
SHIFT_LANE = 48
BOUND_SLACK = 1.02
DEN_FLOOR = 2.0 ** -60
LOG2E = 1.4426950408889634
FFT_R = 8
FFT_N = 256
SEARCH_BITS = 3
SEARCH_DONE = 2.0 ** -30
SEARCH_MAX_ROUNDS = 56
KEY_CHUNK = 256
VMEM_LIMIT = 56 * 1024 * 1024

PC_Q = 0
PC_PE = PC_Q + Q_LORA
PC_KV = PC_PE + HEAD_SLOT
PC_F = PC_KV + KV_LORA
PC_GA = PC_F + D_F
PC_GB = PC_GA + D_MODEL
PC_END = PC_GB + D_MODEL


def _dot(a, b):
    return jnp.dot(a, b, preferred_element_type=F32)


def _dot_nt(a, b):
    return lax.dot_general(a, b, (((1,), (1,)), ((), ())), preferred_element_type=F32)


def _split2(a):
    hi = a.astype(BF16)
    lo = (a - hi.astype(F32)).astype(BF16)
    return hi, lo


def _split3(a):
    hi = a.astype(BF16)
    r = a - hi.astype(F32)
    mid = r.astype(BF16)
    lo = (r - mid.astype(F32)).astype(BF16)
    return hi, mid, lo


def _dot3(a, b):
    ah, al = _split2(a)
    bh, bl = _split2(b)
    return _dot(ah, bh) + (_dot(ah, bl) + _dot(al, bh))


def _sigmoid(x):
    return 1.0 / (1.0 + jnp.exp(-x))


def _silu(x):
    return x * (0.5 * jnp.tanh(0.5 * x) + 0.5)


def _params(sem):
    return pltpu.CompilerParams(dimension_semantics=sem, vmem_limit_bytes=VMEM_LIMIT)


def _mod_body(c_ref, w_ref, b_ref, o_ref):
    c = c_ref[...]
    s = c * _sigmoid(c)
    o_ref[...] = _dot3(s, w_ref[...]) + b_ref[...]


def _mod_call(cc, w_mod, b_mod):
    rows = cc.shape[0]
    n = w_mod.shape[1]
    tn = 1024
    return pl.pallas_call(
        _mod_body,
        grid=(n // tn,),
        in_specs=[
            pl.BlockSpec((rows, D_MODEL), lambda j: (0, 0)),
            pl.BlockSpec((D_MODEL, tn), lambda j: (0, j)),
            pl.BlockSpec((1, tn), lambda j: (0, j)),
        ],
        out_specs=pl.BlockSpec((rows, tn), lambda j: (0, j)),
        out_shape=jax.ShapeDtypeStruct((rows, n), F32),
        compiler_params=_params(("arbitrary",)),
        name="mod",
    )(cc, w_mod, b_mod)


def _rms_rows(x, n):
    return x * lax.rsqrt(jnp.sum(x * x, axis=-1, keepdims=True) * (1.0 / n) + EPS)


def _rope(x, cos, sin):
    return x * cos + pltpu.roll(x, LANES // 2, 1) * sin


def _proj_body(*refs, latent):
    if latent:
        (x_ref, g_ref, sc_ref, sh_ref, win_ref, qag_ref, kvag_ref, wq_ref, wk_ref, wv_ref,
         qg_ref, kg_ref, qadd_ref, kadd_ref, cos_ref, sin_ref,
         q_out, k_out, v_out, f_out, ga_out, gb_out) = refs
    else:
        (x_ref, g_ref, sc_ref, sh_ref, win_ref, kvag_ref, wk_ref, wv_ref, kg_ref, kadd_ref,
         k_out, v_out) = refs
    x = x_ref[0]
    h = _rms_rows(x, D_MODEL) * (g_ref[...] * (1.0 + sc_ref[0])) + sh_ref[0]
    hb = h.astype(BF16)

    if latent:
        cos = cos_ref[...]
        sin = sin_ref[...]

    pe = _dot(hb, win_ref[:, PC_PE:PC_KV])

    ckv = _rms_rows(_dot(hb, win_ref[:, PC_KV:PC_F]), KV_LORA) * kvag_ref[...]
    ckvb = ckv.astype(BF16)
    kall = _dot(ckvb, wk_ref[...])
    kg = kg_ref[...]
    kadd = kadd_ref[...]
    for hd in range(MLA_HEADS):
        kh = kall[:, hd * HEAD_SLOT:(hd + 1) * HEAD_SLOT] + pe
        kh = _rms_rows(kh, QK_DIM) * kg
        if latent:
            kh = _rope(kh, cos, sin)
        k_out[0, hd] = (kh + kadd).astype(BF16)
    v = _dot(ckvb, wv_ref[...])
    for hp in range(MLA_HEADS // 2):
        v_out[0, hp] = v[:, hp * LANES:(hp + 1) * LANES].T.astype(BF16)

    if latent:
        cq = _rms_rows(_dot(hb, win_ref[:, PC_Q:PC_PE]), Q_LORA) * qag_ref[...]
        qall = _dot(cq.astype(BF16), wq_ref[...])
        qg = qg_ref[...]
        qadd = qadd_ref[...]
        for hd in range(MLA_HEADS):
            qh = qall[:, hd * HEAD_SLOT:(hd + 1) * HEAD_SLOT]
            q_out[0, hd] = (_rope(_rms_rows(qh, QK_DIM) * qg, cos, sin) + qadd).astype(BF16)
        f_out[0] = _dot(hb, win_ref[:, PC_F:PC_GA]).astype(BF16)
        ga_out[0] = _sigmoid(_dot(hb, win_ref[:, PC_GA:PC_GB])).astype(BF16)
        gb_out[0] = _sigmoid(_dot(hb, win_ref[:, PC_GB:PC_END])).astype(BF16)


def _full(shape):
    nd = len(shape)
    return pl.BlockSpec(shape, lambda *_: (0,) * nd)


def _proj_lat_call(x, g1n, sc1, sh1, w_in_p, qag, kvag, wq_p, wk_p, wv_p, qg_p, kg_p, qadd, kadd, cos_t, sin_t, tm):
    B, T, _ = x.shape
    H = MLA_HEADS
    tok = lambda w: pl.BlockSpec((1, tm, w), lambda b, j: (b, j, 0))
    per_b = pl.BlockSpec((1, 1, D_MODEL), lambda b, j: (b, 0, 0))
    tab = pl.BlockSpec((tm, HEAD_SLOT), lambda b, j: (j, 0))
    return pl.pallas_call(
        functools.partial(_proj_body, latent=True),
        grid=(B, T // tm),
        in_specs=[tok(D_MODEL), _full(g1n.shape), per_b, per_b, _full(w_in_p.shape), _full(qag.shape),
                  _full(kvag.shape), _full(wq_p.shape), _full(wk_p.shape), _full(wv_p.shape),
                  _full(qg_p.shape), _full(kg_p.shape), _full(qadd.shape), _full(kadd.shape), tab, tab],
        out_specs=[
            pl.BlockSpec((1, H, tm, HEAD_SLOT), lambda b, j: (b, 0, j, 0)),
            pl.BlockSpec((1, H, tm, HEAD_SLOT), lambda b, j: (b, 0, j, 0)),
            pl.BlockSpec((1, H // 2, LANES, tm), lambda b, j: (b, 0, 0, j)),
            tok(D_F), tok(D_MODEL), tok(D_MODEL),
        ],
        out_shape=[
            jax.ShapeDtypeStruct((B, H, T, HEAD_SLOT), BF16),
            jax.ShapeDtypeStruct((B, H, T, HEAD_SLOT), BF16),
            jax.ShapeDtypeStruct((B, H // 2, LANES, T), BF16),
            jax.ShapeDtypeStruct((B, T, D_F), BF16),
            jax.ShapeDtypeStruct((B, T, D_MODEL), BF16),
            jax.ShapeDtypeStruct((B, T, D_MODEL), BF16),
        ],
        compiler_params=_params(("parallel", "parallel")),
        name="proj_lat",
    )(x, g1n, sc1, sh1, w_in_p, qag, kvag, wq_p, wk_p, wv_p, qg_p, kg_p, qadd, kadd, cos_t, sin_t)


def _proj_ctx_call(ctx, g1n, csc1, csh1, w_in_p, kvag, wk_p, wv_p, kg_p, kadd):
    B, TC, _ = ctx.shape
    H = MLA_HEADS
    shared = pl.BlockSpec((1, 1, D_MODEL), lambda b: (0, 0, 0))
    return pl.pallas_call(
        functools.partial(_proj_body, latent=False),
        grid=(B,),
        in_specs=[pl.BlockSpec((1, TC, D_MODEL), lambda b: (b, 0, 0)), _full(g1n.shape), shared, shared,
                  _full(w_in_p.shape), _full(kvag.shape), _full(wk_p.shape), _full(wv_p.shape),
                  _full(kg_p.shape), _full(kadd.shape)],
        out_specs=[
            pl.BlockSpec((1, H, TC, HEAD_SLOT), lambda b: (b, 0, 0, 0)),
            pl.BlockSpec((1, H // 2, LANES, TC), lambda b: (b, 0, 0, 0)),
        ],
        out_shape=[
            jax.ShapeDtypeStruct((B, H, TC, HEAD_SLOT), BF16),
            jax.ShapeDtypeStruct((B, H // 2, LANES, TC), BF16),
        ],
        compiler_params=_params(("parallel",)),
        name="proj_ctx",
    )(ctx, g1n, csc1, csh1, w_in_p, kvag, wk_p, wv_p, kg_p, kadd)


def _attn_body(q_ref, kl_ref, kc_ref, vl_ref, vc_ref, o_ref):
    tq = q_ref.shape[2]
    T = kl_ref.shape[2]
    nck = T // KEY_CHUNK
    row = lax.broadcasted_iota(jnp.int32, (LANES, tq), 0)

    def probs(hd):
        q = q_ref[0, hd]
        den = jnp.zeros((1, tq), F32)
        ps = []
        for c in range(nck + 1):
            k = kc_ref[0, hd] if c == nck else kl_ref[0, hd, c * KEY_CHUNK:(c + 1) * KEY_CHUNK, :]
            e = jnp.exp2(_dot_nt(k, q))
            den = den + jnp.sum(e, axis=0, keepdims=True)
            ps.append(e.astype(BF16))
        return ps, den

    def weighted_values(hd, ps):
        hp = hd // 2
        o = _dot(vc_ref[0, hp], ps[nck])
        for c in range(nck):
            o = o + _dot(vl_ref[0, hp, :, c * KEY_CHUNK:(c + 1) * KEY_CHUNK], ps[c])
        return o

    outs, dens, prev = [], [], None
    for hd in range(MLA_HEADS + 1):
        cur = probs(hd) if hd < MLA_HEADS else None
        if prev is not None:
            outs.append(weighted_values(hd - 1, prev[0]) / prev[1])
            dens.append(prev[1])
        prev = cur
    for hp in range(MLA_HEADS // 2):
        o_ref[0, hp] = jnp.where(row < V_DIM, outs[2 * hp], outs[2 * hp + 1]).astype(BF16)
    min_den = functools.reduce(jnp.minimum, dens)

    @pl.when(jnp.min(min_den) < DEN_FLOOR)
    def _():
        def pair(hp, carry):
            res = []
            for hd in (2 * hp, 2 * hp + 1):
                q = q_ref[0, hd]
                sl = _dot_nt(kl_ref[0, hd], q)
                sc = _dot_nt(kc_ref[0, hd], q)
                m = jnp.maximum(jnp.max(sl, axis=0, keepdims=True), jnp.max(sc, axis=0, keepdims=True))
                el = jnp.exp2(sl - m)
                ec = jnp.exp2(sc - m)
                den = jnp.sum(el, axis=0, keepdims=True) + jnp.sum(ec, axis=0, keepdims=True)
                o = _dot(vl_ref[0, hp], el.astype(BF16)) + _dot(vc_ref[0, hp], ec.astype(BF16))
                res.append(o / den)
            o_ref[0, hp] = jnp.where(row < V_DIM, res[0], res[1]).astype(BF16)
            return carry

        lax.fori_loop(0, MLA_HEADS // 2, pair, 0)


def _attn_call(q, k_lat, k_ctx, vt_lat, vt_ctx, tq):
    B, H, T, _ = q.shape
    TC = k_ctx.shape[2]
    return pl.pallas_call(
        _attn_body,
        grid=(B, T // tq),
        in_specs=[
            pl.BlockSpec((1, H, tq, HEAD_SLOT), lambda b, j: (b, 0, j, 0)),
            pl.BlockSpec((1, H, T, HEAD_SLOT), lambda b, j: (b, 0, 0, 0)),
            pl.BlockSpec((1, H, TC, HEAD_SLOT), lambda b, j: (b, 0, 0, 0)),
            pl.BlockSpec((1, H // 2, LANES, T), lambda b, j: (b, 0, 0, 0)),
            pl.BlockSpec((1, H // 2, LANES, TC), lambda b, j: (b, 0, 0, 0)),
        ],
        out_specs=pl.BlockSpec((1, H // 2, LANES, tq), lambda b, j: (b, 0, 0, j)),
        out_shape=jax.ShapeDtypeStruct((B, H // 2, LANES, T), BF16),
        compiler_params=_params(("parallel", "parallel")),
        name="attn",
    )(q, k_lat, k_ctx, vt_lat, vt_ctx)


def _cadd(a, b):
    return a[0] + b[0], a[1] + b[1]


def _csub(a, b):
    return a[0] - b[0], a[1] - b[1]


def _cmul_neg_i(a):
    return a[1], -a[0]


def _cmul_pos_i(a):
    return -a[1], a[0]


def _fft8(u):
    r = 0.7071067811865476
    a0, a1 = _cadd(u[0], u[4]), _csub(u[0], u[4])
    a2, a3 = _cadd(u[2], u[6]), _csub(u[2], u[6])
    a4, a5 = _cadd(u[1], u[5]), _csub(u[1], u[5])
    a6, a7 = _cadd(u[3], u[7]), _csub(u[3], u[7])
    e0, e2 = _cadd(a0, a2), _csub(a0, a2)
    e1, e3 = _cadd(a1, _cmul_neg_i(a3)), _cadd(a1, _cmul_pos_i(a3))
    o0, o2 = _cadd(a4, a6), _csub(a4, a6)
    o1, o3 = _cadd(a5, _cmul_neg_i(a7)), _cadd(a5, _cmul_pos_i(a7))
    wo1 = (r * (o1[0] + o1[1]), r * (o1[1] - o1[0]))
    wo3 = (r * (o3[1] - o3[0]), -r * (o3[0] + o3[1]))
    return [_cadd(e0, o0), _cadd(e1, wo1), _cadd(e2, _cmul_neg_i(o2)), _cadd(e3, wo3),
            _csub(e0, o0), _csub(e1, wo1), _cadd(e2, _cmul_pos_i(o2)), _csub(e3, wo3)]


def _fourier_body(f_ref, cc_ref, twc_ref, tws_ref, m2_ref, o_ref, b_ref, y_ref):
    for g in range(F_GROUPS):
        cols = slice(g * F_GROUP_DIM, (g + 1) * F_GROUP_DIM)
        y = _dot(f_ref[0, :, cols], cc_ref[...])
        u = [(y[j * FFT_N:(j + 1) * FFT_N, :F_GROUP_DIM], y[j * FFT_N:(j + 1) * FFT_N, F_GROUP_DIM:])
             for j in range(FFT_R)]
        a = _fft8(u)
        for k1 in range(FFT_R):
            ar, ai = a[k1]
            if k1 > 0:
                c = twc_ref[k1 * FFT_N:(k1 + 1) * FFT_N, :]
                s = tws_ref[k1 * FFT_N:(k1 + 1) * FFT_N, :]
                ar, ai = ar * c + ai * s, ai * c - ar * s
            b_ref[k1, 0:FFT_N, cols] = ar.astype(BF16)
            b_ref[k1, FFT_N:2 * FFT_N, cols] = ai.astype(BF16)
    for k1 in range(FFT_R):
        y = _dot(m2_ref[...], b_ref[k1])
        for g in range(F_GROUPS):
            y_ref[g, pl.ds(k1, FFT_N, stride=FFT_R), :] = y[:, g * F_GROUP_DIM:(g + 1) * F_GROUP_DIM]
    for g in range(F_GROUPS):
        o_ref[0, :, g * F_GROUP_DIM:(g + 1) * F_GROUP_DIM] = y_ref[g].astype(BF16)


def _fourier_call(f, cc, twc, tws, m2):
    B, T, _ = f.shape
    return pl.pallas_call(
        _fourier_body,
        grid=(B,),
        in_specs=[pl.BlockSpec((1, T, D_F), lambda b: (b, 0, 0)), _full(cc.shape), _full(twc.shape),
                  _full(tws.shape), _full(m2.shape)],
        out_specs=pl.BlockSpec((1, T, D_F), lambda b: (b, 0, 0)),
        out_shape=jax.ShapeDtypeStruct((B, T, D_F), BF16),
        scratch_shapes=[pltpu.VMEM((FFT_R, 2 * FFT_N, D_F), BF16), pltpu.VMEM((F_GROUPS, T, F_GROUP_DIM), F32)],
        compiler_params=_params(("parallel",)),
        name="fourier",
    )(f, cc, twc, tws, m2)


def _merge_body(at_ref, fo_ref, ga_ref, gb_ref, x_ref, g1_ref, g2n_ref, sc2_ref, sh2_ref,
                wo_ref, wf_ref, wout_ref, wrh_ref, wrl_ref, x1_out, h2_out, aff_out):
    tm = x_ref.shape[1]
    attn_t = jnp.concatenate([at_ref[0, hp] for hp in range(MLA_HEADS // 2)], axis=0)
    a = lax.dot_general(attn_t, wo_ref[...], (((0,), (0,)), ((), ())), preferred_element_type=F32)
    fo = _dot(fo_ref[0], wf_ref[...])
    mix = ga_ref[0].astype(F32) * a + gb_ref[0].astype(F32) * fo
    y = _dot(mix.astype(BF16), wout_ref[...])
    x1 = x_ref[0] + g1_ref[0] * y
    x1_out[0] = x1
    h2 = _rms_rows(x1, D_MODEL) * g2n_ref[...]
    h2 = h2 * (1.0 + sc2_ref[0]) + sh2_ref[0]
    for c in range(D_MODEL // LANES):
        h2_out[0, pl.ds(c, tm, stride=D_MODEL // LANES), :] = h2[:, c * LANES:(c + 1) * LANES]
    hh, hl = _split2(h2)
    p_hi = _dot(hh, wrl_ref[...])
    logits = p_hi + pltpu.roll(p_hi, LANES - N_EXPERTS, 1) + _dot(hl, wrh_ref[...])
    lane = lax.broadcasted_iota(jnp.int32, logits.shape, 1)
    logits = jnp.where(lane < N_EXPERTS, logits, -1e30)
    ex = jnp.exp(logits - jnp.max(logits, axis=-1, keepdims=True))
    aff_out[0] = ex / jnp.sum(ex, axis=-1, keepdims=True)


def _merge_call(attn, four, ga, gb, x, g1, g2n, sc2, sh2, wo, wf, wout, wrh, wrl, tm):
    B, T, _ = x.shape
    H = MLA_HEADS
    tok = lambda w: pl.BlockSpec((1, tm, w), lambda b, j: (b, j, 0))
    per_b = pl.BlockSpec((1, 1, D_MODEL), lambda b, j: (b, 0, 0))
    nchunk = D_MODEL // LANES
    return pl.pallas_call(
        _merge_body,
        grid=(B, T // tm),
        in_specs=[
            pl.BlockSpec((1, H // 2, LANES, tm), lambda b, j: (b, 0, 0, j)),
            tok(D_F), tok(D_MODEL), tok(D_MODEL), tok(D_MODEL), per_b, _full(g2n.shape), per_b, per_b,
            _full(wo.shape), _full(wf.shape), _full(wout.shape), _full(wrh.shape), _full(wrl.shape),
        ],
        out_specs=[
            tok(D_MODEL),
            pl.BlockSpec((1, tm * nchunk, LANES), lambda b, j: (b, j, 0)),
            tok(LANES),
        ],
        out_shape=[
            jax.ShapeDtypeStruct((B, T, D_MODEL), F32),
            jax.ShapeDtypeStruct((B, T * nchunk, LANES), F32),
            jax.ShapeDtypeStruct((B, T, LANES), F32),
        ],
        compiler_params=_params(("parallel", "parallel")),
        name="merge",
    )(attn, four, ga, gb, x, g1, g2n, sc2, sh2, wo, wf, wout, wrh, wrl)


def _prefix_excl(m, tri):
    outs = []
    run = jnp.zeros((m.shape[0], 1), F32)
    for blk in range(m.shape[1] // LANES):
        mb = m[:, blk * LANES:(blk + 1) * LANES]
        inc = _dot(mb.astype(BF16), tri)
        outs.append(inc - mb + run)
        run = run + inc[:, LANES - 1:LANES]
    return jnp.concatenate(outs, axis=1)


def _select_body(aff_ref, tri_ref, pos_ref, idx_out, gate_out, *, cap):
    T = aff_ref.shape[1]
    aff = aff_ref[0].T[0:N_EXPERTS]

    def search(carry):
        base, step, rounds, _ = carry
        thr = base
        for j in range(1, 2 ** SEARCH_BITS):
            cand = base + float(j) * step
            cnt = jnp.sum((aff >= cand).astype(F32), axis=-1, keepdims=True)
            thr = jnp.where(cnt >= float(cap), cand, thr)
        settled = jnp.where((thr > 0.0) & (step < thr * SEARCH_DONE), 1.0, 0.0)
        return thr, step * (0.5 ** SEARCH_BITS), rounds + 1, (jnp.min(settled) < 1.0).astype(jnp.int32)

    init = (jnp.zeros((N_EXPERTS, 1), F32), jnp.full((N_EXPERTS, 1), 2.0 * 0.5 ** SEARCH_BITS, F32),
            jnp.int32(0), jnp.int32(1))
    thr = lax.while_loop(lambda c: (c[3] > 0) & (c[2] < SEARCH_MAX_ROUNDS), search, init)[0]
    gt = (aff > thr).astype(F32)
    eq = (aff == thr).astype(F32)
    need = cap - jnp.sum(gt, axis=-1, keepdims=True)
    tri = tri_ref[...]
    sel = gt + eq * (_prefix_excl(eq, tri) < need).astype(F32)
    slot = jnp.where(sel > 0.0, _prefix_excl(sel, tri), -1.0)

    pos = pos_ref[...]
    srow = lax.broadcasted_iota(jnp.int32, (cap, T), 0).astype(F32)
    zero = jnp.zeros((3, T), BF16)
    for e in range(N_EXPERTS):
        onehot = jnp.where(srow == slot[e:e + 1, :], 1.0, 0.0).astype(BF16)
        ah, am, al = _split3(aff[e:e + 1, :])
        vals = jnp.concatenate([pos, ah, am, al, zero], axis=0)
        res = _dot_nt(vals, onehot)
        idx_out[0, e:e + 1, :] = (res[0:1] * float(LANES) + res[1:2]).astype(jnp.int32)
        gate_out[0, e:e + 1, :] = res[2:3] + (res[3:4] + res[4:5])


def _select_call(aff, tri, pos, cap):
    B, T, _ = aff.shape
    return pl.pallas_call(
        functools.partial(_select_body, cap=cap),
        grid=(B,),
        in_specs=[pl.BlockSpec((1, T, LANES), lambda b: (b, 0, 0)), _full(tri.shape), _full(pos.shape)],
        out_specs=[pl.BlockSpec((1, N_EXPERTS, cap), lambda b: (b, 0, 0)),
                   pl.BlockSpec((1, N_EXPERTS, cap), lambda b: (b, 0, 0))],
        out_shape=[jax.ShapeDtypeStruct((B, N_EXPERTS, cap), jnp.int32),
                   jax.ShapeDtypeStruct((B, N_EXPERTS, cap), F32)],
        compiler_params=_params(("parallel",)),
        name="select",
    )(aff, tri, pos)


def _moe_body(ip_ref, i0_ref, i1_ref, in_ref, g0_ref, g1_ref, h2_ref, wg_ref, wu_ref, wd_ref, acc_ref,
              xg_a, xg_b, ye_a, ye_b, *, cap, stride):
    k = pl.program_id(1)
    nchunk = D_MODEL // LANES
    group = 8

    def zero_of(tile):
        bits = pltpu.bitcast(tile, jnp.uint32)
        return pltpu.bitcast(lax.shift_right_logical(bits, jnp.uint32(32)), F32)

    def gather(i_ref, xg_ref):
        def rows(s0, hold):
            for s in range(s0, s0 + group):
                base = pl.multiple_of(i_ref[0, 0, 0, s], SUBLANES)
                tile = h2_ref[0, pl.ds(base, nchunk), :] + hold
                xg_ref[pl.ds(s, nchunk, stride=stride), :] = tile
            return tile
        return [functools.partial(rows, s0) for s0 in range(0, cap, group)]

    def scatter(i_ref, ye_ref):
        def rows(s0, hold):
            bases, news = [], []
            for s in range(s0, s0 + group):
                base = pl.multiple_of(i_ref[0, 0, 0, s], SUBLANES)
                bases.append(base)
                news.append(acc_ref[0, pl.ds(base, nchunk), :]
                            + (ye_ref[pl.ds(s, nchunk, stride=stride), :] + hold))
            for base, new in zip(bases, news):
                acc_ref[0, pl.ds(base, nchunk), :] = new
            return news[-1]
        return [functools.partial(rows, s0) for s0 in range(0, cap, group)]

    def expert(j, g_ref, xg_ref, ye_ref):
        state = {}
        n_mid = D_EXPERT // MXU_TILE
        n_out = D_MODEL // MXU_TILE

        def load(zero):
            xb = jnp.concatenate([xg_ref[c * stride:c * stride + cap, :] for c in range(nchunk)], axis=1)
            state["x"] = xb.astype(BF16)
            state["gate"] = jnp.broadcast_to(g_ref[0, 0], (LANES, cap)).T[:, 0:1]
            state["h"] = []
            return xb[0:SUBLANES, 0:LANES]

        def mid(n, zero):
            cols = slice(n * MXU_TILE, (n + 1) * MXU_TILE)
            a = _dot(state["x"], wg_ref[j, :, cols]) + zero
            u = _dot(state["x"], wu_ref[j, :, cols])
            h = _silu(a) * u
            state["h"].append(h.astype(BF16))
            return h[0:SUBLANES, 0:LANES]

        def out(n, zero):
            cols = slice(n * MXU_TILE, (n + 1) * MXU_TILE)
            ye = _dot(state["h"][0], wd_ref[j, 0:MXU_TILE, cols])
            for m in range(1, n_mid):
                ye = ye + _dot(state["h"][m], wd_ref[j, m * MXU_TILE:(m + 1) * MXU_TILE, cols])
            ye = ye * state["gate"] + zero
            for c in range(MXU_TILE // LANES):
                cc = n * (MXU_TILE // LANES) + c
                ye_ref[cc * stride:cc * stride + cap, :] = ye[:, c * LANES:(c + 1) * LANES]
            return ye[0:SUBLANES, 0:LANES]

        return ([load] + [functools.partial(mid, n) for n in range(n_mid)]
                + [functools.partial(out, n) for n in range(n_out)])

    def paced(main, side):
        none = jnp.zeros((SUBLANES, LANES), F32)
        hold = none
        edges = [none, none, none]
        done = 0
        for i, piece in enumerate(main):
            zero = jnp.concatenate([edges[0][0:1, :]] * (MXU_TILE // LANES), axis=1)
            witness = piece(zero)
            upto = (len(side) * (i + 1)) // len(main)
            tiles = [s(hold) for s in side[done:upto]]
            done = upto
            hold = zero_of(witness)
            edges = edges[1:] + [zero_of(functools.reduce(jnp.add, tiles)) if tiles else none]

    def zipped(a, b):
        return [f for pair in zip(a, b) for f in pair]

    @pl.when(k == 0)
    def _():
        acc_ref[...] = jnp.zeros_like(acc_ref)
        ye_b[...] = jnp.zeros_like(ye_b)
        for piece in gather(i0_ref, xg_a):
            piece(jnp.zeros((SUBLANES, LANES), F32))

    paced(expert(0, g0_ref, xg_a, ye_a), zipped(scatter(ip_ref, ye_b), gather(i1_ref, xg_b)))
    paced(expert(1, g1_ref, xg_b, ye_b), zipped(scatter(i0_ref, ye_a), gather(in_ref, xg_a)))

    @pl.when(k == pl.num_programs(1) - 1)
    def _():
        for piece in scatter(i1_ref, ye_b):
            piece(jnp.zeros((SUBLANES, LANES), F32))


def _moe_call(idx, gate, h2t, wg, wu, wd, T):
    B, E, cap = idx.shape
    nchunk = D_MODEL // LANES
    stride = cap + SUBLANES
    buf = pltpu.VMEM((nchunk * stride, LANES), F32)
    idx4 = idx.reshape(B, E, 1, cap)
    gate4 = gate.reshape(B, E, 1, cap)

    def rows(expert_of_step, memory_space=None):
        spec = pl.BlockSpec((1, 1, 1, cap), lambda b, k: (b, expert_of_step(k), 0, 0))
        return spec if memory_space is None else pl.BlockSpec(
            (1, 1, 1, cap), lambda b, k: (b, expert_of_step(k), 0, 0), memory_space=memory_space)

    return pl.pallas_call(
        functools.partial(_moe_body, cap=cap, stride=stride),
        grid=(B, E // 2),
        in_specs=[
            rows(lambda k: jnp.maximum(2 * k - 1, 0), pltpu.SMEM),
            rows(lambda k: 2 * k, pltpu.SMEM),
            rows(lambda k: 2 * k + 1, pltpu.SMEM),
            rows(lambda k: jnp.minimum(2 * k + 2, E - 1), pltpu.SMEM),
            rows(lambda k: 2 * k),
            rows(lambda k: 2 * k + 1),
            pl.BlockSpec((1, T * nchunk, LANES), lambda b, k: (b, 0, 0)),
            pl.BlockSpec((2, D_MODEL, D_EXPERT), lambda b, k: (k, 0, 0)),
            pl.BlockSpec((2, D_MODEL, D_EXPERT), lambda b, k: (k, 0, 0)),
            pl.BlockSpec((2, D_EXPERT, D_MODEL), lambda b, k: (k, 0, 0)),
        ],
        out_specs=pl.BlockSpec((1, T * nchunk, LANES), lambda b, k: (b, 0, 0)),
        out_shape=jax.ShapeDtypeStruct((B, T * nchunk, LANES), F32),
        scratch_shapes=[buf, buf, buf, buf],
        compiler_params=_params(("parallel", "arbitrary")),
        name="moe",
    )(idx4, idx4, idx4, idx4, gate4, gate4, h2t, wg, wu, wd)


def _final_body(x1_ref, y_ref, g2_ref, o_ref):
    tm = x1_ref.shape[1]
    for c in range(D_MODEL // LANES):
        cols = slice(c * LANES, (c + 1) * LANES)
        y = y_ref[0, pl.ds(c, tm, stride=D_MODEL // LANES), :]
        o_ref[0, :, cols] = x1_ref[0, :, cols] + g2_ref[0, :, cols] * y


def _final_call(x1, y, g2, tm):
    B, T, _ = x1.shape
    nchunk = D_MODEL // LANES
    tok = pl.BlockSpec((1, tm, D_MODEL), lambda b, j: (b, j, 0))
    return pl.pallas_call(
        _final_body,
        grid=(B, T // tm),
        in_specs=[tok,
                  pl.BlockSpec((1, tm * nchunk, LANES), lambda b, j: (b, j, 0)),
                  pl.BlockSpec((1, 1, D_MODEL), lambda b, j: (b, 0, 0))],
        out_specs=tok,
        out_shape=jax.ShapeDtypeStruct((B, T, D_MODEL), F32),
        compiler_params=_params(("parallel", "parallel")),
        name="final",
    )(x1, y, g2)


def _slot_layout():
    half = QK_ROPE // 2
    n_freq = half // 2
    src = np.full((HEAD_SLOT,), -1, np.int64)
    for axis in range(2):
        first = QK_NOPE + axis * half
        src[axis * n_freq:(axis + 1) * n_freq] = np.arange(first, first + n_freq)
        src[LANES // 2 + axis * n_freq:LANES // 2 + (axis + 1) * n_freq] = np.arange(first + n_freq, first + half)
    src[half:half + QK_NOPE // 2] = np.arange(0, QK_NOPE // 2)
    src[LANES // 2 + half:LANES // 2 + half + QK_NOPE // 2] = np.arange(QK_NOPE // 2, QK_NOPE)
    assert src[SHIFT_LANE] == -1 and sorted(src[src >= 0]) == list(range(QK_DIM))
    return src


_SLOT_SRC = _slot_layout()


def _rope_tables(T):
    half = QK_ROPE // 2
    n_freq = half // 2
    inv_freq = 1.0 / (ROPE_THETA ** (np.arange(n_freq, dtype=np.float32) / n_freq))
    t = np.arange(T)
    cos = np.ones((T, HEAD_SLOT), np.float32)
    sin = np.zeros((T, HEAD_SLOT), np.float32)
    for axis, pos in enumerate((t // GRID_W, t % GRID_W)):
        ang = pos.astype(np.float32)[:, None] * inv_freq[None, :].astype(np.float32)
        c, s = np.cos(ang).astype(np.float32), np.sin(ang).astype(np.float32)
        lo1 = axis * n_freq
        lo2 = LANES // 2 + axis * n_freq
        cos[:, lo1:lo1 + n_freq] = c
        cos[:, lo2:lo2 + n_freq] = c
        sin[:, lo1:lo1 + n_freq] = -s
        sin[:, lo2:lo2 + n_freq] = s
    return jnp.asarray(cos), jnp.asarray(sin)


def _dft_tables(T):
    assert T == FFT_R * FFT_N
    n = np.arange(F_GROUP_DIM)
    ang_c = 2.0 * np.pi * ((n[:, None] * n[None, :]) % F_GROUP_DIM) / F_GROUP_DIM
    cc = np.concatenate([np.cos(ang_c), -np.sin(ang_c)], axis=1) / np.sqrt(F_GROUP_DIM)
    k1 = np.repeat(np.arange(FFT_R), FFT_N)
    n2 = np.tile(np.arange(FFT_N), FFT_R)
    ang_w = 2.0 * np.pi * (k1 * n2) / T
    twc = np.broadcast_to(np.cos(ang_w)[:, None], (T, F_GROUP_DIM))
    tws = np.broadcast_to(np.sin(ang_w)[:, None], (T, F_GROUP_DIM))
    m = np.arange(FFT_N)
    ang_2 = 2.0 * np.pi * ((m[:, None] * m[None, :]) % FFT_N) / FFT_N
    m2 = np.concatenate([np.cos(ang_2), np.sin(ang_2)], axis=1) / np.sqrt(T)
    return (jnp.asarray(cc, F32).astype(BF16), jnp.asarray(twc, F32), jnp.asarray(tws, F32),
            jnp.asarray(m2, F32).astype(BF16))


def kernel(x, c, ctx, c_ctx, w_mod, b_mod, norm1_g, w_in, q_a_norm_g, kv_a_norm_g, w_q_up, w_kv_up,
           q_norm_g, k_norm_g, w_o_attn, w_fourier, w_out, norm2_g, w_router, w_e_gate, w_e_up, w_e_down):
    B, T, D = x.shape
    assert w_mod.shape[0] == 1 and D == D_MODEL and T % 1024 == 0
    H = MLA_HEADS
    cap = (CAPACITY_FACTOR * T) // N_EXPERTS

    rows = -(-(B + 1) // SUBLANES) * SUBLANES
    cc_in = jnp.concatenate([c, c_ctx[None, :], jnp.zeros((rows - B - 1, D), F32)], axis=0)
    mod = _mod_call(cc_in, w_mod[0], b_mod)
    sh1, sc1, g1, sh2, sc2, g2 = [mod[:B, i * D:(i + 1) * D].reshape(B, 1, D) for i in range(6)]
    csh1 = mod[B:B + 1, 0:D].reshape(1, 1, D)
    csc1 = mod[B:B + 1, D:2 * D].reshape(1, 1, D)

    src = _SLOT_SRC
    used = jnp.asarray(src >= 0, F32)
    nope = jnp.asarray((src >= 0) & (src < QK_NOPE), F32)
    rope = jnp.asarray(src >= QK_NOPE, F32)
    lane_src = np.maximum(src, 0)
    wi = w_in[0]
    pe_cols = jnp.take(wi[:, OFF_KPE:OFF_F], np.maximum(src - QK_NOPE, 0), axis=1) * rope
    w_in_p = jnp.concatenate([wi[:, OFF_Q:OFF_KV], pe_cols, wi[:, OFF_KV:OFF_KPE], wi[:, OFF_F:N_IN]], axis=1)
    w_in_p = w_in_p.astype(BF16)
    wq_p = jnp.take(w_q_up[0].reshape(Q_LORA, H, QK_DIM), lane_src, axis=2) * used
    wq_p = wq_p.reshape(Q_LORA, H * HEAD_SLOT).astype(BF16)
    wkv = w_kv_up[0].reshape(KV_LORA, H, QK_NOPE + V_DIM)
    wk_p = jnp.take(wkv[:, :, :QK_NOPE], np.minimum(lane_src, QK_NOPE - 1), axis=2) * nope
    wk_p = wk_p.reshape(KV_LORA, H * HEAD_SLOT).astype(BF16)
    wv_p = wkv[:, :, QK_NOPE:].reshape(KV_LORA, H * V_DIM).astype(BF16)
    qg_p = (jnp.take(q_norm_g[0], lane_src) * used * (QK_DIM ** -0.5 * LOG2E)).reshape(1, HEAD_SLOT)
    kg_p = (jnp.take(k_norm_g[0], lane_src) * used).reshape(1, HEAD_SLOT)
    bound = (QK_DIM * BOUND_SLACK) * jnp.max(jnp.abs(qg_p)) * jnp.max(jnp.abs(kg_p))
    on_shift = jnp.asarray(np.arange(HEAD_SLOT) == SHIFT_LANE, F32).reshape(1, HEAD_SLOT)
    qadd = on_shift * (-bound)
    kadd = on_shift
    qag = q_a_norm_g[0].reshape(1, Q_LORA)
    kvag = kv_a_norm_g[0].reshape(1, KV_LORA)
    g1n = norm1_g[0].reshape(1, D)
    g2n = norm2_g[0].reshape(1, D)
    cos_t, sin_t = _rope_tables(T)

    q, k_lat, vt_lat, f, ga, gb = _proj_lat_call(x, g1n, sc1, sh1, w_in_p, qag, kvag, wq_p, wk_p, wv_p,
                                                 qg_p, kg_p, qadd, kadd, cos_t, sin_t, tm=512)
    k_ctx, vt_ctx = _proj_ctx_call(ctx, g1n, csc1, csh1, w_in_p, kvag, wk_p, wv_p, kg_p, kadd)
    attn = _attn_call(q, k_lat, k_ctx, vt_lat, vt_ctx, tq=512)
    four = _fourier_call(f, *_dft_tables(T))

    wr = jnp.pad(w_router[0], ((0, 0), (0, LANES - N_EXPERTS)))
    wrh = wr.astype(BF16)
    wrl = wrh + jnp.roll((wr - wrh.astype(F32)).astype(BF16), N_EXPERTS, axis=1)
    x1, h2t, aff = _merge_call(attn, four, ga, gb, x, g1, g2n, sc2, sh2, w_o_attn[0].astype(BF16),
                               w_fourier[0].astype(BF16), w_out[0].astype(BF16), wrh, wrl, tm=512)

    tri = jnp.asarray(np.triu(np.ones((LANES, LANES), np.float32)), BF16)
    addr = np.arange(T) * (D // LANES)
    pos = jnp.asarray(np.stack([addr // LANES, addr % LANES]).astype(np.float32), BF16)
    idx, gate = _select_call(aff, tri, pos, cap)
    y = _moe_call(idx, gate, h2t, w_e_gate[0].astype(BF16), w_e_up[0].astype(BF16),
                  w_e_down[0].astype(BF16), T)
    return _final_call(x1, y, g2, tm=1024)
```

```python
import functools

import numpy as np
import jax
import jax.numpy as jnp
from jax import lax
from jax.experimental import pallas as pl
from jax.experimental.pallas import tpu as pltpu

F32 = jnp.float32
BF16 = jnp.bfloat16

D_MODEL = 1024
GRID_W = 64
MLA_HEADS = 8
QK_NOPE = 64
QK_ROPE = 32
QK_DIM = QK_NOPE + QK_ROPE
V_DIM = 64
Q_LORA = 384
KV_LORA = 256
ROPE_THETA = 10000.0
F_GROUPS = 4
F_GROUP_DIM = 128
D_F = F_GROUPS * F_GROUP_DIM
OFF_Q = 0
OFF_KV = OFF_Q + Q_LORA
OFF_KPE = OFF_KV + KV_LORA
OFF_F = OFF_KPE + QK_ROPE
OFF_GA = OFF_F + D_F
OFF_GB = OFF_GA + D_MODEL
N_IN = OFF_GB + D_MODEL
N_EXPERTS = 16
D_EXPERT = 512
CAPACITY_FACTOR = 2
EPS = 1e-6

LANES = 128
SUBLANES = 8
MXU_TILE = 256
HEAD_SLOT = LANES
SHIFT_LANE = 48
BOUND_SLACK = 1.02
DEN_FLOOR = 2.0 ** -60
LOG2E = 1.4426950408889634
FFT_R = 8
FFT_N = 256
SEARCH_BITS = 3
SEARCH_DONE = 2.0 ** -30
SEARCH_MAX_ROUNDS = 56
KEY_CHUNK = 256
VMEM_LIMIT = 56 * 1024 * 1024

PC_Q = 0
PC_PE = PC_Q + Q_LORA
PC_KV = PC_PE + HEAD_SLOT
PC_F = PC_KV + KV_LORA
PC_GA = PC_F + D_F
PC_GB = PC_GA + D_MODEL
PC_END = PC_GB + D_MODEL


def _dot(a, b):
    return jnp.dot(a, b, preferred_element_type=F32)


def _dot_nt(a, b):
    return lax.dot_general(a, b, (((1,), (1,)), ((), ())), preferred_element_type=F32)


def _split2(a):
    hi = a.astype(BF16)
    lo = (a - hi.astype(F32)).astype(BF16)
    return hi, lo


def _split3(a):
    hi = a.astype(BF16)
    r = a - hi.astype(F32)
    mid = r.astype(BF16)
    lo = (r - mid.astype(F32)).astype(BF16)
    return hi, mid, lo


def _dot3(a, b):
    ah, al = _split2(a)
    bh, bl = _split2(b)
    return _dot(ah, bh) + (_dot(ah, bl) + _dot(al, bh))


def _sigmoid(x):
    return 1.0 / (1.0 + jnp.exp(-x))


def _silu(x):
    return x * (0.5 * jnp.tanh(0.5 * x) + 0.5)


def _params(sem):
    return pltpu.CompilerParams(dimension_semantics=sem, vmem_limit_bytes=VMEM_LIMIT)


def _mod_body(c_ref, w_ref, b_ref, o_ref):
    c = c_ref[...]
    s = c * _sigmoid(c)
    o_ref[...] = _dot3(s, w_ref[...]) + b_ref[...]


def _mod_call(cc, w_mod, b_mod):
    rows = cc.shape[0]
    n = w_mod.shape[1]
    tn = 1024
    return pl.pallas_call(
        _mod_body,
        grid=(n // tn,),
        in_specs=[
            pl.BlockSpec((rows, D_MODEL), lambda j: (0, 0)),
            pl.BlockSpec((D_MODEL, tn), lambda j: (0, j)),
            pl.BlockSpec((1, tn), lambda j: (0, j)),
        ],
        out_specs=pl.BlockSpec((rows, tn), lambda j: (0, j)),
        out_shape=jax.ShapeDtypeStruct((rows, n), F32),
        compiler_params=_params(("arbitrary",)),
        name="mod",
    )(cc, w_mod, b_mod)


def _rms_rows(x, n):
    return x * lax.rsqrt(jnp.sum(x * x, axis=-1, keepdims=True) * (1.0 / n) + EPS)


def _rope(x, cos, sin):
    return x * cos + pltpu.roll(x, LANES // 2, 1) * sin


def _proj_body(*refs, latent):
    if latent:
        (x_ref, g_ref, sc_ref, sh_ref, win_ref, qag_ref, kvag_ref, wq_ref, wk_ref, wv_ref,
         qg_ref, kg_ref, qadd_ref, kadd_ref, cos_ref, sin_ref,
         q_out, k_out, v_out, f_out, ga_out, gb_out) = refs
    else:
        (x_ref, g_ref, sc_ref, sh_ref, win_ref, kvag_ref, wk_ref, wv_ref, kg_ref, kadd_ref,
         k_out, v_out) = refs
    x = x_ref[0]
    h = _rms_rows(x, D_MODEL) * (g_ref[...] * (1.0 + sc_ref[0])) + sh_ref[0]
    hb = h.astype(BF16)

    if latent:
        cos = cos_ref[...]
        sin = sin_ref[...]

    pe = _dot(hb, win_ref[:, PC_PE:PC_KV])

    ckv = _rms_rows(_dot(hb, win_ref[:, PC_KV:PC_F]), KV_LORA) * kvag_ref[...]
    ckvb = ckv.astype(BF16)
    kall = _dot(ckvb, wk_ref[...])
    kg = kg_ref[...]
    kadd = kadd_ref[...]
    for hd in range(MLA_HEADS):
        kh = kall[:, hd * HEAD_SLOT:(hd + 1) * HEAD_SLOT] + pe
        kh = _rms_rows(kh, QK_DIM) * kg
        if latent:
            kh = _rope(kh, cos, sin)
        k_out[0, hd] = (kh + kadd).astype(BF16)
    v = _dot(ckvb, wv_ref[...])
    for hp in range(MLA_HEADS // 2):
        v_out[0, hp] = v[:, hp * LANES:(hp + 1) * LANES].T.astype(BF16)

    if latent:
        cq = _rms_rows(_dot(hb, win_ref[:, PC_Q:PC_PE]), Q_LORA) * qag_ref[...]
        qall = _dot(cq.astype(BF16), wq_ref[...])
        qg = qg_ref[...]
        qadd = qadd_ref[...]
        for hd in range(MLA_HEADS):
            qh = qall[:, hd * HEAD_SLOT:(hd + 1) * HEAD_SLOT]
            q_out[0, hd] = (_rope(_rms_rows(qh, QK_DIM) * qg, cos, sin) + qadd).astype(BF16)
        f_out[0] = _dot(hb, win_ref[:, PC_F:PC_GA]).astype(BF16)
        ga_out[0] = _sigmoid(_dot(hb, win_ref[:, PC_GA:PC_GB])).astype(BF16)
        gb_out[0] = _sigmoid(_dot(hb, win_ref[:, PC_GB:PC_END])).astype(BF16)


def _full(shape):
    nd = len(shape)
    return pl.BlockSpec(shape, lambda *_: (0,) * nd)


def _proj_lat_call(x, g1n, sc1, sh1, w_in_p, qag, kvag, wq_p, wk_p, wv_p, qg_p, kg_p, qadd, kadd, cos_t, sin_t, tm):
    B, T, _ = x.shape
    H = MLA_HEADS
    tok = lambda w: pl.BlockSpec((1, tm, w), lambda b, j: (b, j, 0))
    per_b = pl.BlockSpec((1, 1, D_MODEL), lambda b, j: (b, 0, 0))
    tab = pl.BlockSpec((tm, HEAD_SLOT), lambda b, j: (j, 0))
    return pl.pallas_call(
        functools.partial(_proj_body, latent=True),
        grid=(B, T // tm),
        in_specs=[tok(D_MODEL), _full(g1n.shape), per_b, per_b, _full(w_in_p.shape), _full(qag.shape),
                  _full(kvag.shape), _full(wq_p.shape), _full(wk_p.shape), _full(wv_p.shape),
                  _full(qg_p.shape), _full(kg_p.shape), _full(qadd.shape), _full(kadd.shape), tab, tab],
        out_specs=[
            pl.BlockSpec((1, H, tm, HEAD_SLOT), lambda b, j: (b, 0, j, 0)),
            pl.BlockSpec((1, H, tm, HEAD_SLOT), lambda b, j: (b, 0, j, 0)),
            pl.BlockSpec((1, H // 2, LANES, tm), lambda b, j: (b, 0, 0, j)),
            tok(D_F), tok(D_MODEL), tok(D_MODEL),
        ],
        out_shape=[
            jax.ShapeDtypeStruct((B, H, T, HEAD_SLOT), BF16),
            jax.ShapeDtypeStruct((B, H, T, HEAD_SLOT), BF16),
            jax.ShapeDtypeStruct((B, H // 2, LANES, T), BF16),
            jax.ShapeDtypeStruct((B, T, D_F), BF16),
            jax.ShapeDtypeStruct((B, T, D_MODEL), BF16),
            jax.ShapeDtypeStruct((B, T, D_MODEL), BF16),
        ],
        compiler_params=_params(("parallel", "parallel")),
        name="proj_lat",
    )(x, g1n, sc1, sh1, w_in_p, qag, kvag, wq_p, wk_p, wv_p, qg_p, kg_p, qadd, kadd, cos_t, sin_t)


def _proj_ctx_call(ctx, g1n, csc1, csh1, w_in_p, kvag, wk_p, wv_p, kg_p, kadd):
    B, TC, _ = ctx.shape
    H = MLA_HEADS
    shared = pl.BlockSpec((1, 1, D_MODEL), lambda b: (0, 0, 0))
    return pl.pallas_call(
        functools.partial(_proj_body, latent=False),
        grid=(B,),
        in_specs=[pl.BlockSpec((1, TC, D_MODEL), lambda b: (b, 0, 0)), _full(g1n.shape), shared, shared,
                  _full(w_in_p.shape), _full(kvag.shape), _full(wk_p.shape), _full(wv_p.shape),
                  _full(kg_p.shape), _full(kadd.shape)],
        out_specs=[
            pl.BlockSpec((1, H, TC, HEAD_SLOT), lambda b: (b, 0, 0, 0)),
            pl.BlockSpec((1, H // 2, LANES, TC), lambda b: (b, 0, 0, 0)),
        ],
        out_shape=[
            jax.ShapeDtypeStruct((B, H, TC, HEAD_SLOT), BF16),
            jax.ShapeDtypeStruct((B, H // 2, LANES, TC), BF16),
        ],
        compiler_params=_params(("parallel",)),
        name="proj_ctx",
    )(ctx, g1n, csc1, csh1, w_in_p, kvag, wk_p, wv_p, kg_p, kadd)


def _attn_body(q_ref, kl_ref, kc_ref, vl_ref, vc_ref, o_ref):
    tq = q_ref.shape[2]
    T = kl_ref.shape[2]
    nck = T // KEY_CHUNK
    row = lax.broadcasted_iota(jnp.int32, (LANES, tq), 0)

    def probs(hd):
        q = q_ref[0, hd]
        den = jnp.zeros((1, tq), F32)
        ps = []
        for c in range(nck + 1):
            k = kc_ref[0, hd] if c == nck else kl_ref[0, hd, c * KEY_CHUNK:(c + 1) * KEY_CHUNK, :]
            e = jnp.exp2(_dot_nt(k, q))
            den = den + jnp.sum(e, axis=0, keepdims=True)
            ps.append(e.astype(BF16))
        return ps, den

    def weighted_values(hd, ps):
        hp = hd // 2
        o = _dot(vc_ref[0, hp], ps[nck])
        for c in range(nck):
            o = o + _dot(vl_ref[0, hp, :, c * KEY_CHUNK:(c + 1) * KEY_CHUNK], ps[c])
        return o

    outs, dens, prev = [], [], None
    for hd in range(MLA_HEADS + 1):
        cur = probs(hd) if hd < MLA_HEADS else None
        if prev is not None:
            outs.append(weighted_values(hd - 1, prev[0]) / prev[1])
            dens.append(prev[1])
        prev = cur
    for hp in range(MLA_HEADS // 2):
        o_ref[0, hp] = jnp.where(row < V_DIM, outs[2 * hp], outs[2 * hp + 1]).astype(BF16)
    min_den = functools.reduce(jnp.minimum, dens)

    @pl.when(jnp.min(min_den) < DEN_FLOOR)
    def _():
        def pair(hp, carry):
            res = []
            for hd in (2 * hp, 2 * hp + 1):
                q = q_ref[0, hd]
                sl = _dot_nt(kl_ref[0, hd], q)
                sc = _dot_nt(kc_ref[0, hd], q)
                m = jnp.maximum(jnp.max(sl, axis=0, keepdims=True), jnp.max(sc, axis=0, keepdims=True))
                el = jnp.exp2(sl - m)
                ec = jnp.exp2(sc - m)
                den = jnp.sum(el, axis=0, keepdims=True) + jnp.sum(ec, axis=0, keepdims=True)
                o = _dot(vl_ref[0, hp], el.astype(BF16)) + _dot(vc_ref[0, hp], ec.astype(BF16))
                res.append(o / den)
            o_ref[0, hp] = jnp.where(row < V_DIM, res[0], res[1]).astype(BF16)
            return carry

        lax.fori_loop(0, MLA_HEADS // 2, pair, 0)


def _attn_call(q, k_lat, k_ctx, vt_lat, vt_ctx, tq):
    B, H, T, _ = q.shape
    TC = k_ctx.shape[2]
    return pl.pallas_call(
        _attn_body,
        grid=(B, T // tq),
        in_specs=[
            pl.BlockSpec((1, H, tq, HEAD_SLOT), lambda b, j: (b, 0, j, 0)),
            pl.BlockSpec((1, H, T, HEAD_SLOT), lambda b, j: (b, 0, 0, 0)),
            pl.BlockSpec((1, H, TC, HEAD_SLOT), lambda b, j: (b, 0, 0, 0)),
            pl.BlockSpec((1, H // 2, LANES, T), lambda b, j: (b, 0, 0, 0)),
            pl.BlockSpec((1, H // 2, LANES, TC), lambda b, j: (b, 0, 0, 0)),
        ],
        out_specs=pl.BlockSpec((1, H // 2, LANES, tq), lambda b, j: (b, 0, 0, j)),
        out_shape=jax.ShapeDtypeStruct((B, H // 2, LANES, T), BF16),
        compiler_params=_params(("parallel", "parallel")),
        name="attn",
    )(q, k_lat, k_ctx, vt_lat, vt_ctx)


def _cadd(a, b):
    return a[0] + b[0], a[1] + b[1]


def _csub(a, b):
    return a[0] - b[0], a[1] - b[1]


def _cmul_neg_i(a):
    return a[1], -a[0]


def _cmul_pos_i(a):
    return -a[1], a[0]


def _fft8(u):
    r = 0.7071067811865476
    a0, a1 = _cadd(u[0], u[4]), _csub(u[0], u[4])
    a2, a3 = _cadd(u[2], u[6]), _csub(u[2], u[6])
    a4, a5 = _cadd(u[1], u[5]), _csub(u[1], u[5])
    a6, a7 = _cadd(u[3], u[7]), _csub(u[3], u[7])
    e0, e2 = _cadd(a0, a2), _csub(a0, a2)
    e1, e3 = _cadd(a1, _cmul_neg_i(a3)), _cadd(a1, _cmul_pos_i(a3))
    o0, o2 = _cadd(a4, a6), _csub(a4, a6)
    o1, o3 = _cadd(a5, _cmul_neg_i(a7)), _cadd(a5, _cmul_pos_i(a7))
    wo1 = (r * (o1[0] + o1[1]), r * (o1[1] - o1[0]))
    wo3 = (r * (o3[1] - o3[0]), -r * (o3[0] + o3[1]))
    return [_cadd(e0, o0), _cadd(e1, wo1), _cadd(e2, _cmul_neg_i(o2)), _cadd(e3, wo3),
            _csub(e0, o0), _csub(e1, wo1), _cadd(e2, _cmul_pos_i(o2)), _csub(e3, wo3)]


def _fourier_body(f_ref, cc_ref, twc_ref, tws_ref, m2_ref, o_ref, b_ref, y_ref):
    for g in range(F_GROUPS):
        cols = slice(g * F_GROUP_DIM, (g + 1) * F_GROUP_DIM)
        y = _dot(f_ref[0, :, cols], cc_ref[...])
        u = [(y[j * FFT_N:(j + 1) * FFT_N, :F_GROUP_DIM], y[j * FFT_N:(j + 1) * FFT_N, F_GROUP_DIM:])
             for j in range(FFT_R)]
        a = _fft8(u)
        for k1 in range(FFT_R):
            ar, ai = a[k1]
            if k1 > 0:
                c = twc_ref[k1 * FFT_N:(k1 + 1) * FFT_N, :]
                s = tws_ref[k1 * FFT_N:(k1 + 1) * FFT_N, :]
                ar, ai = ar * c + ai * s, ai * c - ar * s
            b_ref[k1, 0:FFT_N, cols] = ar.astype(BF16)
            b_ref[k1, FFT_N:2 * FFT_N, cols] = ai.astype(BF16)
    for k1 in range(FFT_R):
        y = _dot(m2_ref[...], b_ref[k1])
        for g in range(F_GROUPS):
            y_ref[g, pl.ds(k1, FFT_N, stride=FFT_R), :] = y[:, g * F_GROUP_DIM:(g + 1) * F_GROUP_DIM]
    for g in range(F_GROUPS):
        o_ref[0, :, g * F_GROUP_DIM:(g + 1) * F_GROUP_DIM] = y_ref[g].astype(BF16)


def _fourier_call(f, cc, twc, tws, m2):
    B, T, _ = f.shape
    return pl.pallas_call(
        _fourier_body,
        grid=(B,),
        in_specs=[pl.BlockSpec((1, T, D_F), lambda b: (b, 0, 0)), _full(cc.shape), _full(twc.shape),
                  _full(tws.shape), _full(m2.shape)],
        out_specs=pl.BlockSpec((1, T, D_F), lambda b: (b, 0, 0)),
        out_shape=jax.ShapeDtypeStruct((B, T, D_F), BF16),
        scratch_shapes=[pltpu.VMEM((FFT_R, 2 * FFT_N, D_F), BF16), pltpu.VMEM((F_GROUPS, T, F_GROUP_DIM), F32)],
        compiler_params=_params(("parallel",)),
        name="fourier",
    )(f, cc, twc, tws, m2)


def _merge_body(at_ref, fo_ref, ga_ref, gb_ref, x_ref, g1_ref, g2n_ref, sc2_ref, sh2_ref,
                wo_ref, wf_ref, wout_ref, wrh_ref, wrl_ref, x1_out, h2_out, aff_out):
    tm = x_ref.shape[1]
    attn_t = jnp.concatenate([at_ref[0, hp] for hp in range(MLA_HEADS // 2)], axis=0)
    a = lax.dot_general(attn_t, wo_ref[...], (((0,), (0,)), ((), ())), preferred_element_type=F32)
    fo = _dot(fo_ref[0], wf_ref[...])
    mix = ga_ref[0].astype(F32) * a + gb_ref[0].astype(F32) * fo
    y = _dot(mix.astype(BF16), wout_ref[...])
    x1 = x_ref[0] + g1_ref[0] * y
    x1_out[0] = x1
    h2 = _rms_rows(x1, D_MODEL) * g2n_ref[...]
    h2 = h2 * (1.0 + sc2_ref[0]) + sh2_ref[0]
    for c in range(D_MODEL // LANES):
        h2_out[0, pl.ds(c, tm, stride=D_MODEL // LANES), :] = h2[:, c * LANES:(c + 1) * LANES]
    hh, hl = _split2(h2)
    p_hi = _dot(hh, wrl_ref[...])
    logits = p_hi + pltpu.roll(p_hi, LANES - N_EXPERTS, 1) + _dot(hl, wrh_ref[...])
    lane = lax.broadcasted_iota(jnp.int32, logits.shape, 1)
    logits = jnp.where(lane < N_EXPERTS, logits, -1e30)
    ex = jnp.exp(logits - jnp.max(logits, axis=-1, keepdims=True))
    aff_out[0] = ex / jnp.sum(ex, axis=-1, keepdims=True)


def _merge_call(attn, four, ga, gb, x, g1, g2n, sc2, sh2, wo, wf, wout, wrh, wrl, tm):
    B, T, _ = x.shape
    H = MLA_HEADS
    tok = lambda w: pl.BlockSpec((1, tm, w), lambda b, j: (b, j, 0))
    per_b = pl.BlockSpec((1, 1, D_MODEL), lambda b, j: (b, 0, 0))
    nchunk = D_MODEL // LANES
    return pl.pallas_call(
        _merge_body,
        grid=(B, T // tm),
        in_specs=[
            pl.BlockSpec((1, H // 2, LANES, tm), lambda b, j: (b, 0, 0, j)),
            tok(D_F), tok(D_MODEL), tok(D_MODEL), tok(D_MODEL), per_b, _full(g2n.shape), per_b, per_b,
            _full(wo.shape), _full(wf.shape), _full(wout.shape), _full(wrh.shape), _full(wrl.shape),
        ],
        out_specs=[
            tok(D_MODEL),
            pl.BlockSpec((1, tm * nchunk, LANES), lambda b, j: (b, j, 0)),
            tok(LANES),
        ],
        out_shape=[
            jax.ShapeDtypeStruct((B, T, D_MODEL), F32),
            jax.ShapeDtypeStruct((B, T * nchunk, LANES), F32),
            jax.ShapeDtypeStruct((B, T, LANES), F32),
        ],
        compiler_params=_params(("parallel", "parallel")),
        name="merge",
    )(attn, four, ga, gb, x, g1, g2n, sc2, sh2, wo, wf, wout, wrh, wrl)


def _prefix_excl(m, tri):
    outs = []
    run = jnp.zeros((m.shape[0], 1), F32)
    for blk in range(m.shape[1] // LANES):
        mb = m[:, blk * LANES:(blk + 1) * LANES]
        inc = _dot(mb.astype(BF16), tri)
        outs.append(inc - mb + run)
        run = run + inc[:, LANES - 1:LANES]
    return jnp.concatenate(outs, axis=1)


def _select_body(aff_ref, tri_ref, pos_ref, idx_out, gate_out, *, cap):
    T = aff_ref.shape[1]
    aff = aff_ref[0].T[0:N_EXPERTS]

    def search(carry):
        base, step, rounds, _ = carry
        thr = base
        for j in range(1, 2 ** SEARCH_BITS):
            cand = base + float(j) * step
            cnt = jnp.sum((aff >= cand).astype(F32), axis=-1, keepdims=True)
            thr = jnp.where(cnt >= float(cap), cand, thr)
        settled = jnp.where((thr > 0.0) & (step < thr * SEARCH_DONE), 1.0, 0.0)
        return thr, step * (0.5 ** SEARCH_BITS), rounds + 1, (jnp.min(settled) < 1.0).astype(jnp.int32)

    init = (jnp.zeros((N_EXPERTS, 1), F32), jnp.full((N_EXPERTS, 1), 2.0 * 0.5 ** SEARCH_BITS, F32),
            jnp.int32(0), jnp.int32(1))
    thr = lax.while_loop(lambda c: (c[3] > 0) & (c[2] < SEARCH_MAX_ROUNDS), search, init)[0]
    gt = (aff > thr).astype(F32)
    eq = (aff == thr).astype(F32)
    need = cap - jnp.sum(gt, axis=-1, keepdims=True)
    tri = tri_ref[...]
    sel = gt + eq * (_prefix_excl(eq, tri) < need).astype(F32)
    slot = jnp.where(sel > 0.0, _prefix_excl(sel, tri), -1.0)

    pos = pos_ref[...]
    slot_rows = (pl.program_id(0) % 2) * (T * (D_MODEL // LANES))
    srow = lax.broadcasted_iota(jnp.int32, (cap, T), 0).astype(F32)
    zero = jnp.zeros((3, T), BF16)
    for e in range(N_EXPERTS):
        onehot = jnp.where(srow == slot[e:e + 1, :], 1.0, 0.0).astype(BF16)
        ah, am, al = _split3(aff[e:e + 1, :])
        vals = jnp.concatenate([pos, ah, am, al, zero], axis=0)
        res = _dot_nt(vals, onehot)
        idx_out[0, e:e + 1, :] = (res[0:1] * float(LANES) + res[1:2]).astype(jnp.int32) + slot_rows
        gate_out[0, e:e + 1, :] = res[2:3] + (res[3:4] + res[4:5])


def _select_call(aff, tri, pos, cap):
    B, T, _ = aff.shape
    return pl.pallas_call(
        functools.partial(_select_body, cap=cap),
        grid=(B,),
        in_specs=[pl.BlockSpec((1, T, LANES), lambda b: (b, 0, 0)), _full(tri.shape), _full(pos.shape)],
        out_specs=[pl.BlockSpec((1, N_EXPERTS, cap), lambda b: (b, 0, 0)),
                   pl.BlockSpec((1, N_EXPERTS, cap), lambda b: (b, 0, 0))],
        out_shape=[jax.ShapeDtypeStruct((B, N_EXPERTS, cap), jnp.int32),
                   jax.ShapeDtypeStruct((B, N_EXPERTS, cap), F32)],
        compiler_params=_params(("parallel",)),
        name="select",
    )(aff, tri, pos)


def _moe_body(ip_ref, i0_ref, i1_ref, in_ref, g0_ref, g1_ref, h2_hbm, wg_ref, wu_ref, wd_ref, out_hbm,
              h2_ref, acc_ref, xg_a, xg_b, ye_a, ye_b, h2_sem, acc_sem, *, cap, stride, n_samples):
    b = pl.program_id(0)
    k = pl.program_id(1)
    nchunk = D_MODEL // LANES
    rows_per_sample = h2_hbm.shape[1]
    slot = b % 2
    group = 8

    def slot_rows(s):
        return pl.ds(pl.multiple_of(s * rows_per_sample, rows_per_sample), rows_per_sample)

    def h2_copy(sample, s):
        return pltpu.make_async_copy(h2_hbm.at[sample], h2_ref.at[slot_rows(s)], h2_sem.at[s])

    def acc_copy(sample, s):
        return pltpu.make_async_copy(acc_ref.at[slot_rows(s)], out_hbm.at[sample], acc_sem.at[s])

    def zero_of(tile):
        bits = pltpu.bitcast(tile, jnp.uint32)
        return pltpu.bitcast(lax.shift_right_logical(bits, jnp.uint32(32)), F32)

    def gather(i_ref, xg_ref):
        def rows(s0, hold):
            for s in range(s0, s0 + group):
                base = pl.multiple_of(i_ref[0, 0, 0, s], SUBLANES)
                tile = h2_ref[pl.ds(base, nchunk), :] + hold
                xg_ref[pl.ds(s, nchunk, stride=stride), :] = tile
            return tile
        return [functools.partial(rows, s0) for s0 in range(0, cap, group)]

    def scatter(i_ref, ye_ref):
        def rows(s0, hold):
            bases, news = [], []
            for s in range(s0, s0 + group):
                base = pl.multiple_of(i_ref[0, 0, 0, s], SUBLANES)
                bases.append(base)
                news.append(acc_ref[pl.ds(base, nchunk), :]
                            + (ye_ref[pl.ds(s, nchunk, stride=stride), :] + hold))
            for base, new in zip(bases, news):
                acc_ref[pl.ds(base, nchunk), :] = new
            return news[-1]
        return [functools.partial(rows, s0) for s0 in range(0, cap, group)]

    def expert(j, g_ref, xg_ref, ye_ref):
        state = {}
        n_mid = D_EXPERT // MXU_TILE
        n_out = D_MODEL // MXU_TILE

        def load(zero):
            xb = jnp.concatenate([xg_ref[c * stride:c * stride + cap, :] for c in range(nchunk)], axis=1)
            state["x"] = xb.astype(BF16)
            state["gate"] = jnp.broadcast_to(g_ref[0, 0], (LANES, cap)).T[:, 0:1]
            state["h"] = []
            return xb[0:SUBLANES, 0:LANES]

        def mid(n, zero):
            cols = slice(n * MXU_TILE, (n + 1) * MXU_TILE)
            a = _dot(state["x"], wg_ref[j, :, cols]) + zero
            u = _dot(state["x"], wu_ref[j, :, cols])
            h = _silu(a) * u
            state["h"].append(h.astype(BF16))
            return h[0:SUBLANES, 0:LANES]

        def out(n, zero):
            cols = slice(n * MXU_TILE, (n + 1) * MXU_TILE)
            ye = _dot(state["h"][0], wd_ref[j, 0:MXU_TILE, cols])
            for m in range(1, n_mid):
                ye = ye + _dot(state["h"][m], wd_ref[j, m * MXU_TILE:(m + 1) * MXU_TILE, cols])
            ye = ye * state["gate"] + zero
            for c in range(MXU_TILE // LANES):
                cc = n * (MXU_TILE // LANES) + c
                ye_ref[cc * stride:cc * stride + cap, :] = ye[:, c * LANES:(c + 1) * LANES]
            return ye[0:SUBLANES, 0:LANES]

        return ([load] + [functools.partial(mid, n) for n in range(n_mid)]
                + [functools.partial(out, n) for n in range(n_out)])

    def paced(main, side):
        none = jnp.zeros((SUBLANES, LANES), F32)
        hold = none
        edges = [none, none, none]
        done = 0
        for i, piece in enumerate(main):
            zero = jnp.concatenate([edges[0][0:1, :]] * (MXU_TILE // LANES), axis=1)
            witness = piece(zero)
            upto = (len(side) * (i + 1)) // len(main)
            tiles = [s(hold) for s in side[done:upto]]
            done = upto
            hold = zero_of(witness)
            edges = edges[1:] + [zero_of(functools.reduce(jnp.add, tiles)) if tiles else none]

    def zipped(a, b):
        return [f for pair in zip(a, b) for f in pair]

    @pl.when(k == 0)
    def _():
        @pl.when(b == 0)
        def _():
            h2_copy(0, 0).start()

        h2_copy(b, slot).wait()

        @pl.when(b + 1 < n_samples)
        def _():
            h2_copy(b + 1, 1 - slot).start()

        @pl.when(b >= 2)
        def _():
            acc_copy(b - 2, slot).wait()

        acc_ref[slot_rows(slot), :] = jnp.zeros((rows_per_sample, LANES), F32)
        ye_b[...] = jnp.zeros_like(ye_b)
        for piece in gather(i0_ref, xg_a):
            piece(jnp.zeros((SUBLANES, LANES), F32))

    paced(expert(0, g0_ref, xg_a, ye_a), zipped(scatter(ip_ref, ye_b), gather(i1_ref, xg_b)))
    paced(expert(1, g1_ref, xg_b, ye_b), zipped(scatter(i0_ref, ye_a), gather(in_ref, xg_a)))

    @pl.when(k == pl.num_programs(1) - 1)
    def _():
        for piece in scatter(i1_ref, ye_b):
            piece(jnp.zeros((SUBLANES, LANES), F32))
        acc_copy(b, slot).start()

        @pl.when(b == n_samples - 1)
        def _():
            acc_copy(b, slot).wait()
            if n_samples >= 2:
                acc_copy(b - 1, 1 - slot).wait()


def _moe_call(idx, gate, h2t, wg, wu, wd, T):
    B, E, cap = idx.shape
    nchunk = D_MODEL // LANES
    stride = cap + SUBLANES
    buf = pltpu.VMEM((nchunk * stride, LANES), F32)
    idx4 = idx.reshape(B, E, 1, cap)
    gate4 = gate.reshape(B, E, 1, cap)

    def rows(expert_of_step, memory_space=None):
        spec = pl.BlockSpec((1, 1, 1, cap), lambda b, k: (b, expert_of_step(k), 0, 0))
        return spec if memory_space is None else pl.BlockSpec(
            (1, 1, 1, cap), lambda b, k: (b, expert_of_step(k), 0, 0), memory_space=memory_space)

    two_samples = pltpu.VMEM((2 * T * nchunk, LANES), F32)
    return pl.pallas_call(
        functools.partial(_moe_body, cap=cap, stride=stride, n_samples=B),
        grid=(B, E // 2),
        in_specs=[
            rows(lambda k: jnp.maximum(2 * k - 1, 0), pltpu.SMEM),
            rows(lambda k: 2 * k, pltpu.SMEM),
            rows(lambda k: 2 * k + 1, pltpu.SMEM),
            rows(lambda k: jnp.minimum(2 * k + 2, E - 1), pltpu.SMEM),
            rows(lambda k: 2 * k),
            rows(lambda k: 2 * k + 1),
            pl.BlockSpec(memory_space=pl.ANY),
            pl.BlockSpec((2, D_MODEL, D_EXPERT), lambda b, k: (k, 0, 0)),
            pl.BlockSpec((2, D_MODEL, D_EXPERT), lambda b, k: (k, 0, 0)),
            pl.BlockSpec((2, D_EXPERT, D_MODEL), lambda b, k: (k, 0, 0)),
        ],
        out_specs=pl.BlockSpec(memory_space=pl.ANY),
        out_shape=jax.ShapeDtypeStruct((B, T * nchunk, LANES), F32),
        scratch_shapes=[two_samples, two_samples, buf, buf, buf, buf,
                        pltpu.SemaphoreType.DMA((2,)), pltpu.SemaphoreType.DMA((2,))],
        compiler_params=_params(("arbitrary", "arbitrary")),
        name="moe",
    )(idx4, idx4, idx4, idx4, gate4, gate4, h2t, wg, wu, wd)


def _final_body(x1_ref, y_ref, g2_ref, o_ref):
    tm = x1_ref.shape[1]
    for c in range(D_MODEL // LANES):
        cols = slice(c * LANES, (c + 1) * LANES)
        y = y_ref[0, pl.ds(c, tm, stride=D_MODEL // LANES), :]
        o_ref[0, :, cols] = x1_ref[0, :, cols] + g2_ref[0, :, cols] * y


def _final_call(x1, y, g2, tm):
    B, T, _ = x1.shape
    nchunk = D_MODEL // LANES
    tok = pl.BlockSpec((1, tm, D_MODEL), lambda b, j: (b, j, 0))
    return pl.pallas_call(
        _final_body,
        grid=(B, T // tm),
        in_specs=[tok,
                  pl.BlockSpec((1, tm * nchunk, LANES), lambda b, j: (b, j, 0)),
                  pl.BlockSpec((1, 1, D_MODEL), lambda b, j: (b, 0, 0))],
        out_specs=tok,
        out_shape=jax.ShapeDtypeStruct((B, T, D_MODEL), F32),
        compiler_params=_params(("parallel", "parallel")),
        name="final",
    )(x1, y, g2)


def _slot_layout():
    half = QK_ROPE // 2
    n_freq = half // 2
    src = np.full((HEAD_SLOT,), -1, np.int64)
    for axis in range(2):
        first = QK_NOPE + axis * half
        src[axis * n_freq:(axis + 1) * n_freq] = np.arange(first, first + n_freq)
        src[LANES // 2 + axis * n_freq:LANES // 2 + (axis + 1) * n_freq] = np.arange(first + n_freq, first + half)
    src[half:half + QK_NOPE // 2] = np.arange(0, QK_NOPE // 2)
    src[LANES // 2 + half:LANES // 2 + half + QK_NOPE // 2] = np.arange(QK_NOPE // 2, QK_NOPE)
    assert src[SHIFT_LANE] == -1 and sorted(src[src >= 0]) == list(range(QK_DIM))
    return src


_SLOT_SRC = _slot_layout()


def _rope_tables(T):
    half = QK_ROPE // 2
    n_freq = half // 2
    inv_freq = 1.0 / (ROPE_THETA ** (np.arange(n_freq, dtype=np.float32) / n_freq))
    t = np.arange(T)
    cos = np.ones((T, HEAD_SLOT), np.float32)
    sin = np.zeros((T, HEAD_SLOT), np.float32)
    for axis, pos in enumerate((t // GRID_W, t % GRID_W)):
        ang = pos.astype(np.float32)[:, None] * inv_freq[None, :].astype(np.float32)
        c, s = np.cos(ang).astype(np.float32), np.sin(ang).astype(np.float32)
        lo1 = axis * n_freq
        lo2 = LANES // 2 + axis * n_freq
        cos[:, lo1:lo1 + n_freq] = c
        cos[:, lo2:lo2 + n_freq] = c
        sin[:, lo1:lo1 + n_freq] = -s
        sin[:, lo2:lo2 + n_freq] = s
    return jnp.asarray(cos), jnp.asarray(sin)


def _dft_tables(T):
    assert T == FFT_R * FFT_N
    n = np.arange(F_GROUP_DIM)
    ang_c = 2.0 * np.pi * ((n[:, None] * n[None, :]) % F_GROUP_DIM) / F_GROUP_DIM
    cc = np.concatenate([np.cos(ang_c), -np.sin(ang_c)], axis=1) / np.sqrt(F_GROUP_DIM)
    k1 = np.repeat(np.arange(FFT_R), FFT_N)
    n2 = np.tile(np.arange(FFT_N), FFT_R)
    ang_w = 2.0 * np.pi * (k1 * n2) / T
    twc = np.broadcast_to(np.cos(ang_w)[:, None], (T, F_GROUP_DIM))
    tws = np.broadcast_to(np.sin(ang_w)[:, None], (T, F_GROUP_DIM))
    m = np.arange(FFT_N)
    ang_2 = 2.0 * np.pi * ((m[:, None] * m[None, :]) % FFT_N) / FFT_N
    m2 = np.concatenate([np.cos(ang_2), np.sin(ang_2)], axis=1) / np.sqrt(T)
    return (jnp.asarray(cc, F32).astype(BF16), jnp.asarray(twc, F32), jnp.asarray(tws, F32),
            jnp.asarray(m2, F32).astype(BF16))


def kernel(x, c, ctx, c_ctx, w_mod, b_mod, norm1_g, w_in, q_a_norm_g, kv_a_norm_g, w_q_up, w_kv_up,
           q_norm_g, k_norm_g, w_o_attn, w_fourier, w_out, norm2_g, w_router, w_e_gate, w_e_up, w_e_down):
    B, T, D = x.shape
    assert w_mod.shape[0] == 1 and D == D_MODEL and T % 1024 == 0
    H = MLA_HEADS
    cap = (CAPACITY_FACTOR * T) // N_EXPERTS

    rows = -(-(B + 1) // SUBLANES) * SUBLANES
    cc_in = jnp.concatenate([c, c_ctx[None, :], jnp.zeros((rows - B - 1, D), F32)], axis=0)
    mod = _mod_call(cc_in, w_mod[0], b_mod)
    sh1, sc1, g1, sh2, sc2, g2 = [mod[:B, i * D:(i + 1) * D].reshape(B, 1, D) for i in range(6)]
    csh1 = mod[B:B + 1, 0:D].reshape(1, 1, D)
    csc1 = mod[B:B + 1, D:2 * D].reshape(1, 1, D)

    src = _SLOT_SRC
    used = jnp.asarray(src >= 0, F32)
    nope = jnp.asarray((src >= 0) & (src < QK_NOPE), F32)
    rope = jnp.asarray(src >= QK_NOPE, F32)
    lane_src = np.maximum(src, 0)
    wi = w_in[0]
    pe_cols = jnp.take(wi[:, OFF_KPE:OFF_F], np.maximum(src - QK_NOPE, 0), axis=1) * rope
    w_in_p = jnp.concatenate([wi[:, OFF_Q:OFF_KV], pe_cols, wi[:, OFF_KV:OFF_KPE], wi[:, OFF_F:N_IN]], axis=1)
    w_in_p = w_in_p.astype(BF16)
    wq_p = jnp.take(w_q_up[0].reshape(Q_LORA, H, QK_DIM), lane_src, axis=2) * used
    wq_p = wq_p.reshape(Q_LORA, H * HEAD_SLOT).astype(BF16)
    wkv = w_kv_up[0].reshape(KV_LORA, H, QK_NOPE + V_DIM)
    wk_p = jnp.take(wkv[:, :, :QK_NOPE], np.minimum(lane_src, QK_NOPE - 1), axis=2) * nope
    wk_p = wk_p.reshape(KV_LORA, H * HEAD_SLOT).astype(BF16)
    wv_p = wkv[:, :, QK_NOPE:].reshape(KV_LORA, H * V_DIM).astype(BF16)
    qg_p = (jnp.take(q_norm_g[0], lane_src) * used * (QK_DIM ** -0.5 * LOG2E)).reshape(1, HEAD_SLOT)
    kg_p = (jnp.take(k_norm_g[0], lane_src) * used).reshape(1, HEAD_SLOT)
    bound = (QK_DIM * BOUND_SLACK) * jnp.max(jnp.abs(qg_p)) * jnp.max(jnp.abs(kg_p))
    on_shift = jnp.asarray(np.arange(HEAD_SLOT) == SHIFT_LANE, F32).reshape(1, HEAD_SLOT)
    qadd = on_shift * (-bound)
    kadd = on_shift
    qag = q_a_norm_g[0].reshape(1, Q_LORA)
    kvag = kv_a_norm_g[0].reshape(1, KV_LORA)
    g1n = norm1_g[0].reshape(1, D)
    g2n = norm2_g[0].reshape(1, D)
    cos_t, sin_t = _rope_tables(T)

    q, k_lat, vt_lat, f, ga, gb = _proj_lat_call(x, g1n, sc1, sh1, w_in_p, qag, kvag, wq_p, wk_p, wv_p,
                                                 qg_p, kg_p, qadd, kadd, cos_t, sin_t, tm=512)
    k_ctx, vt_ctx = _proj_ctx_call(ctx, g1n, csc1, csh1, w_in_p, kvag, wk_p, wv_p, kg_p, kadd)
    attn = _attn_call(q, k_lat, k_ctx, vt_lat, vt_ctx, tq=512)
    four = _fourier_call(f, *_dft_tables(T))

    wr = jnp.pad(w_router[0], ((0, 0), (0, LANES - N_EXPERTS)))
    wrh = wr.astype(BF16)
    wrl = wrh + jnp.roll((wr - wrh.astype(F32)).astype(BF16), N_EXPERTS, axis=1)
    x1, h2t, aff = _merge_call(attn, four, ga, gb, x, g1, g2n, sc2, sh2, w_o_attn[0].astype(BF16),
                               w_fourier[0].astype(BF16), w_out[0].astype(BF16), wrh, wrl, tm=512)

    tri = jnp.asarray(np.triu(np.ones((LANES, LANES), np.float32)), BF16)
    addr = np.arange(T) * (D // LANES)
    pos = jnp.asarray(np.stack([addr // LANES, addr % LANES]).astype(np.float32), BF16)
    idx, gate = _select_call(aff, tri, pos, cap)
    y = _moe_call(idx, gate, h2t, w_e_gate[0].astype(BF16), w_e_up[0].astype(BF16),
                  w_e_down[0].astype(BF16), T)
    return _final_call(x1, y, g2, tm=1024)
```

```python
import functools

import numpy as np
import jax
import jax.numpy as jnp
from jax import lax
from jax.experimental import pallas as pl
from jax.experimental.pallas import tpu as pltpu

F32 = jnp.float32
BF16 = jnp.bfloat16

D_MODEL = 1024
GRID_W = 64
MLA_HEADS = 8
QK_NOPE = 64
QK_ROPE = 32
QK_DIM = QK_NOPE + QK_ROPE
V_DIM = 64
Q_LORA = 384
KV_LORA = 256
ROPE_THETA = 10000.0
F_GROUPS = 4
F_GROUP_DIM = 128
D_F = F_GROUPS * F_GROUP_DIM
OFF_Q = 0
OFF_KV = OFF_Q + Q_LORA
OFF_KPE = OFF_KV + KV_LORA
OFF_F = OFF_KPE + QK_ROPE
OFF_GA = OFF_F + D_F
OFF_GB = OFF_GA + D_MODEL
N_IN = OFF_GB + D_MODEL
N_EXPERTS = 16
D_EXPERT = 512
CAPACITY_FACTOR = 2
EPS = 1e-6

LANES = 128
SUBLANES = 8
MXU_TILE = 256
HEAD_SLOT = LANES
SHIFT_LANE = 48
BOUND_SLACK = 1.02
DEN_FLOOR = 2.0 ** -60
LOG2E = 1.4426950408889634
FFT_R = 8
FFT_N = 256
SEARCH_BITS = 3
SEARCH_DONE = 2.0 ** -30
SEARCH_MAX_ROUNDS = 56
KEY_CHUNK = 256
VMEM_LIMIT = 56 * 1024 * 1024

PC_Q = 0
PC_PE = PC_Q + Q_LORA
PC_KV = PC_PE + HEAD_SLOT
PC_F = PC_KV + KV_LORA
PC_GA = PC_F + D_F
PC_GB = PC_GA + D_MODEL
PC_END = PC_GB + D_MODEL


def _dot(a, b):
    return jnp.dot(a, b, preferred_element_type=F32)


def _dot_nt(a, b):
    return lax.dot_general(a, b, (((1,), (1,)), ((), ())), preferred_element_type=F32)


def _split2(a):
    hi = a.astype(BF16)
    lo = (a - hi.astype(F32)).astype(BF16)
    return hi, lo


def _split3(a):
    hi = a.astype(BF16)
    r = a - hi.astype(F32)
    mid = r.astype(BF16)
    lo = (r - mid.astype(F32)).astype(BF16)
    return hi, mid, lo


def _dot3(a, b):
    ah, al = _split2(a)
    bh, bl = _split2(b)
    return _dot(ah, bh) + (_dot(ah, bl) + _dot(al, bh))


def _sigmoid(x):
    return 1.0 / (1.0 + jnp.exp(-x))


def _silu(x):
    return x * (0.5 * jnp.tanh(0.5 * x) + 0.5)


def _params(sem):
    return pltpu.CompilerParams(dimension_semantics=sem, vmem_limit_bytes=VMEM_LIMIT)


def _mod_body(c_ref, w_ref, b_ref, o_ref):
    c = c_ref[...]
    s = c * _sigmoid(c)
    o_ref[...] = _dot3(s, w_ref[...]) + b_ref[...]


def _mod_call(cc, w_mod, b_mod):
    rows = cc.shape[0]
    n = w_mod.shape[1]
    tn = 1024
    return pl.pallas_call(
        _mod_body,
        grid=(n // tn,),
        in_specs=[
            pl.BlockSpec((rows, D_MODEL), lambda j: (0, 0)),
            pl.BlockSpec((D_MODEL, tn), lambda j: (0, j)),
            pl.BlockSpec((1, tn), lambda j: (0, j)),
        ],
        out_specs=pl.BlockSpec((rows, tn), lambda j: (0, j)),
        out_shape=jax.ShapeDtypeStruct((rows, n), F32),
        compiler_params=_params(("arbitrary",)),
        name="mod",
    )(cc, w_mod, b_mod)


def _rms_rows(x, n):
    return x * lax.rsqrt(jnp.sum(x * x, axis=-1, keepdims=True) * (1.0 / n) + EPS)


def _rope(x, cos, sin):
    return x * cos + pltpu.roll(x, LANES // 2, 1) * sin


def _proj_body(*refs, latent):
    if latent:
        (x_ref, g_ref, sc_ref, sh_ref, win_ref, qag_ref, kvag_ref, wq_ref, wk_ref, wv_ref,
         qg_ref, kg_ref, qadd_ref, kadd_ref, cos_ref, sin_ref,
         q_out, k_out, v_out, f_out, ga_out, gb_out) = refs
    else:
        (x_ref, g_ref, sc_ref, sh_ref, win_ref, kvag_ref, wk_ref, wv_ref, kg_ref, kadd_ref,
         k_out, v_out) = refs
    x = x_ref[0]
    h = _rms_rows(x, D_MODEL) * (g_ref[...] * (1.0 + sc_ref[0])) + sh_ref[0]
    hb = h.astype(BF16)

    if latent:
        cos = cos_ref[...]
        sin = sin_ref[...]

    pe = _dot(hb, win_ref[:, PC_PE:PC_KV])

    ckv = _rms_rows(_dot(hb, win_ref[:, PC_KV:PC_F]), KV_LORA) * kvag_ref[...]
    ckvb = ckv.astype(BF16)
    kall = _dot(ckvb, wk_ref[...])
    kg = kg_ref[...]
    kadd = kadd_ref[...]
    for hd in range(MLA_HEADS):
        kh = kall[:, hd * HEAD_SLOT:(hd + 1) * HEAD_SLOT] + pe
        kh = _rms_rows(kh, QK_DIM) * kg
        if latent:
            kh = _rope(kh, cos, sin)
        k_out[0, hd] = (kh + kadd).astype(BF16)
    v = _dot(ckvb, wv_ref[...])
    for hp in range(MLA_HEADS // 2):
        v_out[0, hp] = v[:, hp * LANES:(hp + 1) * LANES].T.astype(BF16)

    if latent:
        cq = _rms_rows(_dot(hb, win_ref[:, PC_Q:PC_PE]), Q_LORA) * qag_ref[...]
        qall = _dot(cq.astype(BF16), wq_ref[...])
        qg = qg_ref[...]
        qadd = qadd_ref[...]
        for hd in range(MLA_HEADS):
            qh = qall[:, hd * HEAD_SLOT:(hd + 1) * HEAD_SLOT]
            q_out[0, hd] = (_rope(_rms_rows(qh, QK_DIM) * qg, cos, sin) + qadd).astype(BF16)
        f_out[0] = _dot(hb, win_ref[:, PC_F:PC_GA]).astype(BF16)
        ga_out[0] = _sigmoid(_dot(hb, win_ref[:, PC_GA:PC_GB])).astype(BF16)
        gb_out[0] = _sigmoid(_dot(hb, win_ref[:, PC_GB:PC_END])).astype(BF16)


def _full(shape):
    nd = len(shape)
    return pl.BlockSpec(shape, lambda *_: (0,) * nd)


def _proj_lat_call(x, g1n, sc1, sh1, w_in_p, qag, kvag, wq_p, wk_p, wv_p, qg_p, kg_p, qadd, kadd, cos_t, sin_t, tm):
    B, T, _ = x.shape
    H = MLA_HEADS
    tok = lambda w: pl.BlockSpec((1, tm, w), lambda b, j: (b, j, 0))
    per_b = pl.BlockSpec((1, 1, D_MODEL), lambda b, j: (b, 0, 0))
    tab = pl.BlockSpec((tm, HEAD_SLOT), lambda b, j: (j, 0))
    return pl.pallas_call(
        functools.partial(_proj_body, latent=True),
        grid=(B, T // tm),
        in_specs=[tok(D_MODEL), _full(g1n.shape), per_b, per_b, _full(w_in_p.shape), _full(qag.shape),
                  _full(kvag.shape), _full(wq_p.shape), _full(wk_p.shape), _full(wv_p.shape),
                  _full(qg_p.shape), _full(kg_p.shape), _full(qadd.shape), _full(kadd.shape), tab, tab],
        out_specs=[
            pl.BlockSpec((1, H, tm, HEAD_SLOT), lambda b, j: (b, 0, j, 0)),
            pl.BlockSpec((1, H, tm, HEAD_SLOT), lambda b, j: (b, 0, j, 0)),
            pl.BlockSpec((1, H // 2, LANES, tm), lambda b, j: (b, 0, 0, j)),
            tok(D_F), tok(D_MODEL), tok(D_MODEL),
        ],
        out_shape=[
            jax.ShapeDtypeStruct((B, H, T, HEAD_SLOT), BF16),
            jax.ShapeDtypeStruct((B, H, T, HEAD_SLOT), BF16),
            jax.ShapeDtypeStruct((B, H // 2, LANES, T), BF16),
            jax.ShapeDtypeStruct((B, T, D_F), BF16),
            jax.ShapeDtypeStruct((B, T, D_MODEL), BF16),
            jax.ShapeDtypeStruct((B, T, D_MODEL), BF16),
        ],
        compiler_params=_params(("parallel", "parallel")),
        name="proj_lat",
    )(x, g1n, sc1, sh1, w_in_p, qag, kvag, wq_p, wk_p, wv_p, qg_p, kg_p, qadd, kadd, cos_t, sin_t)


def _proj_ctx_call(ctx, g1n, csc1, csh1, w_in_p, kvag, wk_p, wv_p, kg_p, kadd):
    B, TC, _ = ctx.shape
    H = MLA_HEADS
    shared = pl.BlockSpec((1, 1, D_MODEL), lambda b: (0, 0, 0))
    return pl.pallas_call(
        functools.partial(_proj_body, latent=False),
        grid=(B,),
        in_specs=[pl.BlockSpec((1, TC, D_MODEL), lambda b: (b, 0, 0)), _full(g1n.shape), shared, shared,
                  _full(w_in_p.shape), _full(kvag.shape), _full(wk_p.shape), _full(wv_p.shape),
                  _full(kg_p.shape), _full(kadd.shape)],
        out_specs=[
            pl.BlockSpec((1, H, TC, HEAD_SLOT), lambda b: (b, 0, 0, 0)),
            pl.BlockSpec((1, H // 2, LANES, TC), lambda b: (b, 0, 0, 0)),
        ],
        out_shape=[
            jax.ShapeDtypeStruct((B, H, TC, HEAD_SLOT), BF16),
            jax.ShapeDtypeStruct((B, H // 2, LANES, TC), BF16),
        ],
        compiler_params=_params(("parallel",)),
        name="proj_ctx",
    )(ctx, g1n, csc1, csh1, w_in_p, kvag, wk_p, wv_p, kg_p, kadd)


def _attn_body(q_ref, kl_ref, kc_ref, vl_ref, vc_ref, o_ref):
    tq = q_ref.shape[2]
    T = kl_ref.shape[2]
    nck = T // KEY_CHUNK
    row = lax.broadcasted_iota(jnp.int32, (LANES, tq), 0)

    def probs(hd):
        q = q_ref[0, hd]
        den = jnp.zeros((1, tq), F32)
        ps = []
        for c in range(nck + 1):
            k = kc_ref[0, hd] if c == nck else kl_ref[0, hd, c * KEY_CHUNK:(c + 1) * KEY_CHUNK, :]
            e = jnp.exp2(_dot_nt(k, q))
            den = den + jnp.sum(e, axis=0, keepdims=True)
            ps.append(e.astype(BF16))
        return ps, den

    def weighted_values(hd, ps):
        hp = hd // 2
        o = _dot(vc_ref[0, hp], ps[nck])
        for c in range(nck):
            o = o + _dot(vl_ref[0, hp, :, c * KEY_CHUNK:(c + 1) * KEY_CHUNK], ps[c])
        return o

    outs, dens, prev = [], [], None
    for hd in range(MLA_HEADS + 1):
        cur = probs(hd) if hd < MLA_HEADS else None
        if prev is not None:
            outs.append(weighted_values(hd - 1, prev[0]) / prev[1])
            dens.append(prev[1])
        prev = cur
    for hp in range(MLA_HEADS // 2):
        o_ref[0, hp] = jnp.where(row < V_DIM, outs[2 * hp], outs[2 * hp + 1]).astype(BF16)
    min_den = functools.reduce(jnp.minimum, dens)

    @pl.when(jnp.min(min_den) < DEN_FLOOR)
    def _():
        def pair(hp, carry):
            res = []
            for hd in (2 * hp, 2 * hp + 1):
                q = q_ref[0, hd]
                sl = _dot_nt(kl_ref[0, hd], q)
                sc = _dot_nt(kc_ref[0, hd], q)
                m = jnp.maximum(jnp.max(sl, axis=0, keepdims=True), jnp.max(sc, axis=0, keepdims=True))
                el = jnp.exp2(sl - m)
                ec = jnp.exp2(sc - m)
                den = jnp.sum(el, axis=0, keepdims=True) + jnp.sum(ec, axis=0, keepdims=True)
                o = _dot(vl_ref[0, hp], el.astype(BF16)) + _dot(vc_ref[0, hp], ec.astype(BF16))
                res.append(o / den)
            o_ref[0, hp] = jnp.where(row < V_DIM, res[0], res[1]).astype(BF16)
            return carry

        lax.fori_loop(0, MLA_HEADS // 2, pair, 0)


def _attn_call(q, k_lat, k_ctx, vt_lat, vt_ctx, tq):
    B, H, T, _ = q.shape
    TC = k_ctx.shape[2]
    return pl.pallas_call(
        _attn_body,
        grid=(B, T // tq),
        in_specs=[
            pl.BlockSpec((1, H, tq, HEAD_SLOT), lambda b, j: (b, 0, j, 0)),
            pl.BlockSpec((1, H, T, HEAD_SLOT), lambda b, j: (b, 0, 0, 0)),
            pl.BlockSpec((1, H, TC, HEAD_SLOT), lambda b, j: (b, 0, 0, 0)),
            pl.BlockSpec((1, H // 2, LANES, T), lambda b, j: (b, 0, 0, 0)),
            pl.BlockSpec((1, H // 2, LANES, TC), lambda b, j: (b, 0, 0, 0)),
        ],
        out_specs=pl.BlockSpec((1, H // 2, LANES, tq), lambda b, j: (b, 0, 0, j)),
        out_shape=jax.ShapeDtypeStruct((B, H // 2, LANES, T), BF16),
        compiler_params=_params(("parallel", "parallel")),
        name="attn",
    )(q, k_lat, k_ctx, vt_lat, vt_ctx)


def _cadd(a, b):
    return a[0] + b[0], a[1] + b[1]


def _csub(a, b):
    return a[0] - b[0], a[1] - b[1]


def _cmul_neg_i(a):
    return a[1], -a[0]


def _cmul_pos_i(a):
    return -a[1], a[0]


def _fft8(u):
    r = 0.7071067811865476
    a0, a1 = _cadd(u[0], u[4]), _csub(u[0], u[4])
    a2, a3 = _cadd(u[2], u[6]), _csub(u[2], u[6])
    a4, a5 = _cadd(u[1], u[5]), _csub(u[1], u[5])
    a6, a7 = _cadd(u[3], u[7]), _csub(u[3], u[7])
    e0, e2 = _cadd(a0, a2), _csub(a0, a2)
    e1, e3 = _cadd(a1, _cmul_neg_i(a3)), _cadd(a1, _cmul_pos_i(a3))
    o0, o2 = _cadd(a4, a6), _csub(a4, a6)
    o1, o3 = _cadd(a5, _cmul_neg_i(a7)), _cadd(a5, _cmul_pos_i(a7))
    wo1 = (r * (o1[0] + o1[1]), r * (o1[1] - o1[0]))
    wo3 = (r * (o3[1] - o3[0]), -r * (o3[0] + o3[1]))
    return [_cadd(e0, o0), _cadd(e1, wo1), _cadd(e2, _cmul_neg_i(o2)), _cadd(e3, wo3),
            _csub(e0, o0), _csub(e1, wo1), _cadd(e2, _cmul_pos_i(o2)), _csub(e3, wo3)]


def _fourier_body(f_ref, cc_ref, twc_ref, tws_ref, m2_ref, o_ref, b_ref, y_ref):
    for g in range(F_GROUPS):
        cols = slice(g * F_GROUP_DIM, (g + 1) * F_GROUP_DIM)
        y = _dot(f_ref[0, :, cols], cc_ref[...])
        u = [(y[j * FFT_N:(j + 1) * FFT_N, :F_GROUP_DIM], y[j * FFT_N:(j + 1) * FFT_N, F_GROUP_DIM:])
             for j in range(FFT_R)]
        a = _fft8(u)
        for k1 in range(FFT_R):
            ar, ai = a[k1]
            if k1 > 0:
                c = twc_ref[k1 * FFT_N:(k1 + 1) * FFT_N, :]
                s = tws_ref[k1 * FFT_N:(k1 + 1) * FFT_N, :]
                ar, ai = ar * c + ai * s, ai * c - ar * s
            b_ref[k1, 0:FFT_N, cols] = ar.astype(BF16)
            b_ref[k1, FFT_N:2 * FFT_N, cols] = ai.astype(BF16)
    for k1 in range(FFT_R):
        y = _dot(m2_ref[...], b_ref[k1])
        for g in range(F_GROUPS):
            y_ref[g, pl.ds(k1, FFT_N, stride=FFT_R), :] = y[:, g * F_GROUP_DIM:(g + 1) * F_GROUP_DIM]
    for g in range(F_GROUPS):
        o_ref[0, :, g * F_GROUP_DIM:(g + 1) * F_GROUP_DIM] = y_ref[g].astype(BF16)


def _fourier_call(f, cc, twc, tws, m2):
    B, T, _ = f.shape
    return pl.pallas_call(
        _fourier_body,
        grid=(B,),
        in_specs=[pl.BlockSpec((1, T, D_F), lambda b: (b, 0, 0)), _full(cc.shape), _full(twc.shape),
                  _full(tws.shape), _full(m2.shape)],
        out_specs=pl.BlockSpec((1, T, D_F), lambda b: (b, 0, 0)),
        out_shape=jax.ShapeDtypeStruct((B, T, D_F), BF16),
        scratch_shapes=[pltpu.VMEM((FFT_R, 2 * FFT_N, D_F), BF16), pltpu.VMEM((F_GROUPS, T, F_GROUP_DIM), F32)],
        compiler_params=_params(("parallel",)),
        name="fourier",
    )(f, cc, twc, tws, m2)


def _merge_body(at_ref, fo_ref, ga_ref, gb_ref, x_ref, g1_ref, g2n_ref, sc2_ref, sh2_ref,
                wo_ref, wf_ref, wout_ref, wrh_ref, wrl_ref, x1_out, h2_out, aff_out):
    tm = x_ref.shape[1]
    attn_t = jnp.concatenate([at_ref[0, hp] for hp in range(MLA_HEADS // 2)], axis=0)
    a = lax.dot_general(attn_t, wo_ref[...], (((0,), (0,)), ((), ())), preferred_element_type=F32)
    fo = _dot(fo_ref[0], wf_ref[...])
    mix = ga_ref[0].astype(F32) * a + gb_ref[0].astype(F32) * fo
    y = _dot(mix.astype(BF16), wout_ref[...])
    x1 = x_ref[0] + g1_ref[0] * y
    x1_out[0] = x1
    h2 = _rms_rows(x1, D_MODEL) * g2n_ref[...]
    h2 = h2 * (1.0 + sc2_ref[0]) + sh2_ref[0]
    for c in range(D_MODEL // LANES):
        h2_out[0, pl.ds(c, tm, stride=D_MODEL // LANES), :] = h2[:, c * LANES:(c + 1) * LANES]
    hh, hl = _split2(h2)
    p_hi = _dot(hh, wrl_ref[...])
    logits = p_hi + pltpu.roll(p_hi, LANES - N_EXPERTS, 1) + _dot(hl, wrh_ref[...])
    lane = lax.broadcasted_iota(jnp.int32, logits.shape, 1)
    logits = jnp.where(lane < N_EXPERTS, logits, -1e30)
    ex = jnp.exp(logits - jnp.max(logits, axis=-1, keepdims=True))
    aff_out[0] = ex / jnp.sum(ex, axis=-1, keepdims=True)


def _merge_call(attn, four, ga, gb, x, g1, g2n, sc2, sh2, wo, wf, wout, wrh, wrl, tm):
    B, T, _ = x.shape
    H = MLA_HEADS
    tok = lambda w: pl.BlockSpec((1, tm, w), lambda b, j: (b, j, 0))
    per_b = pl.BlockSpec((1, 1, D_MODEL), lambda b, j: (b, 0, 0))
    nchunk = D_MODEL // LANES
    return pl.pallas_call(
        _merge_body,
        grid=(B, T // tm),
        in_specs=[
            pl.BlockSpec((1, H // 2, LANES, tm), lambda b, j: (b, 0, 0, j)),
            tok(D_F), tok(D_MODEL), tok(D_MODEL), tok(D_MODEL), per_b, _full(g2n.shape), per_b, per_b,
            _full(wo.shape), _full(wf.shape), _full(wout.shape), _full(wrh.shape), _full(wrl.shape),
        ],
        out_specs=[
            tok(D_MODEL),
            pl.BlockSpec((1, tm * nchunk, LANES), lambda b, j: (b, j, 0)),
            tok(LANES),
        ],
        out_shape=[
            jax.ShapeDtypeStruct((B, T, D_MODEL), F32),
            jax.ShapeDtypeStruct((B, T * nchunk, LANES), F32),
            jax.ShapeDtypeStruct((B, T, LANES), F32),
        ],
        compiler_params=_params(("parallel", "parallel")),
        name="merge",
    )(attn, four, ga, gb, x, g1, g2n, sc2, sh2, wo, wf, wout, wrh, wrl)


def _prefix_excl(m, tri):
    outs = []
    run = jnp.zeros((m.shape[0], 1), F32)
    for blk in range(m.shape[1] // LANES):
        mb = m[:, blk * LANES:(blk + 1) * LANES]
        inc = _dot(mb.astype(BF16), tri)
        outs.append(inc - mb + run)
        run = run + inc[:, LANES - 1:LANES]
    return jnp.concatenate(outs, axis=1)


def _select_body(aff_ref, tri_ref, pos_ref, idx_out, gate_out, *, cap):
    T = aff_ref.shape[1]
    aff = aff_ref[0].T[0:N_EXPERTS]

    def search(carry):
        base, step, rounds, _ = carry
        thr = base
        for j in range(1, 2 ** SEARCH_BITS):
            cand = base + float(j) * step
            cnt = jnp.sum((aff >= cand).astype(F32), axis=-1, keepdims=True)
            thr = jnp.where(cnt >= float(cap), cand, thr)
        settled = jnp.where((thr > 0.0) & (step < thr * SEARCH_DONE), 1.0, 0.0)
        return thr, step * (0.5 ** SEARCH_BITS), rounds + 1, (jnp.min(settled) < 1.0).astype(jnp.int32)

    init = (jnp.zeros((N_EXPERTS, 1), F32), jnp.full((N_EXPERTS, 1), 2.0 * 0.5 ** SEARCH_BITS, F32),
            jnp.int32(0), jnp.int32(1))
    thr = lax.while_loop(lambda c: (c[3] > 0) & (c[2] < SEARCH_MAX_ROUNDS), search, init)[0]
    gt = (aff > thr).astype(F32)
    eq = (aff == thr).astype(F32)
    need = cap - jnp.sum(gt, axis=-1, keepdims=True)
    tri = tri_ref[...]
    sel = gt + eq * (_prefix_excl(eq, tri) < need).astype(F32)
    slot = jnp.where(sel > 0.0, _prefix_excl(sel, tri), -1.0)

    pos = pos_ref[...]
    slot_rows = (pl.program_id(0) % 2) * (T * (D_MODEL // LANES))
    srow = lax.broadcasted_iota(jnp.int32, (cap, T), 0).astype(F32)
    zero = jnp.zeros((3, T), BF16)
    for e in range(N_EXPERTS):
        onehot = jnp.where(srow == slot[e:e + 1, :], 1.0, 0.0).astype(BF16)
        ah, am, al = _split3(aff[e:e + 1, :])
        vals = jnp.concatenate([pos, ah, am, al, zero], axis=0)
        res = _dot_nt(vals, onehot)
        idx_out[0, e:e + 1, :] = (res[0:1] * float(LANES) + res[1:2]).astype(jnp.int32) + slot_rows
        gate_out[0, e:e + 1, :] = res[2:3] + (res[3:4] + res[4:5])


def _select_call(aff, tri, pos, cap):
    B, T, _ = aff.shape
    return pl.pallas_call(
        functools.partial(_select_body, cap=cap),
        grid=(B,),
        in_specs=[pl.BlockSpec((1, T, LANES), lambda b: (b, 0, 0)), _full(tri.shape), _full(pos.shape)],
        out_specs=[pl.BlockSpec((1, N_EXPERTS, cap), lambda b: (b, 0, 0)),
                   pl.BlockSpec((1, N_EXPERTS, cap), lambda b: (b, 0, 0))],
        out_shape=[jax.ShapeDtypeStruct((B, N_EXPERTS, cap), jnp.int32),
                   jax.ShapeDtypeStruct((B, N_EXPERTS, cap), F32)],
        compiler_params=_params(("parallel",)),
        name="select",
    )(aff, tri, pos)


def _moe_body(idx_ref, gate_ref, h2_hbm, wgu_ref, wd_ref, out_hbm,
              h2_ref, acc_ref, xg_a, xg_b, ye_a, ye_b, h2_sem, acc_sem, *, cap, stride, n_samples):
    b = pl.program_id(0)
    k = pl.program_id(1)
    nchunk = D_MODEL // LANES
    rows_per_sample = h2_hbm.shape[1]
    slot = b % 2
    group = 8

    def slot_rows(s):
        return pl.ds(pl.multiple_of(s * rows_per_sample, rows_per_sample), rows_per_sample)

    def h2_copy(sample, s):
        return pltpu.make_async_copy(h2_hbm.at[sample], h2_ref.at[slot_rows(s)], h2_sem.at[s])

    def acc_copy(sample, s):
        return pltpu.make_async_copy(acc_ref.at[slot_rows(s)], out_hbm.at[sample], acc_sem.at[s])

    def zero_of(tile):
        bits = pltpu.bitcast(tile, jnp.uint32)
        return pltpu.bitcast(lax.shift_right_logical(bits, jnp.uint32(32)), F32)

    def gather(e, xg_ref):
        def rows(s0, hold):
            for s in range(s0, s0 + group):
                base = pl.multiple_of(idx_ref[0, e, 0, s], SUBLANES)
                tile = h2_ref[pl.ds(base, nchunk), :] + hold
                xg_ref[pl.ds(s, nchunk, stride=stride), :] = tile
            return tile
        return [functools.partial(rows, s0) for s0 in range(0, cap, group)]

    def scatter(e, ye_ref):
        def rows(s0, hold):
            bases, news = [], []
            for s in range(s0, s0 + group):
                base = pl.multiple_of(idx_ref[0, e, 0, s], SUBLANES)
                bases.append(base)
                news.append(acc_ref[pl.ds(base, nchunk), :]
                            + (ye_ref[pl.ds(s, nchunk, stride=stride), :] + hold))
            for base, new in zip(bases, news):
                acc_ref[pl.ds(base, nchunk), :] = new
            return news[-1]
        return [functools.partial(rows, s0) for s0 in range(0, cap, group)]

    def expert(j, e, xg_ref, ye_ref):
        state = {}
        n_mid = D_EXPERT // MXU_TILE
        n_out = D_MODEL // MXU_TILE

        def load(zero):
            xb = jnp.concatenate([xg_ref[c * stride:c * stride + cap, :] for c in range(nchunk)], axis=1)
            state["x"] = xb.astype(BF16)
            state["gate"] = jnp.broadcast_to(gate_ref[0, e], (LANES, cap)).T[:, 0:1]
            state["h"] = []
            return xb[0:SUBLANES, 0:LANES]

        def mid(n, zero):
            cols = slice(n * MXU_TILE, (n + 1) * MXU_TILE)
            a = _dot(state["x"], wgu_ref[j, :, cols]) + zero
            u = _dot(state["x"], wgu_ref[j, :, D_EXPERT + n * MXU_TILE:D_EXPERT + (n + 1) * MXU_TILE])
            h = _silu(a) * u
            state["h"].append(h.astype(BF16))
            return h[0:SUBLANES, 0:LANES]

        def out(n, zero):
            cols = slice(n * MXU_TILE, (n + 1) * MXU_TILE)
            ye = _dot(state["h"][0], wd_ref[j, 0:MXU_TILE, cols])
            for m in range(1, n_mid):
                ye = ye + _dot(state["h"][m], wd_ref[j, m * MXU_TILE:(m + 1) * MXU_TILE, cols])
            ye = ye * state["gate"] + zero
            for c in range(MXU_TILE // LANES):
                cc = n * (MXU_TILE // LANES) + c
                ye_ref[cc * stride:cc * stride + cap, :] = ye[:, c * LANES:(c + 1) * LANES]
            return ye[0:SUBLANES, 0:LANES]

        return ([load] + [functools.partial(mid, n) for n in range(n_mid)]
                + [functools.partial(out, n) for n in range(n_out)])

    def paced(main, side):
        none = jnp.zeros((SUBLANES, LANES), F32)
        hold = none
        edges = [none, none, none]
        done = 0
        for i, piece in enumerate(main):
            zero = jnp.concatenate([edges[0][0:1, :]] * (MXU_TILE // LANES), axis=1)
            witness = piece(zero)
            upto = (len(side) * (i + 1)) // len(main)
            tiles = [s(hold) for s in side[done:upto]]
            done = upto
            hold = zero_of(witness)
            edges = edges[1:] + [zero_of(functools.reduce(jnp.add, tiles)) if tiles else none]

    def zipped(a, b):
        return [f for pair in zip(a, b) for f in pair]

    @pl.when(k == 0)
    def _():
        @pl.when(b == 0)
        def _():
            h2_copy(0, 0).start()

        h2_copy(b, slot).wait()

        @pl.when(b + 1 < n_samples)
        def _():
            h2_copy(b + 1, 1 - slot).start()

        @pl.when(b >= 2)
        def _():
            acc_copy(b - 2, slot).wait()

        acc_ref[slot_rows(slot), :] = jnp.zeros((rows_per_sample, LANES), F32)
        ye_b[...] = jnp.zeros_like(ye_b)
        for piece in gather(0, xg_a):
            piece(jnp.zeros((SUBLANES, LANES), F32))

    n_experts = 2 * pl.num_programs(1)
    e0 = 2 * k
    e_prev = jnp.maximum(e0 - 1, 0)
    e_next = jnp.minimum(e0 + 2, n_experts - 1)
    paced(expert(0, e0, xg_a, ye_a), zipped(scatter(e_prev, ye_b), gather(e0 + 1, xg_b)))
    paced(expert(1, e0 + 1, xg_b, ye_b), zipped(scatter(e0, ye_a), gather(e_next, xg_a)))

    @pl.when(k == pl.num_programs(1) - 1)
    def _():
        for piece in scatter(n_experts - 1, ye_b):
            piece(jnp.zeros((SUBLANES, LANES), F32))
        acc_copy(b, slot).start()

        @pl.when(b == n_samples - 1)
        def _():
            acc_copy(b, slot).wait()
            if n_samples >= 2:
                acc_copy(b - 1, 1 - slot).wait()


def _moe_call(idx, gate, h2t, wgu, wd, T):
    B, E, cap = idx.shape
    nchunk = D_MODEL // LANES
    stride = cap + SUBLANES
    buf = pltpu.VMEM((nchunk * stride, LANES), F32)
    idx4 = idx.reshape(B, E, 1, cap)
    gate4 = gate.reshape(B, E, 1, cap)

    two_samples = pltpu.VMEM((2 * T * nchunk, LANES), F32)
    return pl.pallas_call(
        functools.partial(_moe_body, cap=cap, stride=stride, n_samples=B),
        grid=(B, E // 2),
        in_specs=[
            pl.BlockSpec((1, E, 1, cap), lambda b, k: (b, 0, 0, 0), memory_space=pltpu.SMEM),
            pl.BlockSpec((1, E, 1, cap), lambda b, k: (b, 0, 0, 0)),
            pl.BlockSpec(memory_space=pl.ANY),
            pl.BlockSpec((2, D_MODEL, 2 * D_EXPERT), lambda b, k: (k, 0, 0)),
            pl.BlockSpec((2, D_EXPERT, D_MODEL), lambda b, k: (k, 0, 0)),
        ],
        out_specs=pl.BlockSpec(memory_space=pl.ANY),
        out_shape=jax.ShapeDtypeStruct((B, T * nchunk, LANES), F32),
        scratch_shapes=[two_samples, two_samples, buf, buf, buf, buf,
                        pltpu.SemaphoreType.DMA((2,)), pltpu.SemaphoreType.DMA((2,))],
        compiler_params=_params(("arbitrary", "arbitrary")),
        name="moe",
    )(idx4, gate4, h2t, wgu, wd)


def _final_body(x1_ref, y_ref, g2_ref, o_ref):
    tm = x1_ref.shape[1]
    for c in range(D_MODEL // LANES):
        cols = slice(c * LANES, (c + 1) * LANES)
        y = y_ref[0, pl.ds(c, tm, stride=D_MODEL // LANES), :]
        o_ref[0, :, cols] = x1_ref[0, :, cols] + g2_ref[0, :, cols] * y


def _final_call(x1, y, g2, tm):
    B, T, _ = x1.shape
    nchunk = D_MODEL // LANES
    tok = pl.BlockSpec((1, tm, D_MODEL), lambda b, j: (b, j, 0))
    return pl.pallas_call(
        _final_body,
        grid=(B, T // tm),
        in_specs=[tok,
                  pl.BlockSpec((1, tm * nchunk, LANES), lambda b, j: (b, j, 0)),
                  pl.BlockSpec((1, 1, D_MODEL), lambda b, j: (b, 0, 0))],
        out_specs=tok,
        out_shape=jax.ShapeDtypeStruct((B, T, D_MODEL), F32),
        compiler_params=_params(("parallel", "parallel")),
        name="final",
    )(x1, y, g2)


def _slot_layout():
    half = QK_ROPE // 2
    n_freq = half // 2
    src = np.full((HEAD_SLOT,), -1, np.int64)
    for axis in range(2):
        first = QK_NOPE + axis * half
        src[axis * n_freq:(axis + 1) * n_freq] = np.arange(first, first + n_freq)
        src[LANES // 2 + axis * n_freq:LANES // 2 + (axis + 1) * n_freq] = np.arange(first + n_freq, first + half)
    src[half:half + QK_NOPE // 2] = np.arange(0, QK_NOPE // 2)
    src[LANES // 2 + half:LANES // 2 + half + QK_NOPE // 2] = np.arange(QK_NOPE // 2, QK_NOPE)
    assert src[SHIFT_LANE] == -1 and sorted(src[src >= 0]) == list(range(QK_DIM))
    return src


_SLOT_SRC = _slot_layout()


def _rope_tables(T):
    half = QK_ROPE // 2
    n_freq = half // 2
    inv_freq = 1.0 / (ROPE_THETA ** (np.arange(n_freq, dtype=np.float32) / n_freq))
    t = np.arange(T)
    cos = np.ones((T, HEAD_SLOT), np.float32)
    sin = np.zeros((T, HEAD_SLOT), np.float32)
    for axis, pos in enumerate((t // GRID_W, t % GRID_W)):
        ang = pos.astype(np.float32)[:, None] * inv_freq[None, :].astype(np.float32)
        c, s = np.cos(ang).astype(np.float32), np.sin(ang).astype(np.float32)
        lo1 = axis * n_freq
        lo2 = LANES // 2 + axis * n_freq
        cos[:, lo1:lo1 + n_freq] = c
        cos[:, lo2:lo2 + n_freq] = c
        sin[:, lo1:lo1 + n_freq] = -s
        sin[:, lo2:lo2 + n_freq] = s
    return jnp.asarray(cos), jnp.asarray(sin)


def _dft_tables(T):
    assert T == FFT_R * FFT_N
    n = np.arange(F_GROUP_DIM)
    ang_c = 2.0 * np.pi * ((n[:, None] * n[None, :]) % F_GROUP_DIM) / F_GROUP_DIM
    cc = np.concatenate([np.cos(ang_c), -np.sin(ang_c)], axis=1) / np.sqrt(F_GROUP_DIM)
    k1 = np.repeat(np.arange(FFT_R), FFT_N)
    n2 = np.tile(np.arange(FFT_N), FFT_R)
    ang_w = 2.0 * np.pi * (k1 * n2) / T
    twc = np.broadcast_to(np.cos(ang_w)[:, None], (T, F_GROUP_DIM))
    tws = np.broadcast_to(np.sin(ang_w)[:, None], (T, F_GROUP_DIM))
    m = np.arange(FFT_N)
    ang_2 = 2.0 * np.pi * ((m[:, None] * m[None, :]) % FFT_N) / FFT_N
    m2 = np.concatenate([np.cos(ang_2), np.sin(ang_2)], axis=1) / np.sqrt(T)
    return (jnp.asarray(cc, F32).astype(BF16), jnp.asarray(twc, F32), jnp.asarray(tws, F32),
            jnp.asarray(m2, F32).astype(BF16))


def kernel(x, c, ctx, c_ctx, w_mod, b_mod, norm1_g, w_in, q_a_norm_g, kv_a_norm_g, w_q_up, w_kv_up,
           q_norm_g, k_norm_g, w_o_attn, w_fourier, w_out, norm2_g, w_router, w_e_gate, w_e_up, w_e_down):
    B, T, D = x.shape
    assert w_mod.shape[0] == 1 and D == D_MODEL and T % 1024 == 0
    H = MLA_HEADS
    cap = (CAPACITY_FACTOR * T) // N_EXPERTS

    rows = -(-(B + 1) // SUBLANES) * SUBLANES
    cc_in = jnp.concatenate([c, c_ctx[None, :], jnp.zeros((rows - B - 1, D), F32)], axis=0)
    mod = _mod_call(cc_in, w_mod[0], b_mod)
    sh1, sc1, g1, sh2, sc2, g2 = [mod[:B, i * D:(i + 1) * D].reshape(B, 1, D) for i in range(6)]
    csh1 = mod[B:B + 1, 0:D].reshape(1, 1, D)
    csc1 = mod[B:B + 1, D:2 * D].reshape(1, 1, D)

    src = _SLOT_SRC
    used = jnp.asarray(src >= 0, F32)
    nope = jnp.asarray((src >= 0) & (src < QK_NOPE), F32)
    rope = jnp.asarray(src >= QK_NOPE, F32)
    lane_src = np.maximum(src, 0)
    wi = w_in[0]
    pe_cols = jnp.take(wi[:, OFF_KPE:OFF_F], np.maximum(src - QK_NOPE, 0), axis=1) * rope
    w_in_p = jnp.concatenate([wi[:, OFF_Q:OFF_KV], pe_cols, wi[:, OFF_KV:OFF_KPE], wi[:, OFF_F:N_IN]], axis=1)
    w_in_p = w_in_p.astype(BF16)
    wq_p = jnp.take(w_q_up[0].reshape(Q_LORA, H, QK_DIM), lane_src, axis=2) * used
    wq_p = wq_p.reshape(Q_LORA, H * HEAD_SLOT).astype(BF16)
    wkv = w_kv_up[0].reshape(KV_LORA, H, QK_NOPE + V_DIM)
    wk_p = jnp.take(wkv[:, :, :QK_NOPE], np.minimum(lane_src, QK_NOPE - 1), axis=2) * nope
    wk_p = wk_p.reshape(KV_LORA, H * HEAD_SLOT).astype(BF16)
    wv_p = wkv[:, :, QK_NOPE:].reshape(KV_LORA, H * V_DIM).astype(BF16)
    qg_p = (jnp.take(q_norm_g[0], lane_src) * used * (QK_DIM ** -0.5 * LOG2E)).reshape(1, HEAD_SLOT)
    kg_p = (jnp.take(k_norm_g[0], lane_src) * used).reshape(1, HEAD_SLOT)
    bound = (QK_DIM * BOUND_SLACK) * jnp.max(jnp.abs(qg_p)) * jnp.max(jnp.abs(kg_p))
    on_shift = jnp.asarray(np.arange(HEAD_SLOT) == SHIFT_LANE, F32).reshape(1, HEAD_SLOT)
    qadd = on_shift * (-bound)
    kadd = on_shift
    qag = q_a_norm_g[0].reshape(1, Q_LORA)
    kvag = kv_a_norm_g[0].reshape(1, KV_LORA)
    g1n = norm1_g[0].reshape(1, D)
    g2n = norm2_g[0].reshape(1, D)
    cos_t, sin_t = _rope_tables(T)

    q, k_lat, vt_lat, f, ga, gb = _proj_lat_call(x, g1n, sc1, sh1, w_in_p, qag, kvag, wq_p, wk_p, wv_p,
                                                 qg_p, kg_p, qadd, kadd, cos_t, sin_t, tm=512)
    k_ctx, vt_ctx = _proj_ctx_call(ctx, g1n, csc1, csh1, w_in_p, kvag, wk_p, wv_p, kg_p, kadd)
    attn = _attn_call(q, k_lat, k_ctx, vt_lat, vt_ctx, tq=512)
    four = _fourier_call(f, *_dft_tables(T))

    wr = jnp.pad(w_router[0], ((0, 0), (0, LANES - N_EXPERTS)))
    wrh = wr.astype(BF16)
    wrl = wrh + jnp.roll((wr - wrh.astype(F32)).astype(BF16), N_EXPERTS, axis=1)
    x1, h2t, aff = _merge_call(attn, four, ga, gb, x, g1, g2n, sc2, sh2, w_o_attn[0].astype(BF16),
                               w_fourier[0].astype(BF16), w_out[0].astype(BF16), wrh, wrl, tm=512)

    tri = jnp.asarray(np.triu(np.ones((LANES, LANES), np.float32)), BF16)
    addr = np.arange(T) * (D // LANES)
    pos = jnp.asarray(np.stack([addr // LANES, addr % LANES]).astype(np.float32), BF16)
    idx, gate = _select_call(aff, tri, pos, cap)
    w_gate_up = jnp.concatenate([w_e_gate[0].astype(BF16), w_e_up[0].astype(BF16)], axis=2)
    y = _moe_call(idx, gate, h2t, w_gate_up, w_e_down[0].astype(BF16), T)
    return _final_call(x1, y, g2, tm=1024)
```

```python
import functools

import numpy as np
import jax
import jax.numpy as jnp
from jax import lax
from jax.experimental import pallas as pl
from jax.experimental.pallas import tpu as pltpu

F32 = jnp.float32
BF16 = jnp.bfloat16

D_MODEL = 1024
GRID_W = 64
MLA_HEADS = 8
QK_NOPE = 64
QK_ROPE = 32
QK_DIM = QK_NOPE + QK_ROPE
V_DIM = 64
Q_LORA = 384
KV_LORA = 256
ROPE_THETA = 10000.0
F_GROUPS = 4
F_GROUP_DIM = 128
D_F = F_GROUPS * F_GROUP_DIM
OFF_Q = 0
OFF_KV = OFF_Q + Q_LORA
OFF_KPE = OFF_KV + KV_LORA
OFF_F = OFF_KPE + QK_ROPE
OFF_GA = OFF_F + D_F
OFF_GB = OFF_GA + D_MODEL
N_IN = OFF_GB + D_MODEL
N_EXPERTS = 16
D_EXPERT = 512
CAPACITY_FACTOR = 2
EPS = 1e-6

LANES = 128
SUBLANES = 8
MXU_TILE = 256
HEAD_SLOT = LANES
SHIFT_LANE = 48
BOUND_SLACK = 1.02
DEN_FLOOR = 2.0 ** -60
LOG2E = 1.4426950408889634
FFT_R = 8
FFT_N = 256
SEARCH_BITS = 3
SEARCH_DONE = 2.0 ** -30
SEARCH_MAX_ROUNDS = 56
KEY_CHUNK = 256
VMEM_LIMIT = 56 * 1024 * 1024

PC_Q = 0
PC_PE = PC_Q + Q_LORA
PC_KV = PC_PE + HEAD_SLOT
PC_F = PC_KV + KV_LORA
PC_GA = PC_F + D_F
PC_GB = PC_GA + D_MODEL
PC_END = PC_GB + D_MODEL


def _dot(a, b):
    return jnp.dot(a, b, preferred_element_type=F32)


def _dot_nt(a, b):
    return lax.dot_general(a, b, (((1,), (1,)), ((), ())), preferred_element_type=F32)


def _split2(a):
    hi = a.astype(BF16)
    lo = (a - hi.astype(F32)).astype(BF16)
    return hi, lo


def _split3(a):
    hi = a.astype(BF16)
    r = a - hi.astype(F32)
    mid = r.astype(BF16)
    lo = (r - mid.astype(F32)).astype(BF16)
    return hi, mid, lo


def _dot3(a, b):
    ah, al = _split2(a)
    bh, bl = _split2(b)
    return _dot(ah, bh) + (_dot(ah, bl) + _dot(al, bh))


def _sigmoid(x):
    return 1.0 / (1.0 + jnp.exp(-x))


def _silu(x):
    return x * (0.5 * jnp.tanh(0.5 * x) + 0.5)


def _params(sem):
    return pltpu.CompilerParams(dimension_semantics=sem, vmem_limit_bytes=VMEM_LIMIT)


def _mod_body(c_ref, w_ref, b_ref, o_ref):
    c = c_ref[...]
    s = c * _sigmoid(c)
    o_ref[...] = _dot3(s, w_ref[...]) + b_ref[...]


def _mod_call(cc, w_mod, b_mod):
    rows = cc.shape[0]
    n = w_mod.shape[1]
    tn = 1024
    return pl.pallas_call(
        _mod_body,
        grid=(n // tn,),
        in_specs=[
            pl.BlockSpec((rows, D_MODEL), lambda j: (0, 0)),
            pl.BlockSpec((D_MODEL, tn), lambda j: (0, j)),
            pl.BlockSpec((1, tn), lambda j: (0, j)),
        ],
        out_specs=pl.BlockSpec((rows, tn), lambda j: (0, j)),
        out_shape=jax.ShapeDtypeStruct((rows, n), F32),
        compiler_params=_params(("arbitrary",)),
        name="mod",
    )(cc, w_mod, b_mod)


def _rms_rows(x, n):
    return x * lax.rsqrt(jnp.sum(x * x, axis=-1, keepdims=True) * (1.0 / n) + EPS)


def _rope(x, cos, sin):
    return x * cos + pltpu.roll(x, LANES // 2, 1) * sin


def _proj_body(*refs, latent):
    if latent:
        (x_ref, g_ref, sc_ref, sh_ref, win_ref, qag_ref, kvag_ref, wq_ref, wk_ref, wv_ref,
         qg_ref, kg_ref, qadd_ref, kadd_ref, cos_ref, sin_ref,
         q_out, k_out, v_out, f_out, ga_out, gb_out) = refs
    else:
        (x_ref, g_ref, sc_ref, sh_ref, win_ref, kvag_ref, wk_ref, wv_ref, kg_ref, kadd_ref,
         k_out, v_out) = refs
    x = x_ref[0]
    h = _rms_rows(x, D_MODEL) * (g_ref[...] * (1.0 + sc_ref[0])) + sh_ref[0]
    hb = h.astype(BF16)

    if latent:
        cos = cos_ref[...]
        sin = sin_ref[...]

    pe = _dot(hb, win_ref[:, PC_PE:PC_KV])

    ckv = _rms_rows(_dot(hb, win_ref[:, PC_KV:PC_F]), KV_LORA) * kvag_ref[...]
    ckvb = ckv.astype(BF16)
    kall = _dot(ckvb, wk_ref[...])
    kg = kg_ref[...]
    kadd = kadd_ref[...]
    for hd in range(MLA_HEADS):
        kh = kall[:, hd * HEAD_SLOT:(hd + 1) * HEAD_SLOT] + pe
        kh = _rms_rows(kh, QK_DIM) * kg
        if latent:
            kh = _rope(kh, cos, sin)
        k_out[0, hd] = (kh + kadd).astype(BF16)
    v = _dot(ckvb, wv_ref[...])
    for hp in range(MLA_HEADS // 2):
        v_out[0, hp] = v[:, hp * LANES:(hp + 1) * LANES].T.astype(BF16)

    if latent:
        cq = _rms_rows(_dot(hb, win_ref[:, PC_Q:PC_PE]), Q_LORA) * qag_ref[...]
        qall = _dot(cq.astype(BF16), wq_ref[...])
        qg = qg_ref[...]
        qadd = qadd_ref[...]
        for hd in range(MLA_HEADS):
            qh = qall[:, hd * HEAD_SLOT:(hd + 1) * HEAD_SLOT]
            q_out[0, hd] = (_rope(_rms_rows(qh, QK_DIM) * qg, cos, sin) + qadd).astype(BF16)
        f_out[0] = _dot(hb, win_ref[:, PC_F:PC_GA]).astype(BF16)
        ga_out[0] = _sigmoid(_dot(hb, win_ref[:, PC_GA:PC_GB])).astype(BF16)
        gb_out[0] = _sigmoid(_dot(hb, win_ref[:, PC_GB:PC_END])).astype(BF16)


def _full(shape):
    nd = len(shape)
    return pl.BlockSpec(shape, lambda *_: (0,) * nd)


def _proj_lat_call(x, g1n, sc1, sh1, w_in_p, qag, kvag, wq_p, wk_p, wv_p, qg_p, kg_p, qadd, kadd, cos_t, sin_t, tm):
    B, T, _ = x.shape
    H = MLA_HEADS
    tok = lambda w: pl.BlockSpec((1, tm, w), lambda b, j: (b, j, 0))
    per_b = pl.BlockSpec((1, 1, D_MODEL), lambda b, j: (b, 0, 0))
    tab = pl.BlockSpec((tm, HEAD_SLOT), lambda b, j: (j, 0))
    return pl.pallas_call(
        functools.partial(_proj_body, latent=True),
        grid=(B, T // tm),
        in_specs=[tok(D_MODEL), _full(g1n.shape), per_b, per_b, _full(w_in_p.shape), _full(qag.shape),
                  _full(kvag.shape), _full(wq_p.shape), _full(wk_p.shape), _full(wv_p.shape),
                  _full(qg_p.shape), _full(kg_p.shape), _full(qadd.shape), _full(kadd.shape), tab, tab],
        out_specs=[
            pl.BlockSpec((1, H, tm, HEAD_SLOT), lambda b, j: (b, 0, j, 0)),
            pl.BlockSpec((1, H, tm, HEAD_SLOT), lambda b, j: (b, 0, j, 0)),
            pl.BlockSpec((1, H // 2, LANES, tm), lambda b, j: (b, 0, 0, j)),
            tok(D_F), tok(D_MODEL), tok(D_MODEL),
        ],
        out_shape=[
            jax.ShapeDtypeStruct((B, H, T, HEAD_SLOT), BF16),
            jax.ShapeDtypeStruct((B, H, T, HEAD_SLOT), BF16),
            jax.ShapeDtypeStruct((B, H // 2, LANES, T), BF16),
            jax.ShapeDtypeStruct((B, T, D_F), BF16),
            jax.ShapeDtypeStruct((B, T, D_MODEL), BF16),
            jax.ShapeDtypeStruct((B, T, D_MODEL), BF16),
        ],
        compiler_params=_params(("parallel", "parallel")),
        name="proj_lat",
    )(x, g1n, sc1, sh1, w_in_p, qag, kvag, wq_p, wk_p, wv_p, qg_p, kg_p, qadd, kadd, cos_t, sin_t)


def _proj_ctx_call(ctx, g1n, csc1, csh1, w_in_p, kvag, wk_p, wv_p, kg_p, kadd):
    B, TC, _ = ctx.shape
    H = MLA_HEADS
    shared = pl.BlockSpec((1, 1, D_MODEL), lambda b: (0, 0, 0))
    return pl.pallas_call(
        functools.partial(_proj_body, latent=False),
        grid=(B,),
        in_specs=[pl.BlockSpec((1, TC, D_MODEL), lambda b: (b, 0, 0)), _full(g1n.shape), shared, shared,
                  _full(w_in_p.shape), _full(kvag.shape), _full(wk_p.shape), _full(wv_p.shape),
                  _full(kg_p.shape), _full(kadd.shape)],
        out_specs=[
            pl.BlockSpec((1, H, TC, HEAD_SLOT), lambda b: (b, 0, 0, 0)),
            pl.BlockSpec((1, H // 2, LANES, TC), lambda b: (b, 0, 0, 0)),
        ],
        out_shape=[
            jax.ShapeDtypeStruct((B, H, TC, HEAD_SLOT), BF16),
            jax.ShapeDtypeStruct((B, H // 2, LANES, TC), BF16),
        ],
        compiler_params=_params(("parallel",)),
        name="proj_ctx",
    )(ctx, g1n, csc1, csh1, w_in_p, kvag, wk_p, wv_p, kg_p, kadd)


def _attn_body(q_ref, kl_ref, kc_ref, vl_ref, vc_ref, o_ref):
    tq = q_ref.shape[2]
    T = kl_ref.shape[2]
    nck = T // KEY_CHUNK
    row = lax.broadcasted_iota(jnp.int32, (LANES, tq), 0)

    def probs(hd):
        q = q_ref[0, hd]
        den = jnp.zeros((1, tq), F32)
        ps = []
        for c in range(nck + 1):
            k = kc_ref[0, hd] if c == nck else kl_ref[0, hd, c * KEY_CHUNK:(c + 1) * KEY_CHUNK, :]
            e = jnp.exp2(_dot_nt(k, q))
            den = den + jnp.sum(e, axis=0, keepdims=True)
            ps.append(e.astype(BF16))
        return ps, den

    def weighted_values(hd, ps):
        hp = hd // 2
        o = _dot(vc_ref[0, hp], ps[nck])
        for c in range(nck):
            o = o + _dot(vl_ref[0, hp, :, c * KEY_CHUNK:(c + 1) * KEY_CHUNK], ps[c])
        return o

    outs, dens, prev = [], [], None
    for hd in range(MLA_HEADS + 1):
        cur = probs(hd) if hd < MLA_HEADS else None
        if prev is not None:
            outs.append(weighted_values(hd - 1, prev[0]) / prev[1])
            dens.append(prev[1])
        prev = cur
    for hp in range(MLA_HEADS // 2):
        o_ref[0, hp] = jnp.where(row < V_DIM, outs[2 * hp], outs[2 * hp + 1]).astype(BF16)
    min_den = functools.reduce(jnp.minimum, dens)

    @pl.when(jnp.min(min_den) < DEN_FLOOR)
    def _():
        def pair(hp, carry):
            res = []
            for hd in (2 * hp, 2 * hp + 1):
                q = q_ref[0, hd]
                sl = _dot_nt(kl_ref[0, hd], q)
                sc = _dot_nt(kc_ref[0, hd], q)
                m = jnp.maximum(jnp.max(sl, axis=0, keepdims=True), jnp.max(sc, axis=0, keepdims=True))
                el = jnp.exp2(sl - m)
                ec = jnp.exp2(sc - m)
                den = jnp.sum(el, axis=0, keepdims=True) + jnp.sum(ec, axis=0, keepdims=True)
                o = _dot(vl_ref[0, hp], el.astype(BF16)) + _dot(vc_ref[0, hp], ec.astype(BF16))
                res.append(o / den)
            o_ref[0, hp] = jnp.where(row < V_DIM, res[0], res[1]).astype(BF16)
            return carry

        lax.fori_loop(0, MLA_HEADS // 2, pair, 0)


def _attn_call(q, k_lat, k_ctx, vt_lat, vt_ctx, tq):
    B, H, T, _ = q.shape
    TC = k_ctx.shape[2]
    return pl.pallas_call(
        _attn_body,
        grid=(B, T // tq),
        in_specs=[
            pl.BlockSpec((1, H, tq, HEAD_SLOT), lambda b, j: (b, 0, j, 0)),
            pl.BlockSpec((1, H, T, HEAD_SLOT), lambda b, j: (b, 0, 0, 0)),
            pl.BlockSpec((1, H, TC, HEAD_SLOT), lambda b, j: (b, 0, 0, 0)),
            pl.BlockSpec((1, H // 2, LANES, T), lambda b, j: (b, 0, 0, 0)),
            pl.BlockSpec((1, H // 2, LANES, TC), lambda b, j: (b, 0, 0, 0)),
        ],
        out_specs=pl.BlockSpec((1, H // 2, LANES, tq), lambda b, j: (b, 0, 0, j)),
        out_shape=jax.ShapeDtypeStruct((B, H // 2, LANES, T), BF16),
        compiler_params=_params(("parallel", "parallel")),
        name="attn",
    )(q, k_lat, k_ctx, vt_lat, vt_ctx)


def _cadd(a, b):
    return a[0] + b[0], a[1] + b[1]


def _csub(a, b):
    return a[0] - b[0], a[1] - b[1]


def _cmul_neg_i(a):
    return a[1], -a[0]


def _cmul_pos_i(a):
    return -a[1], a[0]


def _fft8(u):
    r = 0.7071067811865476
    a0, a1 = _cadd(u[0], u[4]), _csub(u[0], u[4])
    a2, a3 = _cadd(u[2], u[6]), _csub(u[2], u[6])
    a4, a5 = _cadd(u[1], u[5]), _csub(u[1], u[5])
    a6, a7 = _cadd(u[3], u[7]), _csub(u[3], u[7])
    e0, e2 = _cadd(a0, a2), _csub(a0, a2)
    e1, e3 = _cadd(a1, _cmul_neg_i(a3)), _cadd(a1, _cmul_pos_i(a3))
    o0, o2 = _cadd(a4, a6), _csub(a4, a6)
    o1, o3 = _cadd(a5, _cmul_neg_i(a7)), _cadd(a5, _cmul_pos_i(a7))
    wo1 = (r * (o1[0] + o1[1]), r * (o1[1] - o1[0]))
    wo3 = (r * (o3[1] - o3[0]), -r * (o3[0] + o3[1]))
    return [_cadd(e0, o0), _cadd(e1, wo1), _cadd(e2, _cmul_neg_i(o2)), _cadd(e3, wo3),
            _csub(e0, o0), _csub(e1, wo1), _cadd(e2, _cmul_pos_i(o2)), _csub(e3, wo3)]


def _fourier_body(f_ref, cc_ref, twc_ref, tws_ref, m2_ref, o_ref, b_ref, y_ref):
    for g in range(F_GROUPS):
        cols = slice(g * F_GROUP_DIM, (g + 1) * F_GROUP_DIM)
        y = _dot(f_ref[0, :, cols], cc_ref[...])
        u = [(y[j * FFT_N:(j + 1) * FFT_N, :F_GROUP_DIM], y[j * FFT_N:(j + 1) * FFT_N, F_GROUP_DIM:])
             for j in range(FFT_R)]
        a = _fft8(u)
        for k1 in range(FFT_R):
            ar, ai = a[k1]
            if k1 > 0:
                c = twc_ref[k1 * FFT_N:(k1 + 1) * FFT_N, :]
                s = tws_ref[k1 * FFT_N:(k1 + 1) * FFT_N, :]
                ar, ai = ar * c + ai * s, ai * c - ar * s
            b_ref[k1, 0:FFT_N, cols] = ar.astype(BF16)
            b_ref[k1, FFT_N:2 * FFT_N, cols] = ai.astype(BF16)
    for k1 in range(FFT_R):
        y = _dot(m2_ref[...], b_ref[k1])
        for g in range(F_GROUPS):
            y_ref[g, pl.ds(k1, FFT_N, stride=FFT_R), :] = y[:, g * F_GROUP_DIM:(g + 1) * F_GROUP_DIM]
    for g in range(F_GROUPS):
        o_ref[0, :, g * F_GROUP_DIM:(g + 1) * F_GROUP_DIM] = y_ref[g].astype(BF16)


def _fourier_call(f, cc, twc, tws, m2):
    B, T, _ = f.shape
    return pl.pallas_call(
        _fourier_body,
        grid=(B,),
        in_specs=[pl.BlockSpec((1, T, D_F), lambda b: (b, 0, 0)), _full(cc.shape), _full(twc.shape),
                  _full(tws.shape), _full(m2.shape)],
        out_specs=pl.BlockSpec((1, T, D_F), lambda b: (b, 0, 0)),
        out_shape=jax.ShapeDtypeStruct((B, T, D_F), BF16),
        scratch_shapes=[pltpu.VMEM((FFT_R, 2 * FFT_N, D_F), BF16), pltpu.VMEM((F_GROUPS, T, F_GROUP_DIM), F32)],
        compiler_params=_params(("parallel",)),
        name="fourier",
    )(f, cc, twc, tws, m2)


def _merge_body(at_ref, fo_ref, ga_ref, gb_ref, x_ref, g1_ref, g2n_ref, sc2_ref, sh2_ref,
                wo_ref, wf_ref, wout_ref, wrh_ref, wrl_ref, x1_out, h2_out, aff_out):
    tm = x_ref.shape[1]
    attn_t = jnp.concatenate([at_ref[0, hp] for hp in range(MLA_HEADS // 2)], axis=0)
    a = lax.dot_general(attn_t, wo_ref[...], (((0,), (0,)), ((), ())), preferred_element_type=F32)
    fo = _dot(fo_ref[0], wf_ref[...])
    mix = ga_ref[0].astype(F32) * a + gb_ref[0].astype(F32) * fo
    y = _dot(mix.astype(BF16), wout_ref[...])
    x1 = x_ref[0] + g1_ref[0] * y
    x1_out[0] = x1
    h2 = _rms_rows(x1, D_MODEL) * g2n_ref[...]
    h2 = h2 * (1.0 + sc2_ref[0]) + sh2_ref[0]
    for c in range(D_MODEL // LANES):
        h2_out[0, pl.ds(c, tm, stride=D_MODEL // LANES), :] = h2[:, c * LANES:(c + 1) * LANES]
    hh, hl = _split2(h2)
    p_hi = _dot(hh, wrl_ref[...])
    logits = p_hi + pltpu.roll(p_hi, LANES - N_EXPERTS, 1) + _dot(hl, wrh_ref[...])
    lane = lax.broadcasted_iota(jnp.int32, logits.shape, 1)
    logits = jnp.where(lane < N_EXPERTS, logits, -1e30)
    ex = jnp.exp(logits - jnp.max(logits, axis=-1, keepdims=True))
    aff_out[0] = ex / jnp.sum(ex, axis=-1, keepdims=True)


def _merge_call(attn, four, ga, gb, x, g1, g2n, sc2, sh2, wo, wf, wout, wrh, wrl, tm):
    B, T, _ = x.shape
    H = MLA_HEADS
    tok = lambda w: pl.BlockSpec((1, tm, w), lambda b, j: (b, j, 0))
    per_b = pl.BlockSpec((1, 1, D_MODEL), lambda b, j: (b, 0, 0))
    nchunk = D_MODEL // LANES
    return pl.pallas_call(
        _merge_body,
        grid=(B, T // tm),
        in_specs=[
            pl.BlockSpec((1, H // 2, LANES, tm), lambda b, j: (b, 0, 0, j)),
            tok(D_F), tok(D_MODEL), tok(D_MODEL), tok(D_MODEL), per_b, _full(g2n.shape), per_b, per_b,
            _full(wo.shape), _full(wf.shape), _full(wout.shape), _full(wrh.shape), _full(wrl.shape),
        ],
        out_specs=[
            tok(D_MODEL),
            pl.BlockSpec((1, tm * nchunk, LANES), lambda b, j: (b, j, 0)),
            tok(LANES),
        ],
        out_shape=[
            jax.ShapeDtypeStruct((B, T, D_MODEL), F32),
            jax.ShapeDtypeStruct((B, T * nchunk, LANES), F32),
            jax.ShapeDtypeStruct((B, T, LANES), F32),
        ],
        compiler_params=_params(("parallel", "parallel")),
        name="merge",
    )(attn, four, ga, gb, x, g1, g2n, sc2, sh2, wo, wf, wout, wrh, wrl)


def _prefix_excl(m, tri):
    outs = []
    run = jnp.zeros((m.shape[0], 1), F32)
    for blk in range(m.shape[1] // LANES):
        mb = m[:, blk * LANES:(blk + 1) * LANES]
        inc = _dot(mb.astype(BF16), tri)
        outs.append(inc - mb + run)
        run = run + inc[:, LANES - 1:LANES]
    return jnp.concatenate(outs, axis=1)


def _select_body(aff_ref, tri_ref, pos_ref, idx_out, gate_out, *, cap):
    T = aff_ref.shape[1]
    aff = aff_ref[0].T[0:N_EXPERTS]

    def search(carry):
        base, step, rounds, _ = carry
        thr = base
        for j in range(1, 2 ** SEARCH_BITS):
            cand = base + float(j) * step
            cnt = jnp.sum((aff >= cand).astype(F32), axis=-1, keepdims=True)
            thr = jnp.where(cnt >= float(cap), cand, thr)
        settled = jnp.where((thr > 0.0) & (step < thr * SEARCH_DONE), 1.0, 0.0)
        return thr, step * (0.5 ** SEARCH_BITS), rounds + 1, (jnp.min(settled) < 1.0).astype(jnp.int32)

    init = (jnp.zeros((N_EXPERTS, 1), F32), jnp.full((N_EXPERTS, 1), 2.0 * 0.5 ** SEARCH_BITS, F32),
            jnp.int32(0), jnp.int32(1))
    thr = lax.while_loop(lambda c: (c[3] > 0) & (c[2] < SEARCH_MAX_ROUNDS), search, init)[0]
    gt = (aff > thr).astype(F32)
    eq = (aff == thr).astype(F32)
    need = cap - jnp.sum(gt, axis=-1, keepdims=True)
    tri = tri_ref[...]
    sel = gt + eq * (_prefix_excl(eq, tri) < need).astype(F32)
    slot = jnp.where(sel > 0.0, _prefix_excl(sel, tri), -1.0)

    pos = pos_ref[...]
    srow = lax.broadcasted_iota(jnp.int32, (cap, T), 0).astype(F32)
    zero = jnp.zeros((3, T), BF16)
    for e in range(N_EXPERTS):
        onehot = jnp.where(srow == slot[e:e + 1, :], 1.0, 0.0).astype(BF16)
        ah, am, al = _split3(aff[e:e + 1, :])
        vals = jnp.concatenate([pos, ah, am, al, zero], axis=0)
        res = _dot_nt(vals, onehot)
        idx_out[0, e:e + 1, :] = (res[0:1] * float(LANES) + res[1:2]).astype(jnp.int32)
        gate_out[0, e:e + 1, :] = res[2:3] + (res[3:4] + res[4:5])


def _select_call(aff, tri, pos, cap):
    B, T, _ = aff.shape
    return pl.pallas_call(
        functools.partial(_select_body, cap=cap),
        grid=(B,),
        in_specs=[pl.BlockSpec((1, T, LANES), lambda b: (b, 0, 0)), _full(tri.shape), _full(pos.shape)],
        out_specs=[pl.BlockSpec((1, N_EXPERTS, cap), lambda b: (b, 0, 0)),
                   pl.BlockSpec((1, N_EXPERTS, cap), lambda b: (b, 0, 0))],
        out_shape=[jax.ShapeDtypeStruct((B, N_EXPERTS, cap), jnp.int32),
                   jax.ShapeDtypeStruct((B, N_EXPERTS, cap), F32)],
        compiler_params=_params(("parallel",)),
        name="select",
    )(aff, tri, pos)


def _moe_body(idx_ref, gate_ref, h2_ref, wg_ref, wu_ref, wd_ref, acc_ref, xg_a, xg_b, ye_a, ye_b,
              *, cap, stride):
    k = pl.program_id(1)
    nchunk = D_MODEL // LANES
    group = 8

    def zero_of(tile):
        bits = pltpu.bitcast(tile, jnp.uint32)
        return pltpu.bitcast(lax.shift_right_logical(bits, jnp.uint32(32)), F32)

    def gather(e, xg_ref):
        def rows(s0, hold):
            for s in range(s0, s0 + group):
                base = pl.multiple_of(idx_ref[0, e, 0, s], SUBLANES)
                tile = h2_ref[0, pl.ds(base, nchunk), :] + hold
                xg_ref[pl.ds(s, nchunk, stride=stride), :] = tile
            return tile
        return [functools.partial(rows, s0) for s0 in range(0, cap, group)]

    def scatter(e, ye_ref):
        def rows(s0, hold):
            bases, news = [], []
            for s in range(s0, s0 + group):
                base = pl.multiple_of(idx_ref[0, e, 0, s], SUBLANES)
                bases.append(base)
                news.append(acc_ref[0, pl.ds(base, nchunk), :]
                            + (ye_ref[pl.ds(s, nchunk, stride=stride), :] + hold))
            for base, new in zip(bases, news):
                acc_ref[0, pl.ds(base, nchunk), :] = new
            return news[-1]
        return [functools.partial(rows, s0) for s0 in range(0, cap, group)]

    def expert(j, e, xg_ref, ye_ref):
        state = {}
        n_mid = D_EXPERT // MXU_TILE
        n_out = D_MODEL // MXU_TILE

        def load(zero):
            xb = jnp.concatenate([xg_ref[c * stride:c * stride + cap, :] for c in range(nchunk)], axis=1)
            state["x"] = xb.astype(BF16)
            state["gate"] = jnp.broadcast_to(gate_ref[0, e], (LANES, cap)).T[:, 0:1]
            state["h"] = []
            return xb[0:SUBLANES, 0:LANES]

        def mid(n, zero):
            cols = slice(n * MXU_TILE, (n + 1) * MXU_TILE)
            a = _dot(state["x"], wg_ref[j, :, cols]) + zero
            u = _dot(state["x"], wu_ref[j, :, cols])
            h = _silu(a) * u
            state["h"].append(h.astype(BF16))
            return h[0:SUBLANES, 0:LANES]

        def out(n, zero):
            cols = slice(n * MXU_TILE, (n + 1) * MXU_TILE)
            ye = _dot(state["h"][0], wd_ref[j, 0:MXU_TILE, cols])
            for m in range(1, n_mid):
                ye = ye + _dot(state["h"][m], wd_ref[j, m * MXU_TILE:(m + 1) * MXU_TILE, cols])
            ye = ye * state["gate"] + zero
            for c in range(MXU_TILE // LANES):
                cc = n * (MXU_TILE // LANES) + c
                ye_ref[cc * stride:cc * stride + cap, :] = ye[:, c * LANES:(c + 1) * LANES]
            return ye[0:SUBLANES, 0:LANES]

        return ([load] + [functools.partial(mid, n) for n in range(n_mid)]
                + [functools.partial(out, n) for n in range(n_out)])

    def paced(main, side):
        none = jnp.zeros((SUBLANES, LANES), F32)
        hold = none
        edges = [none, none, none]
        done = 0
        for i, piece in enumerate(main):
            zero = jnp.concatenate([edges[0][0:1, :]] * (MXU_TILE // LANES), axis=1)
            witness = piece(zero)
            upto = (len(side) * (i + 1)) // len(main)
            tiles = [s(hold) for s in side[done:upto]]
            done = upto
            hold = zero_of(witness)
            edges = edges[1:] + [zero_of(functools.reduce(jnp.add, tiles)) if tiles else none]

    def zipped(a, b):
        return [f for pair in zip(a, b) for f in pair]

    @pl.when(k == 0)
    def _():
        acc_ref[...] = jnp.zeros_like(acc_ref)
        ye_b[...] = jnp.zeros_like(ye_b)
        for piece in gather(0, xg_a):
            piece(jnp.zeros((SUBLANES, LANES), F32))

    n_experts = 2 * pl.num_programs(1)
    e0 = 2 * k
    e_prev = jnp.maximum(e0 - 1, 0)
    e_next = jnp.minimum(e0 + 2, n_experts - 1)
    paced(expert(0, e0, xg_a, ye_a), zipped(scatter(e_prev, ye_b), gather(e0 + 1, xg_b)))
    paced(expert(1, e0 + 1, xg_b, ye_b), zipped(scatter(e0, ye_a), gather(e_next, xg_a)))

    @pl.when(k == pl.num_programs(1) - 1)
    def _():
        for piece in scatter(n_experts - 1, ye_b):
            piece(jnp.zeros((SUBLANES, LANES), F32))


def _moe_call(idx, gate, h2t, wg, wu, wd, T):
    B, E, cap = idx.shape
    nchunk = D_MODEL // LANES
    stride = cap + SUBLANES
    buf = pltpu.VMEM((nchunk * stride, LANES), F32)
    idx4 = idx.reshape(B, E, 1, cap)
    gate4 = gate.reshape(B, E, 1, cap)

    return pl.pallas_call(
        functools.partial(_moe_body, cap=cap, stride=stride),
        grid=(B, E // 2),
        in_specs=[
            pl.BlockSpec((1, E, 1, cap), lambda b, k: (b, 0, 0, 0), memory_space=pltpu.SMEM),
            pl.BlockSpec((1, E, 1, cap), lambda b, k: (b, 0, 0, 0)),
            pl.BlockSpec((1, T * nchunk, LANES), lambda b, k: (b, 0, 0)),
            pl.BlockSpec((2, D_MODEL, D_EXPERT), lambda b, k: (k, 0, 0)),
            pl.BlockSpec((2, D_MODEL, D_EXPERT), lambda b, k: (k, 0, 0)),
            pl.BlockSpec((2, D_EXPERT, D_MODEL), lambda b, k: (k, 0, 0)),
        ],
        out_specs=pl.BlockSpec((1, T * nchunk, LANES), lambda b, k: (b, 0, 0)),
        out_shape=jax.ShapeDtypeStruct((B, T * nchunk, LANES), F32),
        scratch_shapes=[buf, buf, buf, buf],
        compiler_params=_params(("parallel", "arbitrary")),
        name="moe",
    )(idx4, gate4, h2t, wg, wu, wd)


def _final_body(x1_ref, y_ref, g2_ref, o_ref):
    tm = x1_ref.shape[1]
    for c in range(D_MODEL // LANES):
        cols = slice(c * LANES, (c + 1) * LANES)
        y = y_ref[0, pl.ds(c, tm, stride=D_MODEL // LANES), :]
        o_ref[0, :, cols] = x1_ref[0, :, cols] + g2_ref[0, :, cols] * y


def _final_call(x1, y, g2, tm):
    B, T, _ = x1.shape
    nchunk = D_MODEL // LANES
    tok = pl.BlockSpec((1, tm, D_MODEL), lambda b, j: (b, j, 0))
    return pl.pallas_call(
        _final_body,
        grid=(B, T // tm),
        in_specs=[tok,
                  pl.BlockSpec((1, tm * nchunk, LANES), lambda b, j: (b, j, 0)),
                  pl.BlockSpec((1, 1, D_MODEL), lambda b, j: (b, 0, 0))],
        out_specs=tok,
        out_shape=jax.ShapeDtypeStruct((B, T, D_MODEL), F32),
        compiler_params=_params(("parallel", "parallel")),
        name="final",
    )(x1, y, g2)


def _slot_layout():
    half = QK_ROPE // 2
    n_freq = half // 2
    src = np.full((HEAD_SLOT,), -1, np.int64)
    for axis in range(2):
        first = QK_NOPE + axis * half
        src[axis * n_freq:(axis + 1) * n_freq] = np.arange(first, first + n_freq)
        src[LANES // 2 + axis * n_freq:LANES // 2 + (axis + 1) * n_freq] = np.arange(first + n_freq, first + half)
    src[half:half + QK_NOPE // 2] = np.arange(0, QK_NOPE // 2)
    src[LANES // 2 + half:LANES // 2 + half + QK_NOPE // 2] = np.arange(QK_NOPE // 2, QK_NOPE)
    assert src[SHIFT_LANE] == -1 and sorted(src[src >= 0]) == list(range(QK_DIM))
    return src


_SLOT_SRC = _slot_layout()


def _rope_tables(T):
    half = QK_ROPE // 2
    n_freq = half // 2
    inv_freq = 1.0 / (ROPE_THETA ** (np.arange(n_freq, dtype=np.float32) / n_freq))
    t = np.arange(T)
    cos = np.ones((T, HEAD_SLOT), np.float32)
    sin = np.zeros((T, HEAD_SLOT), np.float32)
    for axis, pos in enumerate((t // GRID_W, t % GRID_W)):
        ang = pos.astype(np.float32)[:, None] * inv_freq[None, :].astype(np.float32)
        c, s = np.cos(ang).astype(np.float32), np.sin(ang).astype(np.float32)
        lo1 = axis * n_freq
        lo2 = LANES // 2 + axis * n_freq
        cos[:, lo1:lo1 + n_freq] = c
        cos[:, lo2:lo2 + n_freq] = c
        sin[:, lo1:lo1 + n_freq] = -s
        sin[:, lo2:lo2 + n_freq] = s
    return jnp.asarray(cos), jnp.asarray(sin)


def _dft_tables(T):
    assert T == FFT_R * FFT_N
    n = np.arange(F_GROUP_DIM)
    ang_c = 2.0 * np.pi * ((n[:, None] * n[None, :]) % F_GROUP_DIM) / F_GROUP_DIM
    cc = np.concatenate([np.cos(ang_c), -np.sin(ang_c)], axis=1) / np.sqrt(F_GROUP_DIM)
    k1 = np.repeat(np.arange(FFT_R), FFT_N)
    n2 = np.tile(np.arange(FFT_N), FFT_R)
    ang_w = 2.0 * np.pi * (k1 * n2) / T
    twc = np.broadcast_to(np.cos(ang_w)[:, None], (T, F_GROUP_DIM))
    tws = np.broadcast_to(np.sin(ang_w)[:, None], (T, F_GROUP_DIM))
    m = np.arange(FFT_N)
    ang_2 = 2.0 * np.pi * ((m[:, None] * m[None, :]) % FFT_N) / FFT_N
    m2 = np.concatenate([np.cos(ang_2), np.sin(ang_2)], axis=1) / np.sqrt(T)
    return (jnp.asarray(cc, F32).astype(BF16), jnp.asarray(twc, F32), jnp.asarray(tws, F32),
            jnp.asarray(m2, F32).astype(BF16))


def kernel(x, c, ctx, c_ctx, w_mod, b_mod, norm1_g, w_in, q_a_norm_g, kv_a_norm_g, w_q_up, w_kv_up,
           q_norm_g, k_norm_g, w_o_attn, w_fourier, w_out, norm2_g, w_router, w_e_gate, w_e_up, w_e_down):
    B, T, D = x.shape
    assert w_mod.shape[0] == 1 and D == D_MODEL and T % 1024 == 0
    H = MLA_HEADS
    cap = (CAPACITY_FACTOR * T) // N_EXPERTS

    rows = -(-(B + 1) // SUBLANES) * SUBLANES
    cc_in = jnp.concatenate([c, c_ctx[None, :], jnp.zeros((rows - B - 1, D), F32)], axis=0)
    mod = _mod_call(cc_in, w_mod[0], b_mod)
    sh1, sc1, g1, sh2, sc2, g2 = [mod[:B, i * D:(i + 1) * D].reshape(B, 1, D) for i in range(6)]
    csh1 = mod[B:B + 1, 0:D].reshape(1, 1, D)
    csc1 = mod[B:B + 1, D:2 * D].reshape(1, 1, D)

    src = _SLOT_SRC
    used = jnp.asarray(src >= 0, F32)
    nope = jnp.asarray((src >= 0) & (src < QK_NOPE), F32)
    rope = jnp.asarray(src >= QK_NOPE, F32)
    lane_src = np.maximum(src, 0)
    wi = w_in[0]
    pe_cols = jnp.take(wi[:, OFF_KPE:OFF_F], np.maximum(src - QK_NOPE, 0), axis=1) * rope
    w_in_p = jnp.concatenate([wi[:, OFF_Q:OFF_KV], pe_cols, wi[:, OFF_KV:OFF_KPE], wi[:, OFF_F:N_IN]], axis=1)
    w_in_p = w_in_p.astype(BF16)
    wq_p = jnp.take(w_q_up[0].reshape(Q_LORA, H, QK_DIM), lane_src, axis=2) * used
    wq_p = wq_p.reshape(Q_LORA, H * HEAD_SLOT).astype(BF16)
    wkv = w_kv_up[0].reshape(KV_LORA, H, QK_NOPE + V_DIM)
    wk_p = jnp.take(wkv[:, :, :QK_NOPE], np.minimum(lane_src, QK_NOPE - 1), axis=2) * nope
    wk_p = wk_p.reshape(KV_LORA, H * HEAD_SLOT).astype(BF16)
    wv_p = wkv[:, :, QK_NOPE:].reshape(KV_LORA, H * V_DIM).astype(BF16)
    qg_p = (jnp.take(q_norm_g[0], lane_src) * used * (QK_DIM ** -0.5 * LOG2E)).reshape(1, HEAD_SLOT)
    kg_p = (jnp.take(k_norm_g[0], lane_src) * used).reshape(1, HEAD_SLOT)
    bound = (QK_DIM * BOUND_SLACK) * jnp.max(jnp.abs(qg_p)) * jnp.max(jnp.abs(kg_p))
    on_shift = jnp.asarray(np.arange(HEAD_SLOT) == SHIFT_LANE, F32).reshape(1, HEAD_SLOT)
    qadd = on_shift * (-bound)
    kadd = on_shift
    qag = q_a_norm_g[0].reshape(1, Q_LORA)
    kvag = kv_a_norm_g[0].reshape(1, KV_LORA)
    g1n = norm1_g[0].reshape(1, D)
    g2n = norm2_g[0].reshape(1, D)
    cos_t, sin_t = _rope_tables(T)

    q, k_lat, vt_lat, f, ga, gb = _proj_lat_call(x, g1n, sc1, sh1, w_in_p, qag, kvag, wq_p, wk_p, wv_p,
                                                 qg_p, kg_p, qadd, kadd, cos_t, sin_t, tm=512)
    k_ctx, vt_ctx = _proj_ctx_call(ctx, g1n, csc1, csh1, w_in_p, kvag, wk_p, wv_p, kg_p, kadd)
    attn = _attn_call(q, k_lat, k_ctx, vt_lat, vt_ctx, tq=1024)
    four = _fourier_call(f, *_dft_tables(T))

    wr = jnp.pad(w_router[0], ((0, 0), (0, LANES - N_EXPERTS)))
    wrh = wr.astype(BF16)
    wrl = wrh + jnp.roll((wr - wrh.astype(F32)).astype(BF16), N_EXPERTS, axis=1)
    x1, h2t, aff = _merge_call(attn, four, ga, gb, x, g1, g2n, sc2, sh2, w_o_attn[0].astype(BF16),
                               w_fourier[0].astype(BF16), w_out[0].astype(BF16), wrh, wrl, tm=512)

    tri = jnp.asarray(np.triu(np.ones((LANES, LANES), np.float32)), BF16)
    addr = np.arange(T) * (D // LANES)
    pos = jnp.asarray(np.stack([addr // LANES, addr % LANES]).astype(np.float32), BF16)
    idx, gate = _select_call(aff, tri, pos, cap)
    y = _moe_call(idx, gate, h2t, w_e_gate[0].astype(BF16), w_e_up[0].astype(BF16),
                  w_e_down[0].astype(BF16), T)
    return _final_call(x1, y, g2, tm=1024)
```

```python
import functools

import numpy as np
import jax
import jax.numpy as jnp
from jax import lax
from jax.experimental import pallas as pl
from jax.experimental.pallas import tpu as pltpu

F32 = jnp.float32
BF16 = jnp.bfloat16

D_MODEL = 1024
GRID_W = 64
MLA_HEADS = 8
QK_NOPE = 64
QK_ROPE = 32
QK_DIM = QK_NOPE + QK_ROPE
V_DIM = 64
Q_LORA = 384
KV_LORA = 256
ROPE_THETA = 10000.0
F_GROUPS = 4
F_GROUP_DIM = 128
D_F = F_GROUPS * F_GROUP_DIM
OFF_Q = 0
OFF_KV = OFF_Q + Q_LORA
OFF_KPE = OFF_KV + KV_LORA
OFF_F = OFF_KPE + QK_ROPE
OFF_GA = OFF_F + D_F
OFF_GB = OFF_GA + D_MODEL
N_IN = OFF_GB + D_MODEL
N_EXPERTS = 16
D_EXPERT = 512
CAPACITY_FACTOR = 2
EPS = 1e-6

LANES = 128
SUBLANES = 8
MXU_TILE = 256
HEAD_SLOT = LANES
SHIFT_LANE = 48
BOUND_SLACK = 1.02
DEN_FLOOR = 2.0 ** -60
LOG2E = 1.4426950408889634
FFT_R = 8
FFT_N = 256
SEARCH_BITS = 3
SEARCH_DONE = 2.0 ** -30
SEARCH_MAX_ROUNDS = 56
KEY_CHUNK = 256
VMEM_LIMIT = 56 * 1024 * 1024

PC_Q = 0
PC_PE = PC_Q + Q_LORA
PC_KV = PC_PE + HEAD_SLOT
PC_F = PC_KV + KV_LORA
PC_GA = PC_F + D_F
PC_GB = PC_GA + D_MODEL
PC_END = PC_GB + D_MODEL


def _dot(a, b):
    return jnp.dot(a, b, preferred_element_type=F32)


def _dot_nt(a, b):
    return lax.dot_general(a, b, (((1,), (1,)), ((), ())), preferred_element_type=F32)


def _split2(a):
    hi = a.astype(BF16)
    lo = (a - hi.astype(F32)).astype(BF16)
    return hi, lo


def _split3(a):
    hi = a.astype(BF16)
    r = a - hi.astype(F32)
    mid = r.astype(BF16)
    lo = (r - mid.astype(F32)).astype(BF16)
    return hi, mid, lo


def _dot3(a, b):
    ah, al = _split2(a)
    bh, bl = _split2(b)
    return _dot(ah, bh) + (_dot(ah, bl) + _dot(al, bh))


def _sigmoid(x):
    return 1.0 / (1.0 + jnp.exp(-x))


def _silu(x):
    return x * (0.5 * jnp.tanh(0.5 * x) + 0.5)


def _params(sem):
    return pltpu.CompilerParams(dimension_semantics=sem, vmem_limit_bytes=VMEM_LIMIT)


def _mod_body(c_ref, w_ref, b_ref, o_ref):
    c = c_ref[...]
    s = c * _sigmoid(c)
    o_ref[...] = _dot3(s, w_ref[...]) + b_ref[...]


def _mod_call(cc, w_mod, b_mod):
    rows = cc.shape[0]
    n = w_mod.shape[1]
    tn = 1024
    return pl.pallas_call(
        _mod_body,
        grid=(n // tn,),
        in_specs=[
            pl.BlockSpec((rows, D_MODEL), lambda j: (0, 0)),
            pl.BlockSpec((D_MODEL, tn), lambda j: (0, j)),
            pl.BlockSpec((1, tn), lambda j: (0, j)),
        ],
        out_specs=pl.BlockSpec((rows, tn), lambda j: (0, j)),
        out_shape=jax.ShapeDtypeStruct((rows, n), F32),
        compiler_params=_params(("arbitrary",)),
        name="mod",
    )(cc, w_mod, b_mod)


def _rms_rows(x, n):
    return x * lax.rsqrt(jnp.sum(x * x, axis=-1, keepdims=True) * (1.0 / n) + EPS)


def _rope(x, cos, sin):
    return x * cos + pltpu.roll(x, LANES // 2, 1) * sin


def _proj_body(*refs, latent):
    if latent:
        (x_ref, g_ref, sc_ref, sh_ref, win_ref, qag_ref, kvag_ref, wq_ref, wk_ref, wv_ref,
         qg_ref, kg_ref, qadd_ref, kadd_ref, cos_ref, sin_ref,
         q_out, k_out, v_out, f_out, ga_out, gb_out) = refs
    else:
        (x_ref, g_ref, sc_ref, sh_ref, win_ref, kvag_ref, wk_ref, wv_ref, kg_ref, kadd_ref,
         k_out, v_out) = refs
    x = x_ref[0]
    h = _rms_rows(x, D_MODEL) * (g_ref[...] * (1.0 + sc_ref[0])) + sh_ref[0]
    hb = h.astype(BF16)

    if latent:
        cos = cos_ref[...]
        sin = sin_ref[...]

    pe = _dot(hb, win_ref[:, PC_PE:PC_KV])

    ckv = _rms_rows(_dot(hb, win_ref[:, PC_KV:PC_F]), KV_LORA) * kvag_ref[...]
    ckvb = ckv.astype(BF16)
    kall = _dot(ckvb, wk_ref[...])
    kg = kg_ref[...]
    kadd = kadd_ref[...]
    for hd in range(MLA_HEADS):
        kh = kall[:, hd * HEAD_SLOT:(hd + 1) * HEAD_SLOT] + pe
        kh = _rms_rows(kh, QK_DIM) * kg
        if latent:
            kh = _rope(kh, cos, sin)
        k_out[0, hd] = (kh + kadd).astype(BF16)
    v = _dot(ckvb, wv_ref[...])
    for hp in range(MLA_HEADS // 2):
        v_out[0, hp] = v[:, hp * LANES:(hp + 1) * LANES].T.astype(BF16)

    if latent:
        cq = _rms_rows(_dot(hb, win_ref[:, PC_Q:PC_PE]), Q_LORA) * qag_ref[...]
        qall = _dot(cq.astype(BF16), wq_ref[...])
        qg = qg_ref[...]
        qadd = qadd_ref[...]
        for hd in range(MLA_HEADS):
            qh = qall[:, hd * HEAD_SLOT:(hd + 1) * HEAD_SLOT]
            q_out[0, hd] = (_rope(_rms_rows(qh, QK_DIM) * qg, cos, sin) + qadd).astype(BF16)
        f_out[0] = _dot(hb, win_ref[:, PC_F:PC_GA]).astype(BF16)
        ga_out[0] = _sigmoid(_dot(hb, win_ref[:, PC_GA:PC_GB])).astype(BF16)
        gb_out[0] = _sigmoid(_dot(hb, win_ref[:, PC_GB:PC_END])).astype(BF16)


def _full(shape):
    nd = len(shape)
    return pl.BlockSpec(shape, lambda *_: (0,) * nd)


def _proj_lat_call(x, g1n, sc1, sh1, w_in_p, qag, kvag, wq_p, wk_p, wv_p, qg_p, kg_p, qadd, kadd, cos_t, sin_t, tm):
    B, T, _ = x.shape
    H = MLA_HEADS
    tok = lambda w: pl.BlockSpec((1, tm, w), lambda b, j: (b, j, 0))
    per_b = pl.BlockSpec((1, 1, D_MODEL), lambda b, j: (b, 0, 0))
    tab = pl.BlockSpec((tm, HEAD_SLOT), lambda b, j: (j, 0))
    return pl.pallas_call(
        functools.partial(_proj_body, latent=True),
        grid=(B, T // tm),
        in_specs=[tok(D_MODEL), _full(g1n.shape), per_b, per_b, _full(w_in_p.shape), _full(qag.shape),
                  _full(kvag.shape), _full(wq_p.shape), _full(wk_p.shape), _full(wv_p.shape),
                  _full(qg_p.shape), _full(kg_p.shape), _full(qadd.shape), _full(kadd.shape), tab, tab],
        out_specs=[
            pl.BlockSpec((1, H, tm, HEAD_SLOT), lambda b, j: (b, 0, j, 0)),
            pl.BlockSpec((1, H, tm, HEAD_SLOT), lambda b, j: (b, 0, j, 0)),
            pl.BlockSpec((1, H // 2, LANES, tm), lambda b, j: (b, 0, 0, j)),
            tok(D_F), tok(D_MODEL), tok(D_MODEL),
        ],
        out_shape=[
            jax.ShapeDtypeStruct((B, H, T, HEAD_SLOT), BF16),
            jax.ShapeDtypeStruct((B, H, T, HEAD_SLOT), BF16),
            jax.ShapeDtypeStruct((B, H // 2, LANES, T), BF16),
            jax.ShapeDtypeStruct((B, T, D_F), BF16),
            jax.ShapeDtypeStruct((B, T, D_MODEL), BF16),
            jax.ShapeDtypeStruct((B, T, D_MODEL), BF16),
        ],
        compiler_params=_params(("parallel", "parallel")),
        name="proj_lat",
    )(x, g1n, sc1, sh1, w_in_p, qag, kvag, wq_p, wk_p, wv_p, qg_p, kg_p, qadd, kadd, cos_t, sin_t)


def _proj_ctx_call(ctx, g1n, csc1, csh1, w_in_p, kvag, wk_p, wv_p, kg_p, kadd):
    B, TC, _ = ctx.shape
    H = MLA_HEADS
    shared = pl.BlockSpec((1, 1, D_MODEL), lambda b: (0, 0, 0))
    return pl.pallas_call(
        functools.partial(_proj_body, latent=False),
        grid=(B,),
        in_specs=[pl.BlockSpec((1, TC, D_MODEL), lambda b: (b, 0, 0)), _full(g1n.shape), shared, shared,
                  _full(w_in_p.shape), _full(kvag.shape), _full(wk_p.shape), _full(wv_p.shape),
                  _full(kg_p.shape), _full(kadd.shape)],
        out_specs=[
            pl.BlockSpec((1, H, TC, HEAD_SLOT), lambda b: (b, 0, 0, 0)),
            pl.BlockSpec((1, H // 2, LANES, TC), lambda b: (b, 0, 0, 0)),
        ],
        out_shape=[
            jax.ShapeDtypeStruct((B, H, TC, HEAD_SLOT), BF16),
            jax.ShapeDtypeStruct((B, H // 2, LANES, TC), BF16),
        ],
        compiler_params=_params(("parallel",)),
        name="proj_ctx",
    )(ctx, g1n, csc1, csh1, w_in_p, kvag, wk_p, wv_p, kg_p, kadd)


def _attn_body(q_ref, kl_ref, kc_ref, vl_ref, vc_ref, o_ref):
    tq = q_ref.shape[2]
    T = kl_ref.shape[2]
    nck = T // KEY_CHUNK
    row = lax.broadcasted_iota(jnp.int32, (LANES, tq), 0)

    def probs(hd):
        q = q_ref[0, hd]
        den = jnp.zeros((1, tq), F32)
        ps = []
        for c in range(nck + 1):
            k = kc_ref[0, hd] if c == nck else kl_ref[0, hd, c * KEY_CHUNK:(c + 1) * KEY_CHUNK, :]
            e = jnp.exp2(_dot_nt(k, q))
            den = den + jnp.sum(e, axis=0, keepdims=True)
            ps.append(e.astype(BF16))
        return ps, den

    def weighted_values(hd, ps):
        hp = hd // 2
        o = _dot(vc_ref[0, hp], ps[nck])
        for c in range(nck):
            o = o + _dot(vl_ref[0, hp, :, c * KEY_CHUNK:(c + 1) * KEY_CHUNK], ps[c])
        return o

    outs, dens, prev = [], [], None
    for hd in range(MLA_HEADS + 1):
        cur = probs(hd) if hd < MLA_HEADS else None
        if prev is not None:
            outs.append(weighted_values(hd - 1, prev[0]) / prev[1])
            dens.append(prev[1])
        prev = cur
    for hp in range(MLA_HEADS // 2):
        o_ref[0, hp] = jnp.where(row < V_DIM, outs[2 * hp], outs[2 * hp + 1]).astype(BF16)
    min_den = functools.reduce(jnp.minimum, dens)

    @pl.when(jnp.min(min_den) < DEN_FLOOR)
    def _():
        def pair(hp, carry):
            res = []
            for hd in (2 * hp, 2 * hp + 1):
                q = q_ref[0, hd]
                sl = _dot_nt(kl_ref[0, hd], q)
                sc = _dot_nt(kc_ref[0, hd], q)
                m = jnp.maximum(jnp.max(sl, axis=0, keepdims=True), jnp.max(sc, axis=0, keepdims=True))
                el = jnp.exp2(sl - m)
                ec = jnp.exp2(sc - m)
                den = jnp.sum(el, axis=0, keepdims=True) + jnp.sum(ec, axis=0, keepdims=True)
                o = _dot(vl_ref[0, hp], el.astype(BF16)) + _dot(vc_ref[0, hp], ec.astype(BF16))
                res.append(o / den)
            o_ref[0, hp] = jnp.where(row < V_DIM, res[0], res[1]).astype(BF16)
            return carry

        lax.fori_loop(0, MLA_HEADS // 2, pair, 0)


def _attn_call(q, k_lat, k_ctx, vt_lat, vt_ctx, tq):
    B, H, T, _ = q.shape
    TC = k_ctx.shape[2]
    return pl.pallas_call(
        _attn_body,
        grid=(B, T // tq),
        in_specs=[
            pl.BlockSpec((1, H, tq, HEAD_SLOT), lambda b, j: (b, 0, j, 0)),
            pl.BlockSpec((1, H, T, HEAD_SLOT), lambda b, j: (b, 0, 0, 0)),
            pl.BlockSpec((1, H, TC, HEAD_SLOT), lambda b, j: (b, 0, 0, 0)),
            pl.BlockSpec((1, H // 2, LANES, T), lambda b, j: (b, 0, 0, 0)),
            pl.BlockSpec((1, H // 2, LANES, TC), lambda b, j: (b, 0, 0, 0)),
        ],
        out_specs=pl.BlockSpec((1, H // 2, LANES, tq), lambda b, j: (b, 0, 0, j)),
        out_shape=jax.ShapeDtypeStruct((B, H // 2, LANES, T), BF16),
        compiler_params=_params(("parallel", "parallel")),
        name="attn",
    )(q, k_lat, k_ctx, vt_lat, vt_ctx)


def _cadd(a, b):
    return a[0] + b[0], a[1] + b[1]


def _csub(a, b):
    return a[0] - b[0], a[1] - b[1]


def _cmul_neg_i(a):
    return a[1], -a[0]


def _cmul_pos_i(a):
    return -a[1], a[0]


def _fft8(u):
    r = 0.7071067811865476
    a0, a1 = _cadd(u[0], u[4]), _csub(u[0], u[4])
    a2, a3 = _cadd(u[2], u[6]), _csub(u[2], u[6])
    a4, a5 = _cadd(u[1], u[5]), _csub(u[1], u[5])
    a6, a7 = _cadd(u[3], u[7]), _csub(u[3], u[7])
    e0, e2 = _cadd(a0, a2), _csub(a0, a2)
    e1, e3 = _cadd(a1, _cmul_neg_i(a3)), _cadd(a1, _cmul_pos_i(a3))
    o0, o2 = _cadd(a4, a6), _csub(a4, a6)
    o1, o3 = _cadd(a5, _cmul_neg_i(a7)), _cadd(a5, _cmul_pos_i(a7))
    wo1 = (r * (o1[0] + o1[1]), r * (o1[1] - o1[0]))
    wo3 = (r * (o3[1] - o3[0]), -r * (o3[0] + o3[1]))
    return [_cadd(e0, o0), _cadd(e1, wo1), _cadd(e2, _cmul_neg_i(o2)), _cadd(e3, wo3),
            _csub(e0, o0), _csub(e1, wo1), _cadd(e2, _cmul_pos_i(o2)), _csub(e3, wo3)]


def _fourier_body(f_ref, cc_ref, twc_ref, tws_ref, m2_ref, o_ref, b_ref, y_ref):
    for g in range(F_GROUPS):
        cols = slice(g * F_GROUP_DIM, (g + 1) * F_GROUP_DIM)
        y = _dot(f_ref[0, :, cols], cc_ref[...])
        u = [(y[j * FFT_N:(j + 1) * FFT_N, :F_GROUP_DIM], y[j * FFT_N:(j + 1) * FFT_N, F_GROUP_DIM:])
             for j in range(FFT_R)]
        a = _fft8(u)
        for k1 in range(FFT_R):
            ar, ai = a[k1]
            if k1 > 0:
                c = twc_ref[k1 * FFT_N:(k1 + 1) * FFT_N, :]
                s = tws_ref[k1 * FFT_N:(k1 + 1) * FFT_N, :]
                ar, ai = ar * c + ai * s, ai * c - ar * s
            b_ref[k1, 0:FFT_N, cols] = ar.astype(BF16)
            b_ref[k1, FFT_N:2 * FFT_N, cols] = ai.astype(BF16)
    for k1 in range(FFT_R):
        y = _dot(m2_ref[...], b_ref[k1])
        for g in range(F_GROUPS):
            y_ref[g, pl.ds(k1, FFT_N, stride=FFT_R), :] = y[:, g * F_GROUP_DIM:(g + 1) * F_GROUP_DIM]
    for g in range(F_GROUPS):
        o_ref[0, :, g * F_GROUP_DIM:(g + 1) * F_GROUP_DIM] = y_ref[g].astype(BF16)


def _fourier_call(f, cc, twc, tws, m2):
    B, T, _ = f.shape
    return pl.pallas_call(
        _fourier_body,
        grid=(B,),
        in_specs=[pl.BlockSpec((1, T, D_F), lambda b: (b, 0, 0)), _full(cc.shape), _full(twc.shape),
                  _full(tws.shape), _full(m2.shape)],
        out_specs=pl.BlockSpec((1, T, D_F), lambda b: (b, 0, 0)),
        out_shape=jax.ShapeDtypeStruct((B, T, D_F), BF16),
        scratch_shapes=[pltpu.VMEM((FFT_R, 2 * FFT_N, D_F), BF16), pltpu.VMEM((F_GROUPS, T, F_GROUP_DIM), F32)],
        compiler_params=_params(("parallel",)),
        name="fourier",
    )(f, cc, twc, tws, m2)


def _merge_body(at_ref, fo_ref, ga_ref, gb_ref, x_ref, g1_ref, g2n_ref, sc2_ref, sh2_ref,
                wo_ref, wf_ref, wout_ref, wrh_ref, wrl_ref, x1_out, h2_out, aff_out):
    tm = x_ref.shape[1]
    attn_t = jnp.concatenate([at_ref[0, hp] for hp in range(MLA_HEADS // 2)], axis=0)
    a = lax.dot_general(attn_t, wo_ref[...], (((0,), (0,)), ((), ())), preferred_element_type=F32)
    fo = _dot(fo_ref[0], wf_ref[...])
    mix = ga_ref[0].astype(F32) * a + gb_ref[0].astype(F32) * fo
    y = _dot(mix.astype(BF16), wout_ref[...])
    x1 = x_ref[0] + g1_ref[0] * y
    x1_out[0] = x1
    h2 = _rms_rows(x1, D_MODEL) * g2n_ref[...]
    h2 = h2 * (1.0 + sc2_ref[0]) + sh2_ref[0]
    for c in range(D_MODEL // LANES):
        h2_out[0, pl.ds(c, tm, stride=D_MODEL // LANES), :] = h2[:, c * LANES:(c + 1) * LANES]
    hh, hl = _split2(h2)
    p_hi = _dot(hh, wrl_ref[...])
    logits = p_hi + pltpu.roll(p_hi, LANES - N_EXPERTS, 1) + _dot(hl, wrh_ref[...])
    lane = lax.broadcasted_iota(jnp.int32, logits.shape, 1)
    logits = jnp.where(lane < N_EXPERTS, logits, -1e30)
    ex = jnp.exp(logits - jnp.max(logits, axis=-1, keepdims=True))
    aff_out[0] = ex / jnp.sum(ex, axis=-1, keepdims=True)


def _merge_call(attn, four, ga, gb, x, g1, g2n, sc2, sh2, wo, wf, wout, wrh, wrl, tm):
    B, T, _ = x.shape
    H = MLA_HEADS
    tok = lambda w: pl.BlockSpec((1, tm, w), lambda b, j: (b, j, 0))
    per_b = pl.BlockSpec((1, 1, D_MODEL), lambda b, j: (b, 0, 0))
    nchunk = D_MODEL // LANES
    return pl.pallas_call(
        _merge_body,
        grid=(B, T // tm),
        in_specs=[
            pl.BlockSpec((1, H // 2, LANES, tm), lambda b, j: (b, 0, 0, j)),
            tok(D_F), tok(D_MODEL), tok(D_MODEL), tok(D_MODEL), per_b, _full(g2n.shape), per_b, per_b,
            _full(wo.shape), _full(wf.shape), _full(wout.shape), _full(wrh.shape), _full(wrl.shape),
        ],
        out_specs=[
            tok(D_MODEL),
            pl.BlockSpec((1, tm * nchunk, LANES), lambda b, j: (b, j, 0)),
            tok(LANES),
        ],
        out_shape=[
            jax.ShapeDtypeStruct((B, T, D_MODEL), F32),
            jax.ShapeDtypeStruct((B, T * nchunk, LANES), F32),
            jax.ShapeDtypeStruct((B, T, LANES), F32),
        ],
        compiler_params=_params(("parallel", "parallel")),
        name="merge",
    )(attn, four, ga, gb, x, g1, g2n, sc2, sh2, wo, wf, wout, wrh, wrl)


def _prefix_excl(m, tri):
    outs = []
    run = jnp.zeros((m.shape[0], 1), F32)
    for blk in range(m.shape[1] // LANES):
        mb = m[:, blk * LANES:(blk + 1) * LANES]
        inc = _dot(mb.astype(BF16), tri)
        outs.append(inc - mb + run)
        run = run + inc[:, LANES - 1:LANES]
    return jnp.concatenate(outs, axis=1)


def _select_body(aff_ref, tri_ref, pos_ref, idx_out, gate_out, *, cap):
    T = aff_ref.shape[1]
    aff = aff_ref[0].T[0:N_EXPERTS]

    def search(carry):
        base, step, rounds, _ = carry
        thr = base
        for j in range(1, 2 ** SEARCH_BITS):
            cand = base + float(j) * step
            cnt = jnp.sum((aff >= cand).astype(F32), axis=-1, keepdims=True)
            thr = jnp.where(cnt >= float(cap), cand, thr)
        settled = jnp.where((thr > 0.0) & (step < thr * SEARCH_DONE), 1.0, 0.0)
        return thr, step * (0.5 ** SEARCH_BITS), rounds + 1, (jnp.min(settled) < 1.0).astype(jnp.int32)

    init = (jnp.zeros((N_EXPERTS, 1), F32), jnp.full((N_EXPERTS, 1), 2.0 * 0.5 ** SEARCH_BITS, F32),
            jnp.int32(0), jnp.int32(1))
    thr = lax.while_loop(lambda c: (c[3] > 0) & (c[2] < SEARCH_MAX_ROUNDS), search, init)[0]
    gt = (aff > thr).astype(F32)
    eq = (aff == thr).astype(F32)
    need = cap - jnp.sum(gt, axis=-1, keepdims=True)
    tri = tri_ref[...]
    sel = gt + eq * (_prefix_excl(eq, tri) < need).astype(F32)
    slot = jnp.where(sel > 0.0, _prefix_excl(sel, tri), -1.0)

    pos = pos_ref[...]
    srow = lax.broadcasted_iota(jnp.int32, (cap, T), 0).astype(F32)
    zero = jnp.zeros((3, T), BF16)
    for e in range(N_EXPERTS):
        onehot = jnp.where(srow == slot[e:e + 1, :], 1.0, 0.0).astype(BF16)
        ah, am, al = _split3(aff[e:e + 1, :])
        vals = jnp.concatenate([pos, ah, am, al, zero], axis=0)
        res = _dot_nt(vals, onehot)
        idx_out[0, e:e + 1, :] = (res[0:1] * float(LANES) + res[1:2]).astype(jnp.int32)
        gate_out[0, e:e + 1, :] = res[2:3] + (res[3:4] + res[4:5])


def _select_call(aff, tri, pos, cap):
    B, T, _ = aff.shape
    return pl.pallas_call(
        functools.partial(_select_body, cap=cap),
        grid=(B,),
        in_specs=[pl.BlockSpec((1, T, LANES), lambda b: (b, 0, 0)), _full(tri.shape), _full(pos.shape)],
        out_specs=[pl.BlockSpec((1, N_EXPERTS, cap), lambda b: (b, 0, 0)),
                   pl.BlockSpec((1, N_EXPERTS, cap), lambda b: (b, 0, 0))],
        out_shape=[jax.ShapeDtypeStruct((B, N_EXPERTS, cap), jnp.int32),
                   jax.ShapeDtypeStruct((B, N_EXPERTS, cap), F32)],
        compiler_params=_params(("parallel",)),
        name="select",
    )(aff, tri, pos)


def _moe_body(idx_ref, gate_ref, h2_ref, wg_ref, wu_ref, wd_ref, o_ref, acc_ref, xg_a, xg_b, ye_a, ye_b,
              *, cap, stride):
    k = pl.program_id(1)
    nchunk = D_MODEL // LANES
    group = 8

    def zero_of(tile):
        bits = pltpu.bitcast(tile, jnp.uint32)
        return pltpu.bitcast(lax.shift_right_logical(bits, jnp.uint32(32)), F32)

    def gather(e, xg_ref):
        def rows(s0, hold):
            for s in range(s0, s0 + group):
                base = pl.multiple_of(idx_ref[0, e, 0, s], SUBLANES)
                tile = h2_ref[0, pl.ds(base, nchunk), :] + hold
                xg_ref[pl.ds(s, nchunk, stride=stride), :] = tile
            return tile
        return [functools.partial(rows, s0) for s0 in range(0, cap, group)]

    def scatter(e, ye_ref):
        def rows(s0, hold):
            bases, news = [], []
            for s in range(s0, s0 + group):
                base = pl.multiple_of(idx_ref[0, e, 0, s], SUBLANES)
                bases.append(base)
                news.append(acc_ref[pl.ds(base, nchunk), :]
                            + (ye_ref[pl.ds(s, nchunk, stride=stride), :] + hold))
            for base, new in zip(bases, news):
                acc_ref[pl.ds(base, nchunk), :] = new
            return news[-1]
        return [functools.partial(rows, s0) for s0 in range(0, cap, group)]

    def expert(j, e, xg_ref, ye_ref):
        state = {}
        n_mid = D_EXPERT // MXU_TILE
        n_out = D_MODEL // MXU_TILE

        def load(zero):
            xb = jnp.concatenate([xg_ref[c * stride:c * stride + cap, :] for c in range(nchunk)], axis=1)
            state["x"] = xb.astype(BF16)
            state["gate"] = jnp.broadcast_to(gate_ref[0, e], (LANES, cap)).T[:, 0:1]
            state["h"] = []
            return xb[0:SUBLANES, 0:LANES]

        def mid(n, zero):
            cols = slice(n * MXU_TILE, (n + 1) * MXU_TILE)
            a = _dot(state["x"], wg_ref[j, :, cols]) + zero
            u = _dot(state["x"], wu_ref[j, :, cols])
            h = _silu(a) * u
            state["h"].append(h.astype(BF16))
            return h[0:SUBLANES, 0:LANES]

        def out(n, zero):
            cols = slice(n * MXU_TILE, (n + 1) * MXU_TILE)
            ye = _dot(state["h"][0], wd_ref[j, 0:MXU_TILE, cols])
            for m in range(1, n_mid):
                ye = ye + _dot(state["h"][m], wd_ref[j, m * MXU_TILE:(m + 1) * MXU_TILE, cols])
            ye = ye * state["gate"] + zero
            for c in range(MXU_TILE // LANES):
                cc = n * (MXU_TILE // LANES) + c
                ye_ref[cc * stride:cc * stride + cap, :] = ye[:, c * LANES:(c + 1) * LANES]
            return ye[0:SUBLANES, 0:LANES]

        return ([load] + [functools.partial(mid, n) for n in range(n_mid)]
                + [functools.partial(out, n) for n in range(n_out)])

    def paced(main, side):
        none = jnp.zeros((SUBLANES, LANES), F32)
        hold = none
        edges = [none, none, none]
        done = 0
        for i, piece in enumerate(main):
            zero = jnp.concatenate([edges[0][0:1, :]] * (MXU_TILE // LANES), axis=1)
            witness = piece(zero)
            upto = (len(side) * (i + 1)) // len(main)
            tiles = [s(hold) for s in side[done:upto]]
            done = upto
            hold = zero_of(witness)
            edges = edges[1:] + [zero_of(functools.reduce(jnp.add, tiles)) if tiles else none]

    def zipped(a, b):
        return [f for pair in zip(a, b) for f in pair]

    @pl.when(k == 0)
    def _():
        acc_ref[...] = jnp.zeros_like(acc_ref)
        ye_b[...] = jnp.zeros_like(ye_b)
        for piece in gather(0, xg_a):
            piece(jnp.zeros((SUBLANES, LANES), F32))

    n_experts = 2 * pl.num_programs(1)
    e0 = 2 * k
    e_prev = jnp.maximum(e0 - 1, 0)
    e_next = jnp.minimum(e0 + 2, n_experts - 1)
    paced(expert(0, e0, xg_a, ye_a), zipped(scatter(e_prev, ye_b), gather(e0 + 1, xg_b)))
    paced(expert(1, e0 + 1, xg_b, ye_b), zipped(scatter(e0, ye_a), gather(e_next, xg_a)))

    @pl.when(k == pl.num_programs(1) - 1)
    def _():
        for piece in scatter(n_experts - 1, ye_b):
            piece(jnp.zeros((SUBLANES, LANES), F32))
        o_ref[0] = acc_ref[...].astype(BF16)


def _moe_call(idx, gate, h2t, wg, wu, wd, T):
    B, E, cap = idx.shape
    nchunk = D_MODEL // LANES
    stride = cap + SUBLANES
    buf = pltpu.VMEM((nchunk * stride, LANES), F32)
    idx4 = idx.reshape(B, E, 1, cap)
    gate4 = gate.reshape(B, E, 1, cap)

    return pl.pallas_call(
        functools.partial(_moe_body, cap=cap, stride=stride),
        grid=(B, E // 2),
        in_specs=[
            pl.BlockSpec((1, E, 1, cap), lambda b, k: (b, 0, 0, 0), memory_space=pltpu.SMEM),
            pl.BlockSpec((1, E, 1, cap), lambda b, k: (b, 0, 0, 0)),
            pl.BlockSpec((1, T * nchunk, LANES), lambda b, k: (b, 0, 0)),
            pl.BlockSpec((2, D_MODEL, D_EXPERT), lambda b, k: (k, 0, 0)),
            pl.BlockSpec((2, D_MODEL, D_EXPERT), lambda b, k: (k, 0, 0)),
            pl.BlockSpec((2, D_EXPERT, D_MODEL), lambda b, k: (k, 0, 0)),
        ],
        out_specs=pl.BlockSpec((1, T * nchunk, LANES), lambda b, k: (b, 0, 0)),
        out_shape=jax.ShapeDtypeStruct((B, T * nchunk, LANES), BF16),
        scratch_shapes=[pltpu.VMEM((T * nchunk, LANES), F32), buf, buf, buf, buf],
        compiler_params=_params(("parallel", "arbitrary")),
        name="moe",
    )(idx4, gate4, h2t, wg, wu, wd)


def _final_body(x1_ref, y_ref, g2_ref, o_ref, y32_ref):
    tm = x1_ref.shape[1]
    y32_ref[...] = y_ref[0].astype(F32)
    for c in range(D_MODEL // LANES):
        cols = slice(c * LANES, (c + 1) * LANES)
        y = y32_ref[pl.ds(c, tm, stride=D_MODEL // LANES), :]
        o_ref[0, :, cols] = x1_ref[0, :, cols] + g2_ref[0, :, cols] * y


def _final_call(x1, y, g2, tm):
    B, T, _ = x1.shape
    nchunk = D_MODEL // LANES
    tok = pl.BlockSpec((1, tm, D_MODEL), lambda b, j: (b, j, 0))
    return pl.pallas_call(
        _final_body,
        grid=(B, T // tm),
        in_specs=[tok,
                  pl.BlockSpec((1, tm * nchunk, LANES), lambda b, j: (b, j, 0)),
                  pl.BlockSpec((1, 1, D_MODEL), lambda b, j: (b, 0, 0))],
        out_specs=tok,
        out_shape=jax.ShapeDtypeStruct((B, T, D_MODEL), F32),
        scratch_shapes=[pltpu.VMEM((tm * nchunk, LANES), F32)],
        compiler_params=_params(("parallel", "parallel")),
        name="final",
    )(x1, y, g2)


def _slot_layout():
    half = QK_ROPE // 2
    n_freq = half // 2
    src = np.full((HEAD_SLOT,), -1, np.int64)
    for axis in range(2):
        first = QK_NOPE + axis * half
        src[axis * n_freq:(axis + 1) * n_freq] = np.arange(first, first + n_freq)
        src[LANES // 2 + axis * n_freq:LANES // 2 + (axis + 1) * n_freq] = np.arange(first + n_freq, first + half)
    src[half:half + QK_NOPE // 2] = np.arange(0, QK_NOPE // 2)
    src[LANES // 2 + half:LANES // 2 + half + QK_NOPE // 2] = np.arange(QK_NOPE // 2, QK_NOPE)
    assert src[SHIFT_LANE] == -1 and sorted(src[src >= 0]) == list(range(QK_DIM))
    return src


_SLOT_SRC = _slot_layout()


def _rope_tables(T):
    half = QK_ROPE // 2
    n_freq = half // 2
    inv_freq = 1.0 / (ROPE_THETA ** (np.arange(n_freq, dtype=np.float32) / n_freq))
    t = np.arange(T)
    cos = np.ones((T, HEAD_SLOT), np.float32)
    sin = np.zeros((T, HEAD_SLOT), np.float32)
    for axis, pos in enumerate((t // GRID_W, t % GRID_W)):
        ang = pos.astype(np.float32)[:, None] * inv_freq[None, :].astype(np.float32)
        c, s = np.cos(ang).astype(np.float32), np.sin(ang).astype(np.float32)
        lo1 = axis * n_freq
        lo2 = LANES // 2 + axis * n_freq
        cos[:, lo1:lo1 + n_freq] = c
        cos[:, lo2:lo2 + n_freq] = c
        sin[:, lo1:lo1 + n_freq] = -s
        sin[:, lo2:lo2 + n_freq] = s
    return jnp.asarray(cos), jnp.asarray(sin)


def _dft_tables(T):
    assert T == FFT_R * FFT_N
    n = np.arange(F_GROUP_DIM)
    ang_c = 2.0 * np.pi * ((n[:, None] * n[None, :]) % F_GROUP_DIM) / F_GROUP_DIM
    cc = np.concatenate([np.cos(ang_c), -np.sin(ang_c)], axis=1) / np.sqrt(F_GROUP_DIM)
    k1 = np.repeat(np.arange(FFT_R), FFT_N)
    n2 = np.tile(np.arange(FFT_N), FFT_R)
    ang_w = 2.0 * np.pi * (k1 * n2) / T
    twc = np.broadcast_to(np.cos(ang_w)[:, None], (T, F_GROUP_DIM))
    tws = np.broadcast_to(np.sin(ang_w)[:, None], (T, F_GROUP_DIM))
    m = np.arange(FFT_N)
    ang_2 = 2.0 * np.pi * ((m[:, None] * m[None, :]) % FFT_N) / FFT_N
    m2 = np.concatenate([np.cos(ang_2), np.sin(ang_2)], axis=1) / np.sqrt(T)
    return (jnp.asarray(cc, F32).astype(BF16), jnp.asarray(twc, F32), jnp.asarray(tws, F32),
            jnp.asarray(m2, F32).astype(BF16))


def kernel(x, c, ctx, c_ctx, w_mod, b_mod, norm1_g, w_in, q_a_norm_g, kv_a_norm_g, w_q_up, w_kv_up,
           q_norm_g, k_norm_g, w_o_attn, w_fourier, w_out, norm2_g, w_router, w_e_gate, w_e_up, w_e_down):
    B, T, D = x.shape
    assert w_mod.shape[0] == 1 and D == D_MODEL and T % 1024 == 0
    H = MLA_HEADS
    cap = (CAPACITY_FACTOR * T) // N_EXPERTS

    rows = -(-(B + 1) // SUBLANES) * SUBLANES
    cc_in = jnp.concatenate([c, c_ctx[None, :], jnp.zeros((rows - B - 1, D), F32)], axis=0)
    mod = _mod_call(cc_in, w_mod[0], b_mod)
    sh1, sc1, g1, sh2, sc2, g2 = [mod[:B, i * D:(i + 1) * D].reshape(B, 1, D) for i in range(6)]
    csh1 = mod[B:B + 1, 0:D].reshape(1, 1, D)
    csc1 = mod[B:B + 1, D:2 * D].reshape(1, 1, D)

    src = _SLOT_SRC
    used = jnp.asarray(src >= 0, F32)
    nope = jnp.asarray((src >= 0) & (src < QK_NOPE), F32)
    rope = jnp.asarray(src >= QK_NOPE, F32)
    lane_src = np.maximum(src, 0)
    wi = w_in[0]
    pe_cols = jnp.take(wi[:, OFF_KPE:OFF_F], np.maximum(src - QK_NOPE, 0), axis=1) * rope
    w_in_p = jnp.concatenate([wi[:, OFF_Q:OFF_KV], pe_cols, wi[:, OFF_KV:OFF_KPE], wi[:, OFF_F:N_IN]], axis=1)
    w_in_p = w_in_p.astype(BF16)
    wq_p = jnp.take(w_q_up[0].reshape(Q_LORA, H, QK_DIM), lane_src, axis=2) * used
    wq_p = wq_p.reshape(Q_LORA, H * HEAD_SLOT).astype(BF16)
    wkv = w_kv_up[0].reshape(KV_LORA, H, QK_NOPE + V_DIM)
    wk_p = jnp.take(wkv[:, :, :QK_NOPE], np.minimum(lane_src, QK_NOPE - 1), axis=2) * nope
    wk_p = wk_p.reshape(KV_LORA, H * HEAD_SLOT).astype(BF16)
    wv_p = wkv[:, :, QK_NOPE:].reshape(KV_LORA, H * V_DIM).astype(BF16)
    qg_p = (jnp.take(q_norm_g[0], lane_src) * used * (QK_DIM ** -0.5 * LOG2E)).reshape(1, HEAD_SLOT)
    kg_p = (jnp.take(k_norm_g[0], lane_src) * used).reshape(1, HEAD_SLOT)
    bound = (QK_DIM * BOUND_SLACK) * jnp.max(jnp.abs(qg_p)) * jnp.max(jnp.abs(kg_p))
    on_shift = jnp.asarray(np.arange(HEAD_SLOT) == SHIFT_LANE, F32).reshape(1, HEAD_SLOT)
    qadd = on_shift * (-bound)
    kadd = on_shift
    qag = q_a_norm_g[0].reshape(1, Q_LORA)
    kvag = kv_a_norm_g[0].reshape(1, KV_LORA)
    g1n = norm1_g[0].reshape(1, D)
    g2n = norm2_g[0].reshape(1, D)
    cos_t, sin_t = _rope_tables(T)

    q, k_lat, vt_lat, f, ga, gb = _proj_lat_call(x, g1n, sc1, sh1, w_in_p, qag, kvag, wq_p, wk_p, wv_p,
                                                 qg_p, kg_p, qadd, kadd, cos_t, sin_t, tm=512)
    k_ctx, vt_ctx = _proj_ctx_call(ctx, g1n, csc1, csh1, w_in_p, kvag, wk_p, wv_p, kg_p, kadd)
    attn = _attn_call(q, k_lat, k_ctx, vt_lat, vt_ctx, tq=1024)
    four = _fourier_call(f, *_dft_tables(T))

    wr = jnp.pad(w_router[0], ((0, 0), (0, LANES - N_EXPERTS)))
    wrh = wr.astype(BF16)
    wrl = wrh + jnp.roll((wr - wrh.astype(F32)).astype(BF16), N_EXPERTS, axis=1)
    x1, h2t, aff = _merge_call(attn, four, ga, gb, x, g1, g2n, sc2, sh2, w_o_attn[0].astype(BF16),
                               w_fourier[0].astype(BF16), w_out[0].astype(BF16), wrh, wrl, tm=512)

    tri = jnp.asarray(np.triu(np.ones((LANES, LANES), np.float32)), BF16)
    addr = np.arange(T) * (D // LANES)
    pos = jnp.asarray(np.stack([addr // LANES, addr % LANES]).astype(np.float32), BF16)
    idx, gate = _select_call(aff, tri, pos, cap)
    y = _moe_call(idx, gate, h2t, w_e_gate[0].astype(BF16), w_e_up[0].astype(BF16),
                  w_e_down[0].astype(BF16), T)
    return _final_call(x1, y, g2, tm=1024)
```

```python
import functools

import numpy as np
import jax
import jax.numpy as jnp
from jax import lax
from jax.experimental import pallas as pl
from jax.experimental.pallas import tpu as pltpu

F32 = jnp.float32
BF16 = jnp.bfloat16

D_MODEL = 1024
GRID_W = 64
MLA_HEADS = 8
QK_NOPE = 64
QK_ROPE = 32
QK_DIM = QK_NOPE + QK_ROPE
V_DIM = 64
Q_LORA = 384
KV_LORA = 256
ROPE_THETA = 10000.0
F_GROUPS = 4
F_GROUP_DIM = 128
D_F = F_GROUPS * F_GROUP_DIM
OFF_Q = 0
OFF_KV = OFF_Q + Q_LORA
OFF_KPE = OFF_KV + KV_LORA
OFF_F = OFF_KPE + QK_ROPE
OFF_GA = OFF_F + D_F
OFF_GB = OFF_GA + D_MODEL
N_IN = OFF_GB + D_MODEL
N_EXPERTS = 16
D_EXPERT = 512
CAPACITY_FACTOR = 2
EPS = 1e-6

LANES = 128
SUBLANES = 8
MXU_TILE = 256
HEAD_SLOT = LANES
SHIFT_LANE = 48
BOUND_SLACK = 1.02
DEN_FLOOR = 2.0 ** -60
LOG2E = 1.4426950408889634
FFT_R = 8
FFT_N = 256
SEARCH_BITS = 3
SEARCH_DONE = 2.0 ** -30
SEARCH_MAX_ROUNDS = 56
KEY_CHUNK = 256
VMEM_LIMIT = 56 * 1024 * 1024

PC_Q = 0
PC_PE = PC_Q + Q_LORA
PC_KV = PC_PE + HEAD_SLOT
PC_F = PC_KV + KV_LORA
PC_GA = PC_F + D_F
PC_GB = PC_GA + D_MODEL
PC_END = PC_GB + D_MODEL


def _dot(a, b):
    return jnp.dot(a, b, preferred_element_type=F32)


def _dot_nt(a, b):
    return lax.dot_general(a, b, (((1,), (1,)), ((), ())), preferred_element_type=F32)


def _split2(a):
    hi = a.astype(BF16)
    lo = (a - hi.astype(F32)).astype(BF16)
    return hi, lo


def _split3(a):
    hi = a.astype(BF16)
    r = a - hi.astype(F32)
    mid = r.astype(BF16)
    lo = (r - mid.astype(F32)).astype(BF16)
    return hi, mid, lo


def _dot3(a, b):
    ah, al = _split2(a)
    bh, bl = _split2(b)
    return _dot(ah, bh) + (_dot(ah, bl) + _dot(al, bh))


def _sigmoid(x):
    return 1.0 / (1.0 + jnp.exp(-x))


def _silu(x):
    return x * (0.5 * jnp.tanh(0.5 * x) + 0.5)


def _params(sem):
    return pltpu.CompilerParams(dimension_semantics=sem, vmem_limit_bytes=VMEM_LIMIT)


def _mod_body(c_ref, w_ref, b_ref, o_ref):
    c = c_ref[...]
    s = c * _sigmoid(c)
    o_ref[...] = _dot3(s, w_ref[...]) + b_ref[...]


def _mod_call(cc, w_mod, b_mod):
    rows = cc.shape[0]
    n = w_mod.shape[1]
    tn = 1024
    return pl.pallas_call(
        _mod_body,
        grid=(n // tn,),
        in_specs=[
            pl.BlockSpec((rows, D_MODEL), lambda j: (0, 0)),
            pl.BlockSpec((D_MODEL, tn), lambda j: (0, j)),
            pl.BlockSpec((1, tn), lambda j: (0, j)),
        ],
        out_specs=pl.BlockSpec((rows, tn), lambda j: (0, j)),
        out_shape=jax.ShapeDtypeStruct((rows, n), F32),
        compiler_params=_params(("arbitrary",)),
        name="mod",
    )(cc, w_mod, b_mod)


def _rms_rows(x, n):
    return x * lax.rsqrt(jnp.sum(x * x, axis=-1, keepdims=True) * (1.0 / n) + EPS)


def _rope(x, cos, sin):
    return x * cos + pltpu.roll(x, LANES // 2, 1) * sin


def _proj_body(*refs, latent):
    if latent:
        (x_ref, g_ref, sc_ref, sh_ref, win_ref, qag_ref, kvag_ref, wq_ref, wk_ref, wv_ref,
         qg_ref, kg_ref, qadd_ref, kadd_ref, cos_ref, sin_ref,
         q_out, k_out, v_out, f_out, ga_out, gb_out) = refs
    else:
        (x_ref, g_ref, sc_ref, sh_ref, win_ref, kvag_ref, wk_ref, wv_ref, kg_ref, kadd_ref,
         k_out, v_out) = refs
    x = x_ref[0]
    h = _rms_rows(x, D_MODEL) * (g_ref[...] * (1.0 + sc_ref[0])) + sh_ref[0]
    hb = h.astype(BF16)

    if latent:
        cos = cos_ref[...]
        sin = sin_ref[...]

    pe = _dot(hb, win_ref[:, PC_PE:PC_KV])

    ckv = _rms_rows(_dot(hb, win_ref[:, PC_KV:PC_F]), KV_LORA) * kvag_ref[...]
    ckvb = ckv.astype(BF16)
    kall = _dot(ckvb, wk_ref[...])
    kg = kg_ref[...]
    kadd = kadd_ref[...]
    for hd in range(MLA_HEADS):
        kh = kall[:, hd * HEAD_SLOT:(hd + 1) * HEAD_SLOT] + pe
        kh = _rms_rows(kh, QK_DIM) * kg
        if latent:
            kh = _rope(kh, cos, sin)
        k_out[0, hd] = (kh + kadd).astype(BF16)
    v = _dot(ckvb, wv_ref[...])
    for hp in range(MLA_HEADS // 2):
        v_out[0, hp] = v[:, hp * LANES:(hp + 1) * LANES].T.astype(BF16)

    if latent:
        cq = _rms_rows(_dot(hb, win_ref[:, PC_Q:PC_PE]), Q_LORA) * qag_ref[...]
        qall = _dot(cq.astype(BF16), wq_ref[...])
        qg = qg_ref[...]
        qadd = qadd_ref[...]
        for hd in range(MLA_HEADS):
            qh = qall[:, hd * HEAD_SLOT:(hd + 1) * HEAD_SLOT]
            q_out[0, hd] = (_rope(_rms_rows(qh, QK_DIM) * qg, cos, sin) + qadd).astype(BF16)
        f_out[0] = _dot(hb, win_ref[:, PC_F:PC_GA]).astype(BF16)
        ga_out[0] = _sigmoid(_dot(hb, win_ref[:, PC_GA:PC_GB])).astype(BF16)
        gb_out[0] = _sigmoid(_dot(hb, win_ref[:, PC_GB:PC_END])).astype(BF16)


def _full(shape):
    nd = len(shape)
    return pl.BlockSpec(shape, lambda *_: (0,) * nd)


def _proj_lat_call(x, g1n, sc1, sh1, w_in_p, qag, kvag, wq_p, wk_p, wv_p, qg_p, kg_p, qadd, kadd, cos_t, sin_t, tm):
    B, T, _ = x.shape
    H = MLA_HEADS
    tok = lambda w: pl.BlockSpec((1, tm, w), lambda b, j: (b, j, 0))
    per_b = pl.BlockSpec((1, 1, D_MODEL), lambda b, j: (b, 0, 0))
    tab = pl.BlockSpec((tm, HEAD_SLOT), lambda b, j: (j, 0))
    return pl.pallas_call(
        functools.partial(_proj_body, latent=True),
        grid=(B, T // tm),
        in_specs=[tok(D_MODEL), _full(g1n.shape), per_b, per_b, _full(w_in_p.shape), _full(qag.shape),
                  _full(kvag.shape), _full(wq_p.shape), _full(wk_p.shape), _full(wv_p.shape),
                  _full(qg_p.shape), _full(kg_p.shape), _full(qadd.shape), _full(kadd.shape), tab, tab],
        out_specs=[
            pl.BlockSpec((1, H, tm, HEAD_SLOT), lambda b, j: (b, 0, j, 0)),
            pl.BlockSpec((1, H, tm, HEAD_SLOT), lambda b, j: (b, 0, j, 0)),
            pl.BlockSpec((1, H // 2, LANES, tm), lambda b, j: (b, 0, 0, j)),
            tok(D_F), tok(D_MODEL), tok(D_MODEL),
        ],
        out_shape=[
            jax.ShapeDtypeStruct((B, H, T, HEAD_SLOT), BF16),
            jax.ShapeDtypeStruct((B, H, T, HEAD_SLOT), BF16),
            jax.ShapeDtypeStruct((B, H // 2, LANES, T), BF16),
            jax.ShapeDtypeStruct((B, T, D_F), BF16),
            jax.ShapeDtypeStruct((B, T, D_MODEL), BF16),
            jax.ShapeDtypeStruct((B, T, D_MODEL), BF16),
        ],
        compiler_params=_params(("parallel", "parallel")),
        name="proj_lat",
    )(x, g1n, sc1, sh1, w_in_p, qag, kvag, wq_p, wk_p, wv_p, qg_p, kg_p, qadd, kadd, cos_t, sin_t)


def _proj_ctx_call(ctx, g1n, csc1, csh1, w_in_p, kvag, wk_p, wv_p, kg_p, kadd):
    B, TC, _ = ctx.shape
    H = MLA_HEADS
    shared = pl.BlockSpec((1, 1, D_MODEL), lambda b: (0, 0, 0))
    return pl.pallas_call(
        functools.partial(_proj_body, latent=False),
        grid=(B,),
        in_specs=[pl.BlockSpec((1, TC, D_MODEL), lambda b: (b, 0, 0)), _full(g1n.shape), shared, shared,
                  _full(w_in_p.shape), _full(kvag.shape), _full(wk_p.shape), _full(wv_p.shape),
                  _full(kg_p.shape), _full(kadd.shape)],
        out_specs=[
            pl.BlockSpec((1, H, TC, HEAD_SLOT), lambda b: (b, 0, 0, 0)),
            pl.BlockSpec((1, H // 2, LANES, TC), lambda b: (b, 0, 0, 0)),
        ],
        out_shape=[
            jax.ShapeDtypeStruct((B, H, TC, HEAD_SLOT), BF16),
            jax.ShapeDtypeStruct((B, H // 2, LANES, TC), BF16),
        ],
        compiler_params=_params(("parallel",)),
        name="proj_ctx",
    )(ctx, g1n, csc1, csh1, w_in_p, kvag, wk_p, wv_p, kg_p, kadd)


def _attn_body(q_ref, kl_ref, kc_ref, vl_ref, vc_ref, o_ref):
    tq = q_ref.shape[2]
    T = kl_ref.shape[2]
    nck = T // KEY_CHUNK
    row = lax.broadcasted_iota(jnp.int32, (LANES, tq), 0)

    def probs(hd):
        q = q_ref[0, hd]
        den = jnp.zeros((1, tq), F32)
        ps = []
        for c in range(nck + 1):
            k = kc_ref[0, hd] if c == nck else kl_ref[0, hd, c * KEY_CHUNK:(c + 1) * KEY_CHUNK, :]
            e = jnp.exp2(_dot_nt(k, q))
            den = den + jnp.sum(e, axis=0, keepdims=True)
            ps.append(e.astype(BF16))
        return ps, den

    def weighted_values(hd, ps):
        hp = hd // 2
        o = _dot(vc_ref[0, hp], ps[nck])
        for c in range(nck):
            o = o + _dot(vl_ref[0, hp, :, c * KEY_CHUNK:(c + 1) * KEY_CHUNK], ps[c])
        return o

    outs, dens, prev = [], [], None
    for hd in range(MLA_HEADS + 1):
        cur = probs(hd) if hd < MLA_HEADS else None
        if prev is not None:
            outs.append(weighted_values(hd - 1, prev[0]) / prev[1])
            dens.append(prev[1])
        prev = cur
    for hp in range(MLA_HEADS // 2):
        o_ref[0, hp] = jnp.where(row < V_DIM, outs[2 * hp], outs[2 * hp + 1]).astype(BF16)
    min_den = functools.reduce(jnp.minimum, dens)

    @pl.when(jnp.min(min_den) < DEN_FLOOR)
    def _():
        def pair(hp, carry):
            res = []
            for hd in (2 * hp, 2 * hp + 1):
                q = q_ref[0, hd]
                sl = _dot_nt(kl_ref[0, hd], q)
                sc = _dot_nt(kc_ref[0, hd], q)
                m = jnp.maximum(jnp.max(sl, axis=0, keepdims=True), jnp.max(sc, axis=0, keepdims=True))
                el = jnp.exp2(sl - m)
                ec = jnp.exp2(sc - m)
                den = jnp.sum(el, axis=0, keepdims=True) + jnp.sum(ec, axis=0, keepdims=True)
                o = _dot(vl_ref[0, hp], el.astype(BF16)) + _dot(vc_ref[0, hp], ec.astype(BF16))
                res.append(o / den)
            o_ref[0, hp] = jnp.where(row < V_DIM, res[0], res[1]).astype(BF16)
            return carry

        lax.fori_loop(0, MLA_HEADS // 2, pair, 0)


def _attn_call(q, k_lat, k_ctx, vt_lat, vt_ctx, tq):
    B, H, T, _ = q.shape
    TC = k_ctx.shape[2]
    return pl.pallas_call(
        _attn_body,
        grid=(B, T // tq),
        in_specs=[
            pl.BlockSpec((1, H, tq, HEAD_SLOT), lambda b, j: (b, 0, j, 0)),
            pl.BlockSpec((1, H, T, HEAD_SLOT), lambda b, j: (b, 0, 0, 0)),
            pl.BlockSpec((1, H, TC, HEAD_SLOT), lambda b, j: (b, 0, 0, 0)),
            pl.BlockSpec((1, H // 2, LANES, T), lambda b, j: (b, 0, 0, 0)),
            pl.BlockSpec((1, H // 2, LANES, TC), lambda b, j: (b, 0, 0, 0)),
        ],
        out_specs=pl.BlockSpec((1, H // 2, LANES, tq), lambda b, j: (b, 0, 0, j)),
        out_shape=jax.ShapeDtypeStruct((B, H // 2, LANES, T), BF16),
        compiler_params=_params(("parallel", "parallel")),
        name="attn",
    )(q, k_lat, k_ctx, vt_lat, vt_ctx)


def _cadd(a, b):
    return a[0] + b[0], a[1] + b[1]


def _csub(a, b):
    return a[0] - b[0], a[1] - b[1]


def _cmul_neg_i(a):
    return a[1], -a[0]


def _cmul_pos_i(a):
    return -a[1], a[0]


def _fft8(u):
    r = 0.7071067811865476
    a0, a1 = _cadd(u[0], u[4]), _csub(u[0], u[4])
    a2, a3 = _cadd(u[2], u[6]), _csub(u[2], u[6])
    a4, a5 = _cadd(u[1], u[5]), _csub(u[1], u[5])
    a6, a7 = _cadd(u[3], u[7]), _csub(u[3], u[7])
    e0, e2 = _cadd(a0, a2), _csub(a0, a2)
    e1, e3 = _cadd(a1, _cmul_neg_i(a3)), _cadd(a1, _cmul_pos_i(a3))
    o0, o2 = _cadd(a4, a6), _csub(a4, a6)
    o1, o3 = _cadd(a5, _cmul_neg_i(a7)), _cadd(a5, _cmul_pos_i(a7))
    wo1 = (r * (o1[0] + o1[1]), r * (o1[1] - o1[0]))
    wo3 = (r * (o3[1] - o3[0]), -r * (o3[0] + o3[1]))
    return [_cadd(e0, o0), _cadd(e1, wo1), _cadd(e2, _cmul_neg_i(o2)), _cadd(e3, wo3),
            _csub(e0, o0), _csub(e1, wo1), _cadd(e2, _cmul_pos_i(o2)), _csub(e3, wo3)]


def _fourier_body(f_ref, cc_ref, twc_ref, tws_ref, m2_ref, o_ref, b_ref, y_ref):
    for g in range(F_GROUPS):
        cols = slice(g * F_GROUP_DIM, (g + 1) * F_GROUP_DIM)
        y = _dot(f_ref[0, :, cols], cc_ref[...])
        u = [(y[j * FFT_N:(j + 1) * FFT_N, :F_GROUP_DIM], y[j * FFT_N:(j + 1) * FFT_N, F_GROUP_DIM:])
             for j in range(FFT_R)]
        a = _fft8(u)
        for k1 in range(FFT_R):
            ar, ai = a[k1]
            if k1 > 0:
                c = twc_ref[k1 * FFT_N:(k1 + 1) * FFT_N, :]
                s = tws_ref[k1 * FFT_N:(k1 + 1) * FFT_N, :]
                ar, ai = ar * c + ai * s, ai * c - ar * s
            b_ref[k1, 0:FFT_N, cols] = ar.astype(BF16)
            b_ref[k1, FFT_N:2 * FFT_N, cols] = ai.astype(BF16)
    for k1 in range(FFT_R):
        y = _dot(m2_ref[...], b_ref[k1])
        for g in range(F_GROUPS):
            y_ref[g, pl.ds(k1, FFT_N, stride=FFT_R), :] = y[:, g * F_GROUP_DIM:(g + 1) * F_GROUP_DIM]
    for g in range(F_GROUPS):
        o_ref[0, :, g * F_GROUP_DIM:(g + 1) * F_GROUP_DIM] = y_ref[g].astype(BF16)


def _fourier_call(f, cc, twc, tws, m2):
    B, T, _ = f.shape
    return pl.pallas_call(
        _fourier_body,
        grid=(B,),
        in_specs=[pl.BlockSpec((1, T, D_F), lambda b: (b, 0, 0)), _full(cc.shape), _full(twc.shape),
                  _full(tws.shape), _full(m2.shape)],
        out_specs=pl.BlockSpec((1, T, D_F), lambda b: (b, 0, 0)),
        out_shape=jax.ShapeDtypeStruct((B, T, D_F), BF16),
        scratch_shapes=[pltpu.VMEM((FFT_R, 2 * FFT_N, D_F), BF16), pltpu.VMEM((F_GROUPS, T, F_GROUP_DIM), F32)],
        compiler_params=_params(("parallel",)),
        name="fourier",
    )(f, cc, twc, tws, m2)


def _merge_body(at_ref, fo_ref, ga_ref, gb_ref, x_ref, g1_ref, g2n_ref, sc2_ref, sh2_ref,
                wo_ref, wf_ref, wout_ref, wrh_ref, wrl_ref, x1_out, h2_out, aff_out, h2_tm):
    tm = x_ref.shape[1]
    attn_t = jnp.concatenate([at_ref[0, hp] for hp in range(MLA_HEADS // 2)], axis=0)
    a = lax.dot_general(attn_t, wo_ref[...], (((0,), (0,)), ((), ())), preferred_element_type=F32)
    fo = _dot(fo_ref[0], wf_ref[...])
    mix = ga_ref[0].astype(F32) * a + gb_ref[0].astype(F32) * fo
    y = _dot(mix.astype(BF16), wout_ref[...])
    x1 = x_ref[0] + g1_ref[0] * y
    x1_out[0] = x1
    h2 = _rms_rows(x1, D_MODEL) * g2n_ref[...]
    h2 = h2 * (1.0 + sc2_ref[0]) + sh2_ref[0]
    for c in range(D_MODEL // LANES):
        h2_tm[pl.ds(c, tm, stride=D_MODEL // LANES), :] = h2[:, c * LANES:(c + 1) * LANES]
    h2_out[0] = h2_tm[...].astype(BF16)
    hh, hl = _split2(h2)
    p_hi = _dot(hh, wrl_ref[...])
    logits = p_hi + pltpu.roll(p_hi, LANES - N_EXPERTS, 1) + _dot(hl, wrh_ref[...])
    lane = lax.broadcasted_iota(jnp.int32, logits.shape, 1)
    logits = jnp.where(lane < N_EXPERTS, logits, -1e30)
    ex = jnp.exp(logits - jnp.max(logits, axis=-1, keepdims=True))
    aff_out[0] = ex / jnp.sum(ex, axis=-1, keepdims=True)


def _merge_call(attn, four, ga, gb, x, g1, g2n, sc2, sh2, wo, wf, wout, wrh, wrl, tm):
    B, T, _ = x.shape
    H = MLA_HEADS
    tok = lambda w: pl.BlockSpec((1, tm, w), lambda b, j: (b, j, 0))
    per_b = pl.BlockSpec((1, 1, D_MODEL), lambda b, j: (b, 0, 0))
    nchunk = D_MODEL // LANES
    return pl.pallas_call(
        _merge_body,
        grid=(B, T // tm),
        in_specs=[
            pl.BlockSpec((1, H // 2, LANES, tm), lambda b, j: (b, 0, 0, j)),
            tok(D_F), tok(D_MODEL), tok(D_MODEL), tok(D_MODEL), per_b, _full(g2n.shape), per_b, per_b,
            _full(wo.shape), _full(wf.shape), _full(wout.shape), _full(wrh.shape), _full(wrl.shape),
        ],
        out_specs=[
            tok(D_MODEL),
            pl.BlockSpec((1, tm * nchunk, LANES), lambda b, j: (b, j, 0)),
            tok(LANES),
        ],
        out_shape=[
            jax.ShapeDtypeStruct((B, T, D_MODEL), F32),
            jax.ShapeDtypeStruct((B, T * nchunk, LANES), BF16),
            jax.ShapeDtypeStruct((B, T, LANES), F32),
        ],
        scratch_shapes=[pltpu.VMEM((tm * nchunk, LANES), F32)],
        compiler_params=_params(("parallel", "parallel")),
        name="merge",
    )(attn, four, ga, gb, x, g1, g2n, sc2, sh2, wo, wf, wout, wrh, wrl)


def _prefix_excl(m, tri):
    outs = []
    run = jnp.zeros((m.shape[0], 1), F32)
    for blk in range(m.shape[1] // LANES):
        mb = m[:, blk * LANES:(blk + 1) * LANES]
        inc = _dot(mb.astype(BF16), tri)
        outs.append(inc - mb + run)
        run = run + inc[:, LANES - 1:LANES]
    return jnp.concatenate(outs, axis=1)


def _select_body(aff_ref, tri_ref, pos_ref, idx_out, gate_out, *, cap):
    T = aff_ref.shape[1]
    aff = aff_ref[0].T[0:N_EXPERTS]

    def search(carry):
        base, step, rounds, _ = carry
        thr = base
        for j in range(1, 2 ** SEARCH_BITS):
            cand = base + float(j) * step
            cnt = jnp.sum((aff >= cand).astype(F32), axis=-1, keepdims=True)
            thr = jnp.where(cnt >= float(cap), cand, thr)
        settled = jnp.where((thr > 0.0) & (step < thr * SEARCH_DONE), 1.0, 0.0)
        return thr, step * (0.5 ** SEARCH_BITS), rounds + 1, (jnp.min(settled) < 1.0).astype(jnp.int32)

    init = (jnp.zeros((N_EXPERTS, 1), F32), jnp.full((N_EXPERTS, 1), 2.0 * 0.5 ** SEARCH_BITS, F32),
            jnp.int32(0), jnp.int32(1))
    thr = lax.while_loop(lambda c: (c[3] > 0) & (c[2] < SEARCH_MAX_ROUNDS), search, init)[0]
    gt = (aff > thr).astype(F32)
    eq = (aff == thr).astype(F32)
    need = cap - jnp.sum(gt, axis=-1, keepdims=True)
    tri = tri_ref[...]
    sel = gt + eq * (_prefix_excl(eq, tri) < need).astype(F32)
    slot = jnp.where(sel > 0.0, _prefix_excl(sel, tri), -1.0)

    pos = pos_ref[...]
    srow = lax.broadcasted_iota(jnp.int32, (cap, T), 0).astype(F32)
    zero = jnp.zeros((3, T), BF16)
    for e in range(N_EXPERTS):
        onehot = jnp.where(srow == slot[e:e + 1, :], 1.0, 0.0).astype(BF16)
        ah, am, al = _split3(aff[e:e + 1, :])
        vals = jnp.concatenate([pos, ah, am, al, zero], axis=0)
        res = _dot_nt(vals, onehot)
        idx_out[0, e:e + 1, :] = (res[0:1] * float(LANES) + res[1:2]).astype(jnp.int32)
        gate_out[0, e:e + 1, :] = res[2:3] + (res[3:4] + res[4:5])


def _select_call(aff, tri, pos, cap):
    B, T, _ = aff.shape
    return pl.pallas_call(
        functools.partial(_select_body, cap=cap),
        grid=(B,),
        in_specs=[pl.BlockSpec((1, T, LANES), lambda b: (b, 0, 0)), _full(tri.shape), _full(pos.shape)],
        out_specs=[pl.BlockSpec((1, N_EXPERTS, cap), lambda b: (b, 0, 0)),
                   pl.BlockSpec((1, N_EXPERTS, cap), lambda b: (b, 0, 0))],
        out_shape=[jax.ShapeDtypeStruct((B, N_EXPERTS, cap), jnp.int32),
                   jax.ShapeDtypeStruct((B, N_EXPERTS, cap), F32)],
        compiler_params=_params(("parallel",)),
        name="select",
    )(aff, tri, pos)


def _moe_body(idx_ref, gate_ref, h2_in, wg_ref, wu_ref, wd_ref, o_ref, h2_ref, acc_ref, xg_a, xg_b, ye_a,
              ye_b, *, cap, stride):
    k = pl.program_id(1)
    nchunk = D_MODEL // LANES
    group = 8

    def zero_of(tile):
        bits = pltpu.bitcast(tile, jnp.uint32)
        return pltpu.bitcast(lax.shift_right_logical(bits, jnp.uint32(32)), F32)

    def gather(e, xg_ref):
        def rows(s0, hold):
            for s in range(s0, s0 + group):
                base = pl.multiple_of(idx_ref[0, e, 0, s], SUBLANES)
                tile = h2_ref[pl.ds(base, nchunk), :] + hold
                xg_ref[pl.ds(s, nchunk, stride=stride), :] = tile
            return tile
        return [functools.partial(rows, s0) for s0 in range(0, cap, group)]

    def scatter(e, ye_ref):
        def rows(s0, hold):
            bases, news = [], []
            for s in range(s0, s0 + group):
                base = pl.multiple_of(idx_ref[0, e, 0, s], SUBLANES)
                bases.append(base)
                news.append(acc_ref[pl.ds(base, nchunk), :]
                            + (ye_ref[pl.ds(s, nchunk, stride=stride), :] + hold))
            for base, new in zip(bases, news):
                acc_ref[pl.ds(base, nchunk), :] = new
            return news[-1]
        return [functools.partial(rows, s0) for s0 in range(0, cap, group)]

    def expert(j, e, xg_ref, ye_ref):
        state = {}
        n_mid = D_EXPERT // MXU_TILE
        n_out = D_MODEL // MXU_TILE

        def load(zero):
            xb = jnp.concatenate([xg_ref[c * stride:c * stride + cap, :] for c in range(nchunk)], axis=1)
            state["x"] = xb.astype(BF16)
            state["gate"] = jnp.broadcast_to(gate_ref[0, e], (LANES, cap)).T[:, 0:1]
            state["h"] = []
            return xb[0:SUBLANES, 0:LANES]

        def mid(n, zero):
            cols = slice(n * MXU_TILE, (n + 1) * MXU_TILE)
            a = _dot(state["x"], wg_ref[j, :, cols]) + zero
            u = _dot(state["x"], wu_ref[j, :, cols])
            h = _silu(a) * u
            state["h"].append(h.astype(BF16))
            return h[0:SUBLANES, 0:LANES]

        def out(n, zero):
            cols = slice(n * MXU_TILE, (n + 1) * MXU_TILE)
            ye = _dot(state["h"][0], wd_ref[j, 0:MXU_TILE, cols])
            for m in range(1, n_mid):
                ye = ye + _dot(state["h"][m], wd_ref[j, m * MXU_TILE:(m + 1) * MXU_TILE, cols])
            ye = ye * state["gate"] + zero
            for c in range(MXU_TILE // LANES):
                cc = n * (MXU_TILE // LANES) + c
                ye_ref[cc * stride:cc * stride + cap, :] = ye[:, c * LANES:(c + 1) * LANES]
            return ye[0:SUBLANES, 0:LANES]

        return ([load] + [functools.partial(mid, n) for n in range(n_mid)]
                + [functools.partial(out, n) for n in range(n_out)])

    def paced(main, side):
        none = jnp.zeros((SUBLANES, LANES), F32)
        hold = none
        edges = [none, none, none]
        done = 0
        for i, piece in enumerate(main):
            zero = jnp.concatenate([edges[0][0:1, :]] * (MXU_TILE // LANES), axis=1)
            witness = piece(zero)
            upto = (len(side) * (i + 1)) // len(main)
            tiles = [s(hold) for s in side[done:upto]]
            done = upto
            hold = zero_of(witness)
            edges = edges[1:] + [zero_of(functools.reduce(jnp.add, tiles)) if tiles else none]

    def zipped(a, b):
        return [f for pair in zip(a, b) for f in pair]

    @pl.when(k == 0)
    def _():
        h2_ref[...] = h2_in[0].astype(F32)
        acc_ref[...] = jnp.zeros_like(acc_ref)
        ye_b[...] = jnp.zeros_like(ye_b)
        for piece in gather(0, xg_a):
            piece(jnp.zeros((SUBLANES, LANES), F32))

    n_experts = 2 * pl.num_programs(1)
    e0 = 2 * k
    e_prev = jnp.maximum(e0 - 1, 0)
    e_next = jnp.minimum(e0 + 2, n_experts - 1)
    paced(expert(0, e0, xg_a, ye_a), zipped(scatter(e_prev, ye_b), gather(e0 + 1, xg_b)))
    paced(expert(1, e0 + 1, xg_b, ye_b), zipped(scatter(e0, ye_a), gather(e_next, xg_a)))

    @pl.when(k == pl.num_programs(1) - 1)
    def _():
        for piece in scatter(n_experts - 1, ye_b):
            piece(jnp.zeros((SUBLANES, LANES), F32))
        o_ref[0] = acc_ref[...].astype(BF16)


def _moe_call(idx, gate, h2t, wg, wu, wd, T):
    B, E, cap = idx.shape
    nchunk = D_MODEL // LANES
    stride = cap + SUBLANES
    buf = pltpu.VMEM((nchunk * stride, LANES), F32)
    idx4 = idx.reshape(B, E, 1, cap)
    gate4 = gate.reshape(B, E, 1, cap)

    return pl.pallas_call(
        functools.partial(_moe_body, cap=cap, stride=stride),
        grid=(B, E // 2),
        in_specs=[
            pl.BlockSpec((1, E, 1, cap), lambda b, k: (b, 0, 0, 0), memory_space=pltpu.SMEM),
            pl.BlockSpec((1, E, 1, cap), lambda b, k: (b, 0, 0, 0)),
            pl.BlockSpec((1, T * nchunk, LANES), lambda b, k: (b, 0, 0)),
            pl.BlockSpec((2, D_MODEL, D_EXPERT), lambda b, k: (k, 0, 0)),
            pl.BlockSpec((2, D_MODEL, D_EXPERT), lambda b, k: (k, 0, 0)),
            pl.BlockSpec((2, D_EXPERT, D_MODEL), lambda b, k: (k, 0, 0)),
        ],
        out_specs=pl.BlockSpec((1, T * nchunk, LANES), lambda b, k: (b, 0, 0)),
        out_shape=jax.ShapeDtypeStruct((B, T * nchunk, LANES), BF16),
        scratch_shapes=[pltpu.VMEM((T * nchunk, LANES), F32), pltpu.VMEM((T * nchunk, LANES), F32),
                        buf, buf, buf, buf],
        compiler_params=_params(("parallel", "arbitrary")),
        name="moe",
    )(idx4, gate4, h2t, wg, wu, wd)


def _final_body(x1_ref, y_ref, g2_ref, o_ref, y32_ref):
    tm = x1_ref.shape[1]
    y32_ref[...] = y_ref[0].astype(F32)
    for c in range(D_MODEL // LANES):
        cols = slice(c * LANES, (c + 1) * LANES)
        y = y32_ref[pl.ds(c, tm, stride=D_MODEL // LANES), :]
        o_ref[0, :, cols] = x1_ref[0, :, cols] + g2_ref[0, :, cols] * y


def _final_call(x1, y, g2, tm):
    B, T, _ = x1.shape
    nchunk = D_MODEL // LANES
    tok = pl.BlockSpec((1, tm, D_MODEL), lambda b, j: (b, j, 0))
    return pl.pallas_call(
        _final_body,
        grid=(B, T // tm),
        in_specs=[tok,
                  pl.BlockSpec((1, tm * nchunk, LANES), lambda b, j: (b, j, 0)),
                  pl.BlockSpec((1, 1, D_MODEL), lambda b, j: (b, 0, 0))],
        out_specs=tok,
        out_shape=jax.ShapeDtypeStruct((B, T, D_MODEL), F32),
        scratch_shapes=[pltpu.VMEM((tm * nchunk, LANES), F32)],
        compiler_params=_params(("parallel", "parallel")),
        name="final",
    )(x1, y, g2)


def _slot_layout():
    half = QK_ROPE // 2
    n_freq = half // 2
    src = np.full((HEAD_SLOT,), -1, np.int64)
    for axis in range(2):
        first = QK_NOPE + axis * half
        src[axis * n_freq:(axis + 1) * n_freq] = np.arange(first, first + n_freq)
        src[LANES // 2 + axis * n_freq:LANES // 2 + (axis + 1) * n_freq] = np.arange(first + n_freq, first + half)
    src[half:half + QK_NOPE // 2] = np.arange(0, QK_NOPE // 2)
    src[LANES // 2 + half:LANES // 2 + half + QK_NOPE // 2] = np.arange(QK_NOPE // 2, QK_NOPE)
    assert src[SHIFT_LANE] == -1 and sorted(src[src >= 0]) == list(range(QK_DIM))
    return src


_SLOT_SRC = _slot_layout()


def _rope_tables(T):
    half = QK_ROPE // 2
    n_freq = half // 2
    inv_freq = 1.0 / (ROPE_THETA ** (np.arange(n_freq, dtype=np.float32) / n_freq))
    t = np.arange(T)
    cos = np.ones((T, HEAD_SLOT), np.float32)
    sin = np.zeros((T, HEAD_SLOT), np.float32)
    for axis, pos in enumerate((t // GRID_W, t % GRID_W)):
        ang = pos.astype(np.float32)[:, None] * inv_freq[None, :].astype(np.float32)
        c, s = np.cos(ang).astype(np.float32), np.sin(ang).astype(np.float32)
        lo1 = axis * n_freq
        lo2 = LANES // 2 + axis * n_freq
        cos[:, lo1:lo1 + n_freq] = c
        cos[:, lo2:lo2 + n_freq] = c
        sin[:, lo1:lo1 + n_freq] = -s
        sin[:, lo2:lo2 + n_freq] = s
    return jnp.asarray(cos), jnp.asarray(sin)


def _dft_tables(T):
    assert T == FFT_R * FFT_N
    n = np.arange(F_GROUP_DIM)
    ang_c = 2.0 * np.pi * ((n[:, None] * n[None, :]) % F_GROUP_DIM) / F_GROUP_DIM
    cc = np.concatenate([np.cos(ang_c), -np.sin(ang_c)], axis=1) / np.sqrt(F_GROUP_DIM)
    k1 = np.repeat(np.arange(FFT_R), FFT_N)
    n2 = np.tile(np.arange(FFT_N), FFT_R)
    ang_w = 2.0 * np.pi * (k1 * n2) / T
    twc = np.broadcast_to(np.cos(ang_w)[:, None], (T, F_GROUP_DIM))
    tws = np.broadcast_to(np.sin(ang_w)[:, None], (T, F_GROUP_DIM))
    m = np.arange(FFT_N)
    ang_2 = 2.0 * np.pi * ((m[:, None] * m[None, :]) % FFT_N) / FFT_N
    m2 = np.concatenate([np.cos(ang_2), np.sin(ang_2)], axis=1) / np.sqrt(T)
    return (jnp.asarray(cc, F32).astype(BF16), jnp.asarray(twc, F32), jnp.asarray(tws, F32),
            jnp.asarray(m2, F32).astype(BF16))


def kernel(x, c, ctx, c_ctx, w_mod, b_mod, norm1_g, w_in, q_a_norm_g, kv_a_norm_g, w_q_up, w_kv_up,
           q_norm_g, k_norm_g, w_o_attn, w_fourier, w_out, norm2_g, w_router, w_e_gate, w_e_up, w_e_down):
    B, T, D = x.shape
    assert w_mod.shape[0] == 1 and D == D_MODEL and T % 1024 == 0
    H = MLA_HEADS
    cap = (CAPACITY_FACTOR * T) // N_EXPERTS

    rows = -(-(B + 1) // SUBLANES) * SUBLANES
    cc_in = jnp.concatenate([c, c_ctx[None, :], jnp.zeros((rows - B - 1, D), F32)], axis=0)
    mod = _mod_call(cc_in, w_mod[0], b_mod)
    sh1, sc1, g1, sh2, sc2, g2 = [mod[:B, i * D:(i + 1) * D].reshape(B, 1, D) for i in range(6)]
    csh1 = mod[B:B + 1, 0:D].reshape(1, 1, D)
    csc1 = mod[B:B + 1, D:2 * D].reshape(1, 1, D)

    src = _SLOT_SRC
    used = jnp.asarray(src >= 0, F32)
    nope = jnp.asarray((src >= 0) & (src < QK_NOPE), F32)
    rope = jnp.asarray(src >= QK_NOPE, F32)
    lane_src = np.maximum(src, 0)
    wi = w_in[0]
    pe_cols = jnp.take(wi[:, OFF_KPE:OFF_F], np.maximum(src - QK_NOPE, 0), axis=1) * rope
    w_in_p = jnp.concatenate([wi[:, OFF_Q:OFF_KV], pe_cols, wi[:, OFF_KV:OFF_KPE], wi[:, OFF_F:N_IN]], axis=1)
    w_in_p = w_in_p.astype(BF16)
    wq_p = jnp.take(w_q_up[0].reshape(Q_LORA, H, QK_DIM), lane_src, axis=2) * used
    wq_p = wq_p.reshape(Q_LORA, H * HEAD_SLOT).astype(BF16)
    wkv = w_kv_up[0].reshape(KV_LORA, H, QK_NOPE + V_DIM)
    wk_p = jnp.take(wkv[:, :, :QK_NOPE], np.minimum(lane_src, QK_NOPE - 1), axis=2) * nope
    wk_p = wk_p.reshape(KV_LORA, H * HEAD_SLOT).astype(BF16)
    wv_p = wkv[:, :, QK_NOPE:].reshape(KV_LORA, H * V_DIM).astype(BF16)
    qg_p = (jnp.take(q_norm_g[0], lane_src) * used * (QK_DIM ** -0.5 * LOG2E)).reshape(1, HEAD_SLOT)
    kg_p = (jnp.take(k_norm_g[0], lane_src) * used).reshape(1, HEAD_SLOT)
    bound = (QK_DIM * BOUND_SLACK) * jnp.max(jnp.abs(qg_p)) * jnp.max(jnp.abs(kg_p))
    on_shift = jnp.asarray(np.arange(HEAD_SLOT) == SHIFT_LANE, F32).reshape(1, HEAD_SLOT)
    qadd = on_shift * (-bound)
    kadd = on_shift
    qag = q_a_norm_g[0].reshape(1, Q_LORA)
    kvag = kv_a_norm_g[0].reshape(1, KV_LORA)
    g1n = norm1_g[0].reshape(1, D)
    g2n = norm2_g[0].reshape(1, D)
    cos_t, sin_t = _rope_tables(T)

    q, k_lat, vt_lat, f, ga, gb = _proj_lat_call(x, g1n, sc1, sh1, w_in_p, qag, kvag, wq_p, wk_p, wv_p,
                                                 qg_p, kg_p, qadd, kadd, cos_t, sin_t, tm=512)
    k_ctx, vt_ctx = _proj_ctx_call(ctx, g1n, csc1, csh1, w_in_p, kvag, wk_p, wv_p, kg_p, kadd)
    attn = _attn_call(q, k_lat, k_ctx, vt_lat, vt_ctx, tq=1024)
    four = _fourier_call(f, *_dft_tables(T))

    wr = jnp.pad(w_router[0], ((0, 0), (0, LANES - N_EXPERTS)))
    wrh = wr.astype(BF16)
    wrl = wrh + jnp.roll((wr - wrh.astype(F32)).astype(BF16), N_EXPERTS, axis=1)
    x1, h2t, aff = _merge_call(attn, four, ga, gb, x, g1, g2n, sc2, sh2, w_o_attn[0].astype(BF16),
                               w_fourier[0].astype(BF16), w_out[0].astype(BF16), wrh, wrl, tm=512)

    tri = jnp.asarray(np.triu(np.ones((LANES, LANES), np.float32)), BF16)
    addr = np.arange(T) * (D // LANES)
    pos = jnp.asarray(np.stack([addr // LANES, addr % LANES]).astype(np.float32), BF16)
    idx, gate = _select_call(aff, tri, pos, cap)
    y = _moe_call(idx, gate, h2t, w_e_gate[0].astype(BF16), w_e_up[0].astype(BF16),
                  w_e_down[0].astype(BF16), T)
    return _final_call(x1, y, g2, tm=1024)
```

```python
import functools

import numpy as np
import jax
import jax.numpy as jnp
from jax import lax
from jax.experimental import pallas as pl
from jax.experimental.pallas import tpu as pltpu

F32 = jnp.float32
BF16 = jnp.bfloat16

D_MODEL = 1024
GRID_W = 64
MLA_HEADS = 8
QK_NOPE = 64
QK_ROPE = 32
QK_DIM = QK_NOPE + QK_ROPE
V_DIM = 64
Q_LORA = 384
KV_LORA = 256
ROPE_THETA = 10000.0
F_GROUPS = 4
F_GROUP_DIM = 128
D_F = F_GROUPS * F_GROUP_DIM
OFF_Q = 0
OFF_KV = OFF_Q + Q_LORA
OFF_KPE = OFF_KV + KV_LORA
OFF_F = OFF_KPE + QK_ROPE
OFF_GA = OFF_F + D_F
OFF_GB = OFF_GA + D_MODEL
N_IN = OFF_GB + D_MODEL
N_EXPERTS = 16
D_EXPERT = 512
CAPACITY_FACTOR = 2
EPS = 1e-6

LANES = 128
SUBLANES = 8
MXU_TILE = 256
HEAD_SLOT = LANES
SHIFT_LANE = 48
BOUND_SLACK = 1.02
DEN_FLOOR = 2.0 ** -60
LOG2E = 1.4426950408889634
FFT_R = 8
FFT_N = 256
SEARCH_BITS = 3
SEARCH_DONE = 2.0 ** -30
SEARCH_MAX_ROUNDS = 56
KEY_CHUNK = 256
VMEM_LIMIT = 56 * 1024 * 1024

PC_Q = 0
PC_PE = PC_Q + Q_LORA
PC_KV = PC_PE + HEAD_SLOT
PC_F = PC_KV + KV_LORA
PC_GA = PC_F + D_F
PC_GB = PC_GA + D_MODEL
PC_END = PC_GB + D_MODEL


def _dot(a, b):
    return jnp.dot(a, b, preferred_element_type=F32)


def _dot_nt(a, b):
    return lax.dot_general(a, b, (((1,), (1,)), ((), ())), preferred_element_type=F32)


def _split2(a):
    hi = a.astype(BF16)
    lo = (a - hi.astype(F32)).astype(BF16)
    return hi, lo


def _split3(a):
    hi = a.astype(BF16)
    r = a - hi.astype(F32)
    mid = r.astype(BF16)
    lo = (r - mid.astype(F32)).astype(BF16)
    return hi, mid, lo


def _dot3(a, b):
    ah, al = _split2(a)
    bh, bl = _split2(b)
    return _dot(ah, bh) + (_dot(ah, bl) + _dot(al, bh))


def _sigmoid(x):
    return 1.0 / (1.0 + jnp.exp(-x))


def _silu(x):
    return x * (0.5 * jnp.tanh(0.5 * x) + 0.5)


def _params(sem):
    return pltpu.CompilerParams(dimension_semantics=sem, vmem_limit_bytes=VMEM_LIMIT)


def _mod_body(c_ref, w_ref, b_ref, o_ref):
    c = c_ref[...]
    s = c * _sigmoid(c)
    o_ref[...] = _dot3(s, w_ref[...]) + b_ref[...]


def _mod_call(cc, w_mod, b_mod):
    rows = cc.shape[0]
    n = w_mod.shape[1]
    tn = 1024
    return pl.pallas_call(
        _mod_body,
        grid=(n // tn,),
        in_specs=[
            pl.BlockSpec((rows, D_MODEL), lambda j: (0, 0)),
            pl.BlockSpec((D_MODEL, tn), lambda j: (0, j)),
            pl.BlockSpec((1, tn), lambda j: (0, j)),
        ],
        out_specs=pl.BlockSpec((rows, tn), lambda j: (0, j)),
        out_shape=jax.ShapeDtypeStruct((rows, n), F32),
        compiler_params=_params(("arbitrary",)),
        name="mod",
    )(cc, w_mod, b_mod)


def _rms_rows(x, n):
    return x * lax.rsqrt(jnp.sum(x * x, axis=-1, keepdims=True) * (1.0 / n) + EPS)


def _rope(x, cos, sin):
    return x * cos + pltpu.roll(x, LANES // 2, 1) * sin


def _proj_body(*refs, latent):
    if latent:
        (x_ref, g_ref, sc_ref, sh_ref, win_ref, qag_ref, kvag_ref, wq_ref, wk_ref, wv_ref,
         qg_ref, kg_ref, qadd_ref, kadd_ref, cos_ref, sin_ref,
         q_out, k_out, v_out, f_out, ga_out, gb_out) = refs
    else:
        (x_ref, g_ref, sc_ref, sh_ref, win_ref, kvag_ref, wk_ref, wv_ref, kg_ref, kadd_ref,
         k_out, v_out) = refs
    x = x_ref[0]
    h = _rms_rows(x, D_MODEL) * (g_ref[...] * (1.0 + sc_ref[0])) + sh_ref[0]
    hb = h.astype(BF16)

    if latent:
        cos = cos_ref[...]
        sin = sin_ref[...]

    pe = _dot(hb, win_ref[:, PC_PE:PC_KV])

    ckv = _rms_rows(_dot(hb, win_ref[:, PC_KV:PC_F]), KV_LORA) * kvag_ref[...]
    ckvb = ckv.astype(BF16)
    kall = _dot(ckvb, wk_ref[...])
    kg = kg_ref[...]
    kadd = kadd_ref[...]
    for hd in range(MLA_HEADS):
        kh = kall[:, hd * HEAD_SLOT:(hd + 1) * HEAD_SLOT] + pe
        kh = _rms_rows(kh, QK_DIM) * kg
        if latent:
            kh = _rope(kh, cos, sin)
        k_out[0, hd] = (kh + kadd).astype(BF16)
    v = _dot(ckvb, wv_ref[...])
    for hp in range(MLA_HEADS // 2):
        v_out[0, hp] = v[:, hp * LANES:(hp + 1) * LANES].T.astype(BF16)

    if latent:
        cq = _rms_rows(_dot(hb, win_ref[:, PC_Q:PC_PE]), Q_LORA) * qag_ref[...]
        qall = _dot(cq.astype(BF16), wq_ref[...])
        qg = qg_ref[...]
        qadd = qadd_ref[...]
        for hd in range(MLA_HEADS):
            qh = qall[:, hd * HEAD_SLOT:(hd + 1) * HEAD_SLOT]
            q_out[0, hd] = (_rope(_rms_rows(qh, QK_DIM) * qg, cos, sin) + qadd).astype(BF16)
        f_out[0] = _dot(hb, win_ref[:, PC_F:PC_GA]).astype(BF16)
        ga_out[0] = _sigmoid(_dot(hb, win_ref[:, PC_GA:PC_GB])).astype(BF16)
        gb_out[0] = _sigmoid(_dot(hb, win_ref[:, PC_GB:PC_END])).astype(BF16)


def _full(shape):
    nd = len(shape)
    return pl.BlockSpec(shape, lambda *_: (0,) * nd)


def _proj_lat_call(x, g1n, sc1, sh1, w_in_p, qag, kvag, wq_p, wk_p, wv_p, qg_p, kg_p, qadd, kadd, cos_t, sin_t, tm):
    B, T, _ = x.shape
    H = MLA_HEADS
    tok = lambda w: pl.BlockSpec((1, tm, w), lambda b, j: (b, j, 0))
    per_b = pl.BlockSpec((1, 1, D_MODEL), lambda b, j: (b, 0, 0))
    tab = pl.BlockSpec((tm, HEAD_SLOT), lambda b, j: (j, 0))
    return pl.pallas_call(
        functools.partial(_proj_body, latent=True),
        grid=(B, T // tm),
        in_specs=[tok(D_MODEL), _full(g1n.shape), per_b, per_b, _full(w_in_p.shape), _full(qag.shape),
                  _full(kvag.shape), _full(wq_p.shape), _full(wk_p.shape), _full(wv_p.shape),
                  _full(qg_p.shape), _full(kg_p.shape), _full(qadd.shape), _full(kadd.shape), tab, tab],
        out_specs=[
            pl.BlockSpec((1, H, tm, HEAD_SLOT), lambda b, j: (b, 0, j, 0)),
            pl.BlockSpec((1, H, tm, HEAD_SLOT), lambda b, j: (b, 0, j, 0)),
            pl.BlockSpec((1, H // 2, LANES, tm), lambda b, j: (b, 0, 0, j)),
            tok(D_F), tok(D_MODEL), tok(D_MODEL),
        ],
        out_shape=[
            jax.ShapeDtypeStruct((B, H, T, HEAD_SLOT), BF16),
            jax.ShapeDtypeStruct((B, H, T, HEAD_SLOT), BF16),
            jax.ShapeDtypeStruct((B, H // 2, LANES, T), BF16),
            jax.ShapeDtypeStruct((B, T, D_F), BF16),
            jax.ShapeDtypeStruct((B, T, D_MODEL), BF16),
            jax.ShapeDtypeStruct((B, T, D_MODEL), BF16),
        ],
        compiler_params=_params(("parallel", "parallel")),
        name="proj_lat",
    )(x, g1n, sc1, sh1, w_in_p, qag, kvag, wq_p, wk_p, wv_p, qg_p, kg_p, qadd, kadd, cos_t, sin_t)


def _proj_ctx_call(ctx, g1n, csc1, csh1, w_in_p, kvag, wk_p, wv_p, kg_p, kadd):
    B, TC, _ = ctx.shape
    H = MLA_HEADS
    shared = pl.BlockSpec((1, 1, D_MODEL), lambda b: (0, 0, 0))
    return pl.pallas_call(
        functools.partial(_proj_body, latent=False),
        grid=(B,),
        in_specs=[pl.BlockSpec((1, TC, D_MODEL), lambda b: (b, 0, 0)), _full(g1n.shape), shared, shared,
                  _full(w_in_p.shape), _full(kvag.shape), _full(wk_p.shape), _full(wv_p.shape),
                  _full(kg_p.shape), _full(kadd.shape)],
        out_specs=[
            pl.BlockSpec((1, H, TC, HEAD_SLOT), lambda b: (b, 0, 0, 0)),
            pl.BlockSpec((1, H // 2, LANES, TC), lambda b: (b, 0, 0, 0)),
        ],
        out_shape=[
            jax.ShapeDtypeStruct((B, H, TC, HEAD_SLOT), BF16),
            jax.ShapeDtypeStruct((B, H // 2, LANES, TC), BF16),
        ],
        compiler_params=_params(("parallel",)),
        name="proj_ctx",
    )(ctx, g1n, csc1, csh1, w_in_p, kvag, wk_p, wv_p, kg_p, kadd)


def _attn_body(q_ref, kl_ref, kc_ref, vl_ref, vc_ref, o_ref):
    tq = q_ref.shape[2]
    T = kl_ref.shape[2]
    nck = T // KEY_CHUNK
    row = lax.broadcasted_iota(jnp.int32, (LANES, tq), 0)

    def probs(hd):
        q = q_ref[0, hd]
        den = jnp.zeros((1, tq), F32)
        ps = []
        for c in range(nck + 1):
            k = kc_ref[0, hd] if c == nck else kl_ref[0, hd, c * KEY_CHUNK:(c + 1) * KEY_CHUNK, :]
            e = jnp.exp2(_dot_nt(k, q))
            den = den + jnp.sum(e, axis=0, keepdims=True)
            ps.append(e.astype(BF16))
        return ps, den

    def weighted_values(hd, ps):
        hp = hd // 2
        o = _dot(vc_ref[0, hp], ps[nck])
        for c in range(nck):
            o = o + _dot(vl_ref[0, hp, :, c * KEY_CHUNK:(c + 1) * KEY_CHUNK], ps[c])
        return o

    outs, dens, prev = [], [], None
    for hd in range(MLA_HEADS + 1):
        cur = probs(hd) if hd < MLA_HEADS else None
        if prev is not None:
            outs.append(weighted_values(hd - 1, prev[0]) / prev[1])
            dens.append(prev[1])
        prev = cur
    for hp in range(MLA_HEADS // 2):
        o_ref[0, hp] = jnp.where(row < V_DIM, outs[2 * hp], outs[2 * hp + 1]).astype(BF16)
    min_den = functools.reduce(jnp.minimum, dens)

    @pl.when(jnp.min(min_den) < DEN_FLOOR)
    def _():
        def pair(hp, carry):
            res = []
            for hd in (2 * hp, 2 * hp + 1):
                q = q_ref[0, hd]
                sl = _dot_nt(kl_ref[0, hd], q)
                sc = _dot_nt(kc_ref[0, hd], q)
                m = jnp.maximum(jnp.max(sl, axis=0, keepdims=True), jnp.max(sc, axis=0, keepdims=True))
                el = jnp.exp2(sl - m)
                ec = jnp.exp2(sc - m)
                den = jnp.sum(el, axis=0, keepdims=True) + jnp.sum(ec, axis=0, keepdims=True)
                o = _dot(vl_ref[0, hp], el.astype(BF16)) + _dot(vc_ref[0, hp], ec.astype(BF16))
                res.append(o / den)
            o_ref[0, hp] = jnp.where(row < V_DIM, res[0], res[1]).astype(BF16)
            return carry

        lax.fori_loop(0, MLA_HEADS // 2, pair, 0)


def _attn_call(q, k_lat, k_ctx, vt_lat, vt_ctx, tq):
    B, H, T, _ = q.shape
    TC = k_ctx.shape[2]
    return pl.pallas_call(
        _attn_body,
        grid=(B, T // tq),
        in_specs=[
            pl.BlockSpec((1, H, tq, HEAD_SLOT), lambda b, j: (b, 0, j, 0)),
            pl.BlockSpec((1, H, T, HEAD_SLOT), lambda b, j: (b, 0, 0, 0)),
            pl.BlockSpec((1, H, TC, HEAD_SLOT), lambda b, j: (b, 0, 0, 0)),
            pl.BlockSpec((1, H // 2, LANES, T), lambda b, j: (b, 0, 0, 0)),
            pl.BlockSpec((1, H // 2, LANES, TC), lambda b, j: (b, 0, 0, 0)),
        ],
        out_specs=pl.BlockSpec((1, H // 2, LANES, tq), lambda b, j: (b, 0, 0, j)),
        out_shape=jax.ShapeDtypeStruct((B, H // 2, LANES, T), BF16),
        compiler_params=_params(("parallel", "parallel")),
        name="attn",
    )(q, k_lat, k_ctx, vt_lat, vt_ctx)


def _cadd(a, b):
    return a[0] + b[0], a[1] + b[1]


def _csub(a, b):
    return a[0] - b[0], a[1] - b[1]


def _cmul_neg_i(a):
    return a[1], -a[0]


def _cmul_pos_i(a):
    return -a[1], a[0]


def _fft8(u):
    r = 0.7071067811865476
    a0, a1 = _cadd(u[0], u[4]), _csub(u[0], u[4])
    a2, a3 = _cadd(u[2], u[6]), _csub(u[2], u[6])
    a4, a5 = _cadd(u[1], u[5]), _csub(u[1], u[5])
    a6, a7 = _cadd(u[3], u[7]), _csub(u[3], u[7])
    e0, e2 = _cadd(a0, a2), _csub(a0, a2)
    e1, e3 = _cadd(a1, _cmul_neg_i(a3)), _cadd(a1, _cmul_pos_i(a3))
    o0, o2 = _cadd(a4, a6), _csub(a4, a6)
    o1, o3 = _cadd(a5, _cmul_neg_i(a7)), _cadd(a5, _cmul_pos_i(a7))
    wo1 = (r * (o1[0] + o1[1]), r * (o1[1] - o1[0]))
    wo3 = (r * (o3[1] - o3[0]), -r * (o3[0] + o3[1]))
    return [_cadd(e0, o0), _cadd(e1, wo1), _cadd(e2, _cmul_neg_i(o2)), _cadd(e3, wo3),
            _csub(e0, o0), _csub(e1, wo1), _cadd(e2, _cmul_pos_i(o2)), _csub(e3, wo3)]


def _fourier_body(f_ref, cc_ref, twc_ref, tws_ref, m2_ref, o_ref, b_ref, y_ref):
    for g in range(F_GROUPS):
        cols = slice(g * F_GROUP_DIM, (g + 1) * F_GROUP_DIM)
        y = _dot(f_ref[0, :, cols], cc_ref[...])
        u = [(y[j * FFT_N:(j + 1) * FFT_N, :F_GROUP_DIM], y[j * FFT_N:(j + 1) * FFT_N, F_GROUP_DIM:])
             for j in range(FFT_R)]
        a = _fft8(u)
        for k1 in range(FFT_R):
            ar, ai = a[k1]
            if k1 > 0:
                c = twc_ref[k1 * FFT_N:(k1 + 1) * FFT_N, :]
                s = tws_ref[k1 * FFT_N:(k1 + 1) * FFT_N, :]
                ar, ai = ar * c + ai * s, ai * c - ar * s
            b_ref[k1, 0:FFT_N, cols] = ar.astype(BF16)
            b_ref[k1, FFT_N:2 * FFT_N, cols] = ai.astype(BF16)
    for k1 in range(FFT_R):
        y = _dot(m2_ref[...], b_ref[k1])
        for g in range(F_GROUPS):
            y_ref[g, pl.ds(k1, FFT_N, stride=FFT_R), :] = y[:, g * F_GROUP_DIM:(g + 1) * F_GROUP_DIM]
    for g in range(F_GROUPS):
        o_ref[0, :, g * F_GROUP_DIM:(g + 1) * F_GROUP_DIM] = y_ref[g].astype(BF16)


def _fourier_call(f, cc, twc, tws, m2):
    B, T, _ = f.shape
    return pl.pallas_call(
        _fourier_body,
        grid=(B,),
        in_specs=[pl.BlockSpec((1, T, D_F), lambda b: (b, 0, 0)), _full(cc.shape), _full(twc.shape),
                  _full(tws.shape), _full(m2.shape)],
        out_specs=pl.BlockSpec((1, T, D_F), lambda b: (b, 0, 0)),
        out_shape=jax.ShapeDtypeStruct((B, T, D_F), BF16),
        scratch_shapes=[pltpu.VMEM((FFT_R, 2 * FFT_N, D_F), BF16), pltpu.VMEM((F_GROUPS, T, F_GROUP_DIM), F32)],
        compiler_params=_params(("parallel",)),
        name="fourier",
    )(f, cc, twc, tws, m2)


def _merge_body(at_ref, fo_ref, ga_ref, gb_ref, x_ref, g1_ref, g2n_ref, sc2_ref, sh2_ref,
                wo_ref, wf_ref, wout_ref, wrh_ref, wrl_ref, x1_out, h2_out, aff_out, h2_tm):
    tm = x_ref.shape[1]
    attn_t = jnp.concatenate([at_ref[0, hp] for hp in range(MLA_HEADS // 2)], axis=0)
    a = lax.dot_general(attn_t, wo_ref[...], (((0,), (0,)), ((), ())), preferred_element_type=F32)
    fo = _dot(fo_ref[0], wf_ref[...])
    mix = ga_ref[0].astype(F32) * a + gb_ref[0].astype(F32) * fo
    y = _dot(mix.astype(BF16), wout_ref[...])
    x1 = x_ref[0] + g1_ref[0] * y
    x1_out[0] = x1
    h2 = _rms_rows(x1, D_MODEL) * g2n_ref[...]
    h2 = h2 * (1.0 + sc2_ref[0]) + sh2_ref[0]
    for c in range(D_MODEL // LANES):
        h2_tm[pl.ds(c, tm, stride=D_MODEL // LANES), :] = h2[:, c * LANES:(c + 1) * LANES]
    h2_out[0] = h2_tm[...].astype(BF16)
    hh, hl = _split2(h2)
    p_hi = _dot(hh, wrl_ref[...])
    logits = p_hi + pltpu.roll(p_hi, LANES - N_EXPERTS, 1) + _dot(hl, wrh_ref[...])
    lane = lax.broadcasted_iota(jnp.int32, logits.shape, 1)
    logits = jnp.where(lane < N_EXPERTS, logits, -1e30)
    ex = jnp.exp(logits - jnp.max(logits, axis=-1, keepdims=True))
    aff_out[0] = ex / jnp.sum(ex, axis=-1, keepdims=True)


def _merge_call(attn, four, ga, gb, x, g1, g2n, sc2, sh2, wo, wf, wout, wrh, wrl, tm):
    B, T, _ = x.shape
    H = MLA_HEADS
    tok = lambda w: pl.BlockSpec((1, tm, w), lambda b, j: (b, j, 0))
    per_b = pl.BlockSpec((1, 1, D_MODEL), lambda b, j: (b, 0, 0))
    nchunk = D_MODEL // LANES
    return pl.pallas_call(
        _merge_body,
        grid=(B, T // tm),
        in_specs=[
            pl.BlockSpec((1, H // 2, LANES, tm), lambda b, j: (b, 0, 0, j)),
            tok(D_F), tok(D_MODEL), tok(D_MODEL), tok(D_MODEL), per_b, _full(g2n.shape), per_b, per_b,
            _full(wo.shape), _full(wf.shape), _full(wout.shape), _full(wrh.shape), _full(wrl.shape),
        ],
        out_specs=[
            tok(D_MODEL),
            pl.BlockSpec((1, tm * nchunk, LANES), lambda b, j: (b, j, 0)),
            tok(LANES),
        ],
        out_shape=[
            jax.ShapeDtypeStruct((B, T, D_MODEL), F32),
            jax.ShapeDtypeStruct((B, T * nchunk, LANES), BF16),
            jax.ShapeDtypeStruct((B, T, LANES), F32),
        ],
        scratch_shapes=[pltpu.VMEM((tm * nchunk, LANES), F32)],
        compiler_params=_params(("parallel", "parallel")),
        name="merge",
    )(attn, four, ga, gb, x, g1, g2n, sc2, sh2, wo, wf, wout, wrh, wrl)


def _prefix_excl(m, tri):
    outs = []
    run = jnp.zeros((m.shape[0], 1), F32)
    for blk in range(m.shape[1] // LANES):
        mb = m[:, blk * LANES:(blk + 1) * LANES]
        inc = _dot(mb.astype(BF16), tri)
        outs.append(inc - mb + run)
        run = run + inc[:, LANES - 1:LANES]
    return jnp.concatenate(outs, axis=1)


def _select_body(aff_ref, tri_ref, pos_ref, idx_out, gate_out, *, cap):
    n_s, T = aff_ref.shape[0], aff_ref.shape[1]
    n_rows = n_s * N_EXPERTS
    aff = jnp.concatenate([aff_ref[i].T[0:N_EXPERTS] for i in range(n_s)], axis=0)

    def search(carry):
        base, step, rounds, _ = carry
        thr = base
        for j in range(1, 2 ** SEARCH_BITS):
            cand = base + float(j) * step
            cnt = jnp.sum((aff >= cand).astype(F32), axis=-1, keepdims=True)
            thr = jnp.where(cnt >= float(cap), cand, thr)
        settled = jnp.where((thr > 0.0) & (step < thr * SEARCH_DONE), 1.0, 0.0)
        return thr, step * (0.5 ** SEARCH_BITS), rounds + 1, (jnp.min(settled) < 1.0).astype(jnp.int32)

    init = (jnp.zeros((n_rows, 1), F32), jnp.full((n_rows, 1), 2.0 * 0.5 ** SEARCH_BITS, F32),
            jnp.int32(0), jnp.int32(1))
    thr = lax.while_loop(lambda c: (c[3] > 0) & (c[2] < SEARCH_MAX_ROUNDS), search, init)[0]
    gt = (aff > thr).astype(F32)
    eq = (aff == thr).astype(F32)
    need = cap - jnp.sum(gt, axis=-1, keepdims=True)
    tri = tri_ref[...]
    sel = gt + eq * (_prefix_excl(eq, tri) < need).astype(F32)
    slot = jnp.where(sel > 0.0, _prefix_excl(sel, tri), -1.0)

    pos = pos_ref[...]
    srow = lax.broadcasted_iota(jnp.int32, (cap, T), 0).astype(F32)
    zero = jnp.zeros((3, T), BF16)
    for r in range(n_rows):
        i, e = divmod(r, N_EXPERTS)
        onehot = jnp.where(srow == slot[r:r + 1, :], 1.0, 0.0).astype(BF16)
        ah, am, al = _split3(aff[r:r + 1, :])
        vals = jnp.concatenate([pos, ah, am, al, zero], axis=0)
        res = _dot_nt(vals, onehot)
        idx_out[i, e:e + 1, :] = (res[0:1] * float(LANES) + res[1:2]).astype(jnp.int32)
        gate_out[i, e:e + 1, :] = res[2:3] + (res[3:4] + res[4:5])


def _select_call(aff, tri, pos, cap):
    B, T, _ = aff.shape
    n_s = 2 if B % 2 == 0 else 1
    return pl.pallas_call(
        functools.partial(_select_body, cap=cap),
        grid=(B // n_s,),
        in_specs=[pl.BlockSpec((n_s, T, LANES), lambda b: (b, 0, 0)), _full(tri.shape), _full(pos.shape)],
        out_specs=[pl.BlockSpec((n_s, N_EXPERTS, cap), lambda b: (b, 0, 0)),
                   pl.BlockSpec((n_s, N_EXPERTS, cap), lambda b: (b, 0, 0))],
        out_shape=[jax.ShapeDtypeStruct((B, N_EXPERTS, cap), jnp.int32),
                   jax.ShapeDtypeStruct((B, N_EXPERTS, cap), F32)],
        compiler_params=_params(("parallel",)),
        name="select",
    )(aff, tri, pos)


def _moe_body(idx_ref, gate_ref, h2_in, wg_ref, wu_ref, wd_ref, o_ref, h2_ref, acc_ref, xg_a, xg_b, ye_a,
              ye_b, *, cap, stride):
    k = pl.program_id(1)
    nchunk = D_MODEL // LANES
    group = 8

    def zero_of(tile):
        bits = pltpu.bitcast(tile, jnp.uint32)
        return pltpu.bitcast(lax.shift_right_logical(bits, jnp.uint32(32)), F32)

    def gather(e, xg_ref):
        def rows(s0, hold):
            for s in range(s0, s0 + group):
                base = pl.multiple_of(idx_ref[0, e, 0, s], SUBLANES)
                tile = h2_ref[pl.ds(base, nchunk), :] + hold
                xg_ref[pl.ds(s, nchunk, stride=stride), :] = tile
            return tile
        return [functools.partial(rows, s0) for s0 in range(0, cap, group)]

    def scatter(e, ye_ref):
        def rows(s0, hold):
            bases, news = [], []
            for s in range(s0, s0 + group):
                base = pl.multiple_of(idx_ref[0, e, 0, s], SUBLANES)
                bases.append(base)
                news.append(acc_ref[pl.ds(base, nchunk), :]
                            + (ye_ref[pl.ds(s, nchunk, stride=stride), :] + hold))
            for base, new in zip(bases, news):
                acc_ref[pl.ds(base, nchunk), :] = new
            return news[-1]
        return [functools.partial(rows, s0) for s0 in range(0, cap, group)]

    def expert(j, e, xg_ref, ye_ref):
        state = {}
        n_mid = D_EXPERT // MXU_TILE
        n_out = D_MODEL // MXU_TILE

        def load(zero):
            xb = jnp.concatenate([xg_ref[c * stride:c * stride + cap, :] for c in range(nchunk)], axis=1)
            state["x"] = xb.astype(BF16)
            state["gate"] = jnp.broadcast_to(gate_ref[0, e], (LANES, cap)).T[:, 0:1]
            state["h"] = []
            return xb[0:SUBLANES, 0:LANES]

        def mid(n, zero):
            cols = slice(n * MXU_TILE, (n + 1) * MXU_TILE)
            a = _dot(state["x"], wg_ref[j, :, cols]) + zero
            u = _dot(state["x"], wu_ref[j, :, cols])
            h = _silu(a) * u
            state["h"].append(h.astype(BF16))
            return h[0:SUBLANES, 0:LANES]

        def out(n, zero):
            cols = slice(n * MXU_TILE, (n + 1) * MXU_TILE)
            ye = _dot(state["h"][0], wd_ref[j, 0:MXU_TILE, cols])
            for m in range(1, n_mid):
                ye = ye + _dot(state["h"][m], wd_ref[j, m * MXU_TILE:(m + 1) * MXU_TILE, cols])
            ye = ye * state["gate"] + zero
            for c in range(MXU_TILE // LANES):
                cc = n * (MXU_TILE // LANES) + c
                ye_ref[cc * stride:cc * stride + cap, :] = ye[:, c * LANES:(c + 1) * LANES]
            return ye[0:SUBLANES, 0:LANES]

        return ([load] + [functools.partial(mid, n) for n in range(n_mid)]
                + [functools.partial(out, n) for n in range(n_out)])

    def paced(main, side):
        none = jnp.zeros((SUBLANES, LANES), F32)
        hold = none
        edges = [none, none, none]
        done = 0
        for i, piece in enumerate(main):
            zero = jnp.concatenate([edges[0][0:1, :]] * (MXU_TILE // LANES), axis=1)
            witness = piece(zero)
            upto = (len(side) * (i + 1)) // len(main)
            tiles = [s(hold) for s in side[done:upto]]
            done = upto
            hold = zero_of(witness)
            edges = edges[1:] + [zero_of(functools.reduce(jnp.add, tiles)) if tiles else none]

    def zipped(a, b):
        return [f for pair in zip(a, b) for f in pair]

    @pl.when(k == 0)
    def _():
        h2_ref[...] = h2_in[0].astype(F32)
        acc_ref[...] = jnp.zeros_like(acc_ref)
        ye_b[...] = jnp.zeros_like(ye_b)
        for piece in gather(0, xg_a):
            piece(jnp.zeros((SUBLANES, LANES), F32))

    n_experts = 2 * pl.num_programs(1)
    e0 = 2 * k
    e_prev = jnp.maximum(e0 - 1, 0)
    e_next = jnp.minimum(e0 + 2, n_experts - 1)
    paced(expert(0, e0, xg_a, ye_a), zipped(scatter(e_prev, ye_b), gather(e0 + 1, xg_b)))
    paced(expert(1, e0 + 1, xg_b, ye_b), zipped(scatter(e0, ye_a), gather(e_next, xg_a)))

    @pl.when(k == pl.num_programs(1) - 1)
    def _():
        for piece in scatter(n_experts - 1, ye_b):
            piece(jnp.zeros((SUBLANES, LANES), F32))
        o_ref[0] = acc_ref[...].astype(BF16)


def _moe_call(idx, gate, h2t, wg, wu, wd, T):
    B, E, cap = idx.shape
    nchunk = D_MODEL // LANES
    stride = cap + SUBLANES
    buf = pltpu.VMEM((nchunk * stride, LANES), F32)
    idx4 = idx.reshape(B, E, 1, cap)
    gate4 = gate.reshape(B, E, 1, cap)

    return pl.pallas_call(
        functools.partial(_moe_body, cap=cap, stride=stride),
        grid=(B, E // 2),
        in_specs=[
            pl.BlockSpec((1, E, 1, cap), lambda b, k: (b, 0, 0, 0), memory_space=pltpu.SMEM),
            pl.BlockSpec((1, E, 1, cap), lambda b, k: (b, 0, 0, 0)),
            pl.BlockSpec((1, T * nchunk, LANES), lambda b, k: (b, 0, 0)),
            pl.BlockSpec((2, D_MODEL, D_EXPERT), lambda b, k: (k, 0, 0)),
            pl.BlockSpec((2, D_MODEL, D_EXPERT), lambda b, k: (k, 0, 0)),
            pl.BlockSpec((2, D_EXPERT, D_MODEL), lambda b, k: (k, 0, 0)),
        ],
        out_specs=pl.BlockSpec((1, T * nchunk, LANES), lambda b, k: (b, 0, 0)),
        out_shape=jax.ShapeDtypeStruct((B, T * nchunk, LANES), BF16),
        scratch_shapes=[pltpu.VMEM((T * nchunk, LANES), F32), pltpu.VMEM((T * nchunk, LANES), F32),
                        buf, buf, buf, buf],
        compiler_params=_params(("parallel", "arbitrary")),
        name="moe",
    )(idx4, gate4, h2t, wg, wu, wd)


def _final_body(x1_ref, y_ref, g2_ref, o_ref, y32_ref):
    tm = x1_ref.shape[1]
    y32_ref[...] = y_ref[0].astype(F32)
    for c in range(D_MODEL // LANES):
        cols = slice(c * LANES, (c + 1) * LANES)
        y = y32_ref[pl.ds(c, tm, stride=D_MODEL // LANES), :]
        o_ref[0, :, cols] = x1_ref[0, :, cols] + g2_ref[0, :, cols] * y


def _final_call(x1, y, g2, tm):
    B, T, _ = x1.shape
    nchunk = D_MODEL // LANES
    tok = pl.BlockSpec((1, tm, D_MODEL), lambda b, j: (b, j, 0))
    return pl.pallas_call(
        _final_body,
        grid=(B, T // tm),
        in_specs=[tok,
                  pl.BlockSpec((1, tm * nchunk, LANES), lambda b, j: (b, j, 0)),
                  pl.BlockSpec((1, 1, D_MODEL), lambda b, j: (b, 0, 0))],
        out_specs=tok,
        out_shape=jax.ShapeDtypeStruct((B, T, D_MODEL), F32),
        scratch_shapes=[pltpu.VMEM((tm * nchunk, LANES), F32)],
        compiler_params=_params(("parallel", "parallel")),
        name="final",
    )(x1, y, g2)


def _slot_layout():
    half = QK_ROPE // 2
    n_freq = half // 2
    src = np.full((HEAD_SLOT,), -1, np.int64)
    for axis in range(2):
        first = QK_NOPE + axis * half
        src[axis * n_freq:(axis + 1) * n_freq] = np.arange(first, first + n_freq)
        src[LANES // 2 + axis * n_freq:LANES // 2 + (axis + 1) * n_freq] = np.arange(first + n_freq, first + half)
    src[half:half + QK_NOPE // 2] = np.arange(0, QK_NOPE // 2)
    src[LANES // 2 + half:LANES // 2 + half + QK_NOPE // 2] = np.arange(QK_NOPE // 2, QK_NOPE)
    assert src[SHIFT_LANE] == -1 and sorted(src[src >= 0]) == list(range(QK_DIM))
    return src


_SLOT_SRC = _slot_layout()


def _rope_tables(T):
    half = QK_ROPE // 2
    n_freq = half // 2
    inv_freq = 1.0 / (ROPE_THETA ** (np.arange(n_freq, dtype=np.float32) / n_freq))
    t = np.arange(T)
    cos = np.ones((T, HEAD_SLOT), np.float32)
    sin = np.zeros((T, HEAD_SLOT), np.float32)
    for axis, pos in enumerate((t // GRID_W, t % GRID_W)):
        ang = pos.astype(np.float32)[:, None] * inv_freq[None, :].astype(np.float32)
        c, s = np.cos(ang).astype(np.float32), np.sin(ang).astype(np.float32)
        lo1 = axis * n_freq
        lo2 = LANES // 2 + axis * n_freq
        cos[:, lo1:lo1 + n_freq] = c
        cos[:, lo2:lo2 + n_freq] = c
        sin[:, lo1:lo1 + n_freq] = -s
        sin[:, lo2:lo2 + n_freq] = s
    return jnp.asarray(cos), jnp.asarray(sin)


def _dft_tables(T):
    assert T == FFT_R * FFT_N
    n = np.arange(F_GROUP_DIM)
    ang_c = 2.0 * np.pi * ((n[:, None] * n[None, :]) % F_GROUP_DIM) / F_GROUP_DIM
    cc = np.concatenate([np.cos(ang_c), -np.sin(ang_c)], axis=1) / np.sqrt(F_GROUP_DIM)
    k1 = np.repeat(np.arange(FFT_R), FFT_N)
    n2 = np.tile(np.arange(FFT_N), FFT_R)
    ang_w = 2.0 * np.pi * (k1 * n2) / T
    twc = np.broadcast_to(np.cos(ang_w)[:, None], (T, F_GROUP_DIM))
    tws = np.broadcast_to(np.sin(ang_w)[:, None], (T, F_GROUP_DIM))
    m = np.arange(FFT_N)
    ang_2 = 2.0 * np.pi * ((m[:, None] * m[None, :]) % FFT_N) / FFT_N
    m2 = np.concatenate([np.cos(ang_2), np.sin(ang_2)], axis=1) / np.sqrt(T)
    return (jnp.asarray(cc, F32).astype(BF16), jnp.asarray(twc, F32), jnp.asarray(tws, F32),
            jnp.asarray(m2, F32).astype(BF16))


def kernel(x, c, ctx, c_ctx, w_mod, b_mod, norm1_g, w_in, q_a_norm_g, kv_a_norm_g, w_q_up, w_kv_up,
           q_norm_g, k_norm_g, w_o_attn, w_fourier, w_out, norm2_g, w_router, w_e_gate, w_e_up, w_e_down):
    B, T, D = x.shape
    assert w_mod.shape[0] == 1 and D == D_MODEL and T % 1024 == 0
    H = MLA_HEADS
    cap = (CAPACITY_FACTOR * T) // N_EXPERTS

    rows = -(-(B + 1) // SUBLANES) * SUBLANES
    cc_in = jnp.concatenate([c, c_ctx[None, :], jnp.zeros((rows - B - 1, D), F32)], axis=0)
    mod = _mod_call(cc_in, w_mod[0], b_mod)
    sh1, sc1, g1, sh2, sc2, g2 = [mod[:B, i * D:(i + 1) * D].reshape(B, 1, D) for i in range(6)]
    csh1 = mod[B:B + 1, 0:D].reshape(1, 1, D)
    csc1 = mod[B:B + 1, D:2 * D].reshape(1, 1, D)

    src = _SLOT_SRC
    used = jnp.asarray(src >= 0, F32)
    nope = jnp.asarray((src >= 0) & (src < QK_NOPE), F32)
    rope = jnp.asarray(src >= QK_NOPE, F32)
    lane_src = np.maximum(src, 0)
    wi = w_in[0]
    pe_cols = jnp.take(wi[:, OFF_KPE:OFF_F], np.maximum(src - QK_NOPE, 0), axis=1) * rope
    w_in_p = jnp.concatenate([wi[:, OFF_Q:OFF_KV], pe_cols, wi[:, OFF_KV:OFF_KPE], wi[:, OFF_F:N_IN]], axis=1)
    w_in_p = w_in_p.astype(BF16)
    wq_p = jnp.take(w_q_up[0].reshape(Q_LORA, H, QK_DIM), lane_src, axis=2) * used
    wq_p = wq_p.reshape(Q_LORA, H * HEAD_SLOT).astype(BF16)
    wkv = w_kv_up[0].reshape(KV_LORA, H, QK_NOPE + V_DIM)
    wk_p = jnp.take(wkv[:, :, :QK_NOPE], np.minimum(lane_src, QK_NOPE - 1), axis=2) * nope
    wk_p = wk_p.reshape(KV_LORA, H * HEAD_SLOT).astype(BF16)
    wv_p = wkv[:, :, QK_NOPE:].reshape(KV_LORA, H * V_DIM).astype(BF16)
    qg_p = (jnp.take(q_norm_g[0], lane_src) * used * (QK_DIM ** -0.5 * LOG2E)).reshape(1, HEAD_SLOT)
    kg_p = (jnp.take(k_norm_g[0], lane_src) * used).reshape(1, HEAD_SLOT)
    bound = (QK_DIM * BOUND_SLACK) * jnp.max(jnp.abs(qg_p)) * jnp.max(jnp.abs(kg_p))
    on_shift = jnp.asarray(np.arange(HEAD_SLOT) == SHIFT_LANE, F32).reshape(1, HEAD_SLOT)
    qadd = on_shift * (-bound)
    kadd = on_shift
    qag = q_a_norm_g[0].reshape(1, Q_LORA)
    kvag = kv_a_norm_g[0].reshape(1, KV_LORA)
    g1n = norm1_g[0].reshape(1, D)
    g2n = norm2_g[0].reshape(1, D)
    cos_t, sin_t = _rope_tables(T)

    q, k_lat, vt_lat, f, ga, gb = _proj_lat_call(x, g1n, sc1, sh1, w_in_p, qag, kvag, wq_p, wk_p, wv_p,
                                                 qg_p, kg_p, qadd, kadd, cos_t, sin_t, tm=512)
    k_ctx, vt_ctx = _proj_ctx_call(ctx, g1n, csc1, csh1, w_in_p, kvag, wk_p, wv_p, kg_p, kadd)
    attn = _attn_call(q, k_lat, k_ctx, vt_lat, vt_ctx, tq=1024)
    four = _fourier_call(f, *_dft_tables(T))

    wr = jnp.pad(w_router[0], ((0, 0), (0, LANES - N_EXPERTS)))
    wrh = wr.astype(BF16)
    wrl = wrh + jnp.roll((wr - wrh.astype(F32)).astype(BF16), N_EXPERTS, axis=1)
    x1, h2t, aff = _merge_call(attn, four, ga, gb, x, g1, g2n, sc2, sh2, w_o_attn[0].astype(BF16),
                               w_fourier[0].astype(BF16), w_out[0].astype(BF16), wrh, wrl, tm=512)

    tri = jnp.asarray(np.triu(np.ones((LANES, LANES), np.float32)), BF16)
    addr = np.arange(T) * (D // LANES)
    pos = jnp.asarray(np.stack([addr // LANES, addr % LANES]).astype(np.float32), BF16)
    idx, gate = _select_call(aff, tri, pos, cap)
    y = _moe_call(idx, gate, h2t, w_e_gate[0].astype(BF16), w_e_up[0].astype(BF16),
                  w_e_down[0].astype(BF16), T)
    return _final_call(x1, y, g2, tm=1024)
```

```python
import functools

import numpy as np
import jax
import jax.numpy as jnp
from jax import lax
from jax.experimental import pallas as pl
from jax.experimental.pallas import tpu as pltpu

F32 = jnp.float32
BF16 = jnp.bfloat16

D_MODEL = 1024
GRID_W = 64
MLA_HEADS = 8
QK_NOPE = 64
QK_ROPE = 32
QK_DIM = QK_NOPE + QK_ROPE
V_DIM = 64
Q_LORA = 384
KV_LORA = 256
ROPE_THETA = 10000.0
F_GROUPS = 4
F_GROUP_DIM = 128
D_F = F_GROUPS * F_GROUP_DIM
OFF_Q = 0
OFF_KV = OFF_Q + Q_LORA
OFF_KPE = OFF_KV + KV_LORA
OFF_F = OFF_KPE + QK_ROPE
OFF_GA = OFF_F + D_F
OFF_GB = OFF_GA + D_MODEL
N_IN = OFF_GB + D_MODEL
N_EXPERTS = 16
D_EXPERT = 512
CAPACITY_FACTOR = 2
EPS = 1e-6

LANES = 128
SUBLANES = 8
MXU_TILE = 256
HEAD_SLOT = LANES
SHIFT_LANE = 48
BOUND_SLACK = 1.02
DEN_FLOOR = 2.0 ** -60
LOG2E = 1.4426950408889634
FFT_R = 8
FFT_N = 256
SEARCH_BITS = 3
SEARCH_DONE = 2.0 ** -30
SEARCH_MAX_ROUNDS = 56
KEY_CHUNK = 256
VMEM_LIMIT = 56 * 1024 * 1024

PC_Q = 0
PC_PE = PC_Q + Q_LORA
PC_KV = PC_PE + HEAD_SLOT
PC_F = PC_KV + KV_LORA
PC_GA = PC_F + D_F
PC_GB = PC_GA + D_MODEL
PC_END = PC_GB + D_MODEL


def _dot(a, b):
    return jnp.dot(a, b, preferred_element_type=F32)


def _dot_nt(a, b):
    return lax.dot_general(a, b, (((1,), (1,)), ((), ())), preferred_element_type=F32)


def _split2(a):
    hi = a.astype(BF16)
    lo = (a - hi.astype(F32)).astype(BF16)
    return hi, lo


def _split3(a):
    hi = a.astype(BF16)
    r = a - hi.astype(F32)
    mid = r.astype(BF16)
    lo = (r - mid.astype(F32)).astype(BF16)
    return hi, mid, lo


def _dot3(a, b):
    ah, al = _split2(a)
    bh, bl = _split2(b)
    return _dot(ah, bh) + (_dot(ah, bl) + _dot(al, bh))


def _sigmoid(x):
    return 1.0 / (1.0 + jnp.exp(-x))


def _silu(x):
    return x * (0.5 * jnp.tanh(0.5 * x) + 0.5)


def _params(sem):
    return pltpu.CompilerParams(dimension_semantics=sem, vmem_limit_bytes=VMEM_LIMIT)


def _mod_body(c_ref, w_ref, b_ref, o_ref):
    c = c_ref[...]
    s = c * _sigmoid(c)
    o_ref[...] = _dot3(s, w_ref[...]) + b_ref[...]


def _mod_call(cc, w_mod, b_mod):
    rows = cc.shape[0]
    n = w_mod.shape[1]
    tn = 1024
    return pl.pallas_call(
        _mod_body,
        grid=(n // tn,),
        in_specs=[
            pl.BlockSpec((rows, D_MODEL), lambda j: (0, 0)),
            pl.BlockSpec((D_MODEL, tn), lambda j: (0, j)),
            pl.BlockSpec((1, tn), lambda j: (0, j)),
        ],
        out_specs=pl.BlockSpec((rows, tn), lambda j: (0, j)),
        out_shape=jax.ShapeDtypeStruct((rows, n), F32),
        compiler_params=_params(("arbitrary",)),
        name="mod",
    )(cc, w_mod, b_mod)


def _rms_rows(x, n):
    return x * lax.rsqrt(jnp.sum(x * x, axis=-1, keepdims=True) * (1.0 / n) + EPS)


def _rope(x, cos, sin):
    return x * cos + pltpu.roll(x, LANES // 2, 1) * sin


def _proj_body(*refs, latent):
    if latent:
        (x_ref, g_ref, sc_ref, sh_ref, win_ref, qag_ref, kvag_ref, wq_ref, wk_ref, wv_ref,
         qg_ref, kg_ref, qadd_ref, kadd_ref, cos_ref, sin_ref,
         q_out, k_out, v_out, f_out, ga_out, gb_out) = refs
    else:
        (x_ref, g_ref, sc_ref, sh_ref, win_ref, kvag_ref, wk_ref, wv_ref, kg_ref, kadd_ref,
         k_out, v_out) = refs
    n_s, rows = x_ref.shape[0], x_ref.shape[1]
    x = x_ref[...].reshape(n_s * rows, D_MODEL)
    h = _rms_rows(x, D_MODEL) * (g_ref[...] * (1.0 + sc_ref[0])) + sh_ref[0]
    hb = h.astype(BF16)

    if latent:
        cos = cos_ref[...]
        sin = sin_ref[...]

    pe = _dot(hb, win_ref[:, PC_PE:PC_KV])

    ckv = _rms_rows(_dot(hb, win_ref[:, PC_KV:PC_F]), KV_LORA) * kvag_ref[...]
    ckvb = ckv.astype(BF16)
    kall = _dot(ckvb, wk_ref[...])
    kg = kg_ref[...]
    kadd = kadd_ref[...]
    for hd in range(MLA_HEADS):
        kh = kall[:, hd * HEAD_SLOT:(hd + 1) * HEAD_SLOT] + pe
        kh = _rms_rows(kh, QK_DIM) * kg
        if latent:
            kh = _rope(kh, cos, sin)
        kh = (kh + kadd).astype(BF16)
        for i in range(n_s):
            k_out[i, hd] = kh[i * rows:(i + 1) * rows]
    v = _dot(ckvb, wv_ref[...])
    for hp in range(MLA_HEADS // 2):
        vt = v[:, hp * LANES:(hp + 1) * LANES].T.astype(BF16)
        for i in range(n_s):
            v_out[i, hp] = vt[:, i * rows:(i + 1) * rows]

    if latent:
        cq = _rms_rows(_dot(hb, win_ref[:, PC_Q:PC_PE]), Q_LORA) * qag_ref[...]
        qall = _dot(cq.astype(BF16), wq_ref[...])
        qg = qg_ref[...]
        qadd = qadd_ref[...]
        for hd in range(MLA_HEADS):
            qh = qall[:, hd * HEAD_SLOT:(hd + 1) * HEAD_SLOT]
            q_out[0, hd] = (_rope(_rms_rows(qh, QK_DIM) * qg, cos, sin) + qadd).astype(BF16)
        f_out[0] = _dot(hb, win_ref[:, PC_F:PC_GA]).astype(BF16)
        ga_out[0] = _sigmoid(_dot(hb, win_ref[:, PC_GA:PC_GB])).astype(BF16)
        gb_out[0] = _sigmoid(_dot(hb, win_ref[:, PC_GB:PC_END])).astype(BF16)


def _full(shape):
    nd = len(shape)
    return pl.BlockSpec(shape, lambda *_: (0,) * nd)


def _proj_lat_call(x, g1n, sc1, sh1, w_in_p, qag, kvag, wq_p, wk_p, wv_p, qg_p, kg_p, qadd, kadd, cos_t, sin_t, tm):
    B, T, _ = x.shape
    H = MLA_HEADS
    tok = lambda w: pl.BlockSpec((1, tm, w), lambda b, j: (b, j, 0))
    per_b = pl.BlockSpec((1, 1, D_MODEL), lambda b, j: (b, 0, 0))
    tab = pl.BlockSpec((tm, HEAD_SLOT), lambda b, j: (j, 0))
    return pl.pallas_call(
        functools.partial(_proj_body, latent=True),
        grid=(B, T // tm),
        in_specs=[tok(D_MODEL), _full(g1n.shape), per_b, per_b, _full(w_in_p.shape), _full(qag.shape),
                  _full(kvag.shape), _full(wq_p.shape), _full(wk_p.shape), _full(wv_p.shape),
                  _full(qg_p.shape), _full(kg_p.shape), _full(qadd.shape), _full(kadd.shape), tab, tab],
        out_specs=[
            pl.BlockSpec((1, H, tm, HEAD_SLOT), lambda b, j: (b, 0, j, 0)),
            pl.BlockSpec((1, H, tm, HEAD_SLOT), lambda b, j: (b, 0, j, 0)),
            pl.BlockSpec((1, H // 2, LANES, tm), lambda b, j: (b, 0, 0, j)),
            tok(D_F), tok(D_MODEL), tok(D_MODEL),
        ],
        out_shape=[
            jax.ShapeDtypeStruct((B, H, T, HEAD_SLOT), BF16),
            jax.ShapeDtypeStruct((B, H, T, HEAD_SLOT), BF16),
            jax.ShapeDtypeStruct((B, H // 2, LANES, T), BF16),
            jax.ShapeDtypeStruct((B, T, D_F), BF16),
            jax.ShapeDtypeStruct((B, T, D_MODEL), BF16),
            jax.ShapeDtypeStruct((B, T, D_MODEL), BF16),
        ],
        compiler_params=_params(("parallel", "parallel")),
        name="proj_lat",
    )(x, g1n, sc1, sh1, w_in_p, qag, kvag, wq_p, wk_p, wv_p, qg_p, kg_p, qadd, kadd, cos_t, sin_t)


def _proj_ctx_call(ctx, g1n, csc1, csh1, w_in_p, kvag, wk_p, wv_p, kg_p, kadd):
    B, TC, _ = ctx.shape
    H = MLA_HEADS
    shared = pl.BlockSpec((1, 1, D_MODEL), lambda b: (0, 0, 0))
    n_s = 4 if B % 4 == 0 else 1
    return pl.pallas_call(
        functools.partial(_proj_body, latent=False),
        grid=(B // n_s,),
        in_specs=[pl.BlockSpec((n_s, TC, D_MODEL), lambda b: (b, 0, 0)), _full(g1n.shape), shared, shared,
                  _full(w_in_p.shape), _full(kvag.shape), _full(wk_p.shape), _full(wv_p.shape),
                  _full(kg_p.shape), _full(kadd.shape)],
        out_specs=[
            pl.BlockSpec((n_s, H, TC, HEAD_SLOT), lambda b: (b, 0, 0, 0)),
            pl.BlockSpec((n_s, H // 2, LANES, TC), lambda b: (b, 0, 0, 0)),
        ],
        out_shape=[
            jax.ShapeDtypeStruct((B, H, TC, HEAD_SLOT), BF16),
            jax.ShapeDtypeStruct((B, H // 2, LANES, TC), BF16),
        ],
        compiler_params=_params(("parallel",)),
        name="proj_ctx",
    )(ctx, g1n, csc1, csh1, w_in_p, kvag, wk_p, wv_p, kg_p, kadd)


def _attn_body(q_ref, kl_ref, kc_ref, vl_ref, vc_ref, o_ref):
    tq = q_ref.shape[2]
    T = kl_ref.shape[2]
    nck = T // KEY_CHUNK
    row = lax.broadcasted_iota(jnp.int32, (LANES, tq), 0)

    def probs(hd):
        q = q_ref[0, hd]
        den = jnp.zeros((1, tq), F32)
        ps = []
        for c in range(nck + 1):
            k = kc_ref[0, hd] if c == nck else kl_ref[0, hd, c * KEY_CHUNK:(c + 1) * KEY_CHUNK, :]
            e = jnp.exp2(_dot_nt(k, q))
            den = den + jnp.sum(e, axis=0, keepdims=True)
            ps.append(e.astype(BF16))
        return ps, den

    def weighted_values(hd, ps):
        hp = hd // 2
        o = _dot(vc_ref[0, hp], ps[nck])
        for c in range(nck):
            o = o + _dot(vl_ref[0, hp, :, c * KEY_CHUNK:(c + 1) * KEY_CHUNK], ps[c])
        return o

    outs, dens, prev = [], [], None
    for hd in range(MLA_HEADS + 1):
        cur = probs(hd) if hd < MLA_HEADS else None
        if prev is not None:
            outs.append(weighted_values(hd - 1, prev[0]) / prev[1])
            dens.append(prev[1])
        prev = cur
    for hp in range(MLA_HEADS // 2):
        o_ref[0, hp] = jnp.where(row < V_DIM, outs[2 * hp], outs[2 * hp + 1]).astype(BF16)
    min_den = functools.reduce(jnp.minimum, dens)

    @pl.when(jnp.min(min_den) < DEN_FLOOR)
    def _():
        def pair(hp, carry):
            res = []
            for hd in (2 * hp, 2 * hp + 1):
                q = q_ref[0, hd]
                sl = _dot_nt(kl_ref[0, hd], q)
                sc = _dot_nt(kc_ref[0, hd], q)
                m = jnp.maximum(jnp.max(sl, axis=0, keepdims=True), jnp.max(sc, axis=0, keepdims=True))
                el = jnp.exp2(sl - m)
                ec = jnp.exp2(sc - m)
                den = jnp.sum(el, axis=0, keepdims=True) + jnp.sum(ec, axis=0, keepdims=True)
                o = _dot(vl_ref[0, hp], el.astype(BF16)) + _dot(vc_ref[0, hp], ec.astype(BF16))
                res.append(o / den)
            o_ref[0, hp] = jnp.where(row < V_DIM, res[0], res[1]).astype(BF16)
            return carry

        lax.fori_loop(0, MLA_HEADS // 2, pair, 0)


def _attn_call(q, k_lat, k_ctx, vt_lat, vt_ctx, tq):
    B, H, T, _ = q.shape
    TC = k_ctx.shape[2]
    return pl.pallas_call(
        _attn_body,
        grid=(B, T // tq),
        in_specs=[
            pl.BlockSpec((1, H, tq, HEAD_SLOT), lambda b, j: (b, 0, j, 0)),
            pl.BlockSpec((1, H, T, HEAD_SLOT), lambda b, j: (b, 0, 0, 0)),
            pl.BlockSpec((1, H, TC, HEAD_SLOT), lambda b, j: (b, 0, 0, 0)),
            pl.BlockSpec((1, H // 2, LANES, T), lambda b, j: (b, 0, 0, 0)),
            pl.BlockSpec((1, H // 2, LANES, TC), lambda b, j: (b, 0, 0, 0)),
        ],
        out_specs=pl.BlockSpec((1, H // 2, LANES, tq), lambda b, j: (b, 0, 0, j)),
        out_shape=jax.ShapeDtypeStruct((B, H // 2, LANES, T), BF16),
        compiler_params=_params(("parallel", "parallel")),
        name="attn",
    )(q, k_lat, k_ctx, vt_lat, vt_ctx)


def _cadd(a, b):
    return a[0] + b[0], a[1] + b[1]


def _csub(a, b):
    return a[0] - b[0], a[1] - b[1]


def _cmul_neg_i(a):
    return a[1], -a[0]


def _cmul_pos_i(a):
    return -a[1], a[0]


def _fft8(u):
    r = 0.7071067811865476
    a0, a1 = _cadd(u[0], u[4]), _csub(u[0], u[4])
    a2, a3 = _cadd(u[2], u[6]), _csub(u[2], u[6])
    a4, a5 = _cadd(u[1], u[5]), _csub(u[1], u[5])
    a6, a7 = _cadd(u[3], u[7]), _csub(u[3], u[7])
    e0, e2 = _cadd(a0, a2), _csub(a0, a2)
    e1, e3 = _cadd(a1, _cmul_neg_i(a3)), _cadd(a1, _cmul_pos_i(a3))
    o0, o2 = _cadd(a4, a6), _csub(a4, a6)
    o1, o3 = _cadd(a5, _cmul_neg_i(a7)), _cadd(a5, _cmul_pos_i(a7))
    wo1 = (r * (o1[0] + o1[1]), r * (o1[1] - o1[0]))
    wo3 = (r * (o3[1] - o3[0]), -r * (o3[0] + o3[1]))
    return [_cadd(e0, o0), _cadd(e1, wo1), _cadd(e2, _cmul_neg_i(o2)), _cadd(e3, wo3),
            _csub(e0, o0), _csub(e1, wo1), _cadd(e2, _cmul_pos_i(o2)), _csub(e3, wo3)]


def _fourier_body(f_ref, cc_ref, twc_ref, tws_ref, m2_ref, o_ref, b_ref, y_ref):
    for g in range(F_GROUPS):
        cols = slice(g * F_GROUP_DIM, (g + 1) * F_GROUP_DIM)
        y = _dot(f_ref[0, :, cols], cc_ref[...])
        u = [(y[j * FFT_N:(j + 1) * FFT_N, :F_GROUP_DIM], y[j * FFT_N:(j + 1) * FFT_N, F_GROUP_DIM:])
             for j in range(FFT_R)]
        a = _fft8(u)
        for k1 in range(FFT_R):
            ar, ai = a[k1]
            if k1 > 0:
                c = twc_ref[k1 * FFT_N:(k1 + 1) * FFT_N, :]
                s = tws_ref[k1 * FFT_N:(k1 + 1) * FFT_N, :]
                ar, ai = ar * c + ai * s, ai * c - ar * s
            b_ref[k1, 0:FFT_N, cols] = ar.astype(BF16)
            b_ref[k1, FFT_N:2 * FFT_N, cols] = ai.astype(BF16)
    for k1 in range(FFT_R):
        y = _dot(m2_ref[...], b_ref[k1])
        for g in range(F_GROUPS):
            y_ref[g, pl.ds(k1, FFT_N, stride=FFT_R), :] = y[:, g * F_GROUP_DIM:(g + 1) * F_GROUP_DIM]
    for g in range(F_GROUPS):
        o_ref[0, :, g * F_GROUP_DIM:(g + 1) * F_GROUP_DIM] = y_ref[g].astype(BF16)


def _fourier_call(f, cc, twc, tws, m2):
    B, T, _ = f.shape
    return pl.pallas_call(
        _fourier_body,
        grid=(B,),
        in_specs=[pl.BlockSpec((1, T, D_F), lambda b: (b, 0, 0)), _full(cc.shape), _full(twc.shape),
                  _full(tws.shape), _full(m2.shape)],
        out_specs=pl.BlockSpec((1, T, D_F), lambda b: (b, 0, 0)),
        out_shape=jax.ShapeDtypeStruct((B, T, D_F), BF16),
        scratch_shapes=[pltpu.VMEM((FFT_R, 2 * FFT_N, D_F), BF16), pltpu.VMEM((F_GROUPS, T, F_GROUP_DIM), F32)],
        compiler_params=_params(("parallel",)),
        name="fourier",
    )(f, cc, twc, tws, m2)


def _merge_body(at_ref, fo_ref, ga_ref, gb_ref, x_ref, g1_ref, g2n_ref, sc2_ref, sh2_ref,
                wo_ref, wf_ref, wout_ref, wrh_ref, wrl_ref, x1_out, h2_out, aff_out, h2_tm):
    tm = x_ref.shape[1]
    attn_t = jnp.concatenate([at_ref[0, hp] for hp in range(MLA_HEADS // 2)], axis=0)
    a = lax.dot_general(attn_t, wo_ref[...], (((0,), (0,)), ((), ())), preferred_element_type=F32)
    fo = _dot(fo_ref[0], wf_ref[...])
    mix = ga_ref[0].astype(F32) * a + gb_ref[0].astype(F32) * fo
    y = _dot(mix.astype(BF16), wout_ref[...])
    x1 = x_ref[0] + g1_ref[0] * y
    x1_out[0] = x1
    h2 = _rms_rows(x1, D_MODEL) * g2n_ref[...]
    h2 = h2 * (1.0 + sc2_ref[0]) + sh2_ref[0]
    for c in range(D_MODEL // LANES):
        h2_tm[pl.ds(c, tm, stride=D_MODEL // LANES), :] = h2[:, c * LANES:(c + 1) * LANES]
    h2_out[0] = h2_tm[...].astype(BF16)
    hh, hl = _split2(h2)
    p_hi = _dot(hh, wrl_ref[...])
    logits = p_hi + pltpu.roll(p_hi, LANES - N_EXPERTS, 1) + _dot(hl, wrh_ref[...])
    lane = lax.broadcasted_iota(jnp.int32, logits.shape, 1)
    logits = jnp.where(lane < N_EXPERTS, logits, -1e30)
    ex = jnp.exp(logits - jnp.max(logits, axis=-1, keepdims=True))
    aff_out[0] = ex / jnp.sum(ex, axis=-1, keepdims=True)


def _merge_call(attn, four, ga, gb, x, g1, g2n, sc2, sh2, wo, wf, wout, wrh, wrl, tm):
    B, T, _ = x.shape
    H = MLA_HEADS
    tok = lambda w: pl.BlockSpec((1, tm, w), lambda b, j: (b, j, 0))
    per_b = pl.BlockSpec((1, 1, D_MODEL), lambda b, j: (b, 0, 0))
    nchunk = D_MODEL // LANES
    return pl.pallas_call(
        _merge_body,
        grid=(B, T // tm),
        in_specs=[
            pl.BlockSpec((1, H // 2, LANES, tm), lambda b, j: (b, 0, 0, j)),
            tok(D_F), tok(D_MODEL), tok(D_MODEL), tok(D_MODEL), per_b, _full(g2n.shape), per_b, per_b,
            _full(wo.shape), _full(wf.shape), _full(wout.shape), _full(wrh.shape), _full(wrl.shape),
        ],
        out_specs=[
            tok(D_MODEL),
            pl.BlockSpec((1, tm * nchunk, LANES), lambda b, j: (b, j, 0)),
            tok(LANES),
        ],
        out_shape=[
            jax.ShapeDtypeStruct((B, T, D_MODEL), F32),
            jax.ShapeDtypeStruct((B, T * nchunk, LANES), BF16),
            jax.ShapeDtypeStruct((B, T, LANES), F32),
        ],
        scratch_shapes=[pltpu.VMEM((tm * nchunk, LANES), F32)],
        compiler_params=_params(("parallel", "parallel")),
        name="merge",
    )(attn, four, ga, gb, x, g1, g2n, sc2, sh2, wo, wf, wout, wrh, wrl)


def _prefix_excl(m, tri):
    outs = []
    run = jnp.zeros((m.shape[0], 1), F32)
    for blk in range(m.shape[1] // LANES):
        mb = m[:, blk * LANES:(blk + 1) * LANES]
        inc = _dot(mb.astype(BF16), tri)
        outs.append(inc - mb + run)
        run = run + inc[:, LANES - 1:LANES]
    return jnp.concatenate(outs, axis=1)


def _select_body(aff_ref, tri_ref, pos_ref, idx_out, gate_out, *, cap):
    n_s, T = aff_ref.shape[0], aff_ref.shape[1]
    n_rows = n_s * N_EXPERTS
    aff = jnp.concatenate([aff_ref[i].T[0:N_EXPERTS] for i in range(n_s)], axis=0)

    def search(carry):
        base, step, rounds, _ = carry
        thr = base
        for j in range(1, 2 ** SEARCH_BITS):
            cand = base + float(j) * step
            cnt = jnp.sum((aff >= cand).astype(F32), axis=-1, keepdims=True)
            thr = jnp.where(cnt >= float(cap), cand, thr)
        settled = jnp.where((thr > 0.0) & (step < thr * SEARCH_DONE), 1.0, 0.0)
        return thr, step * (0.5 ** SEARCH_BITS), rounds + 1, (jnp.min(settled) < 1.0).astype(jnp.int32)

    init = (jnp.zeros((n_rows, 1), F32), jnp.full((n_rows, 1), 2.0 * 0.5 ** SEARCH_BITS, F32),
            jnp.int32(0), jnp.int32(1))
    thr = lax.while_loop(lambda c: (c[3] > 0) & (c[2] < SEARCH_MAX_ROUNDS), search, init)[0]
    gt = (aff > thr).astype(F32)
    eq = (aff == thr).astype(F32)
    need = cap - jnp.sum(gt, axis=-1, keepdims=True)
    tri = tri_ref[...]
    sel = gt + eq * (_prefix_excl(eq, tri) < need).astype(F32)
    slot = jnp.where(sel > 0.0, _prefix_excl(sel, tri), -1.0)

    pos = pos_ref[...]
    srow = lax.broadcasted_iota(jnp.int32, (cap, T), 0).astype(F32)
    zero = jnp.zeros((3, T), BF16)
    for r in range(n_rows):
        i, e = divmod(r, N_EXPERTS)
        onehot = jnp.where(srow == slot[r:r + 1, :], 1.0, 0.0).astype(BF16)
        ah, am, al = _split3(aff[r:r + 1, :])
        vals = jnp.concatenate([pos, ah, am, al, zero], axis=0)
        res = _dot_nt(vals, onehot)
        idx_out[i, e:e + 1, :] = (res[0:1] * float(LANES) + res[1:2]).astype(jnp.int32)
        gate_out[i, e:e + 1, :] = res[2:3] + (res[3:4] + res[4:5])


def _select_call(aff, tri, pos, cap):
    B, T, _ = aff.shape
    n_s = 2 if B % 2 == 0 else 1
    return pl.pallas_call(
        functools.partial(_select_body, cap=cap),
        grid=(B // n_s,),
        in_specs=[pl.BlockSpec((n_s, T, LANES), lambda b: (b, 0, 0)), _full(tri.shape), _full(pos.shape)],
        out_specs=[pl.BlockSpec((n_s, N_EXPERTS, cap), lambda b: (b, 0, 0)),
                   pl.BlockSpec((n_s, N_EXPERTS, cap), lambda b: (b, 0, 0))],
        out_shape=[jax.ShapeDtypeStruct((B, N_EXPERTS, cap), jnp.int32),
                   jax.ShapeDtypeStruct((B, N_EXPERTS, cap), F32)],
        compiler_params=_params(("parallel",)),
        name="select",
    )(aff, tri, pos)


def _moe_body(idx_ref, gate_ref, h2_in, wg_ref, wu_ref, wd_ref, o_ref, h2_ref, acc_ref, xg_a, xg_b, ye_a,
              ye_b, *, cap, stride):
    k = pl.program_id(1)
    nchunk = D_MODEL // LANES
    group = 8

    def zero_of(tile):
        bits = pltpu.bitcast(tile, jnp.uint32)
        return pltpu.bitcast(lax.shift_right_logical(bits, jnp.uint32(32)), F32)

    def gather(e, xg_ref):
        def rows(s0, hold):
            for s in range(s0, s0 + group):
                base = pl.multiple_of(idx_ref[0, e, 0, s], SUBLANES)
                tile = h2_ref[pl.ds(base, nchunk), :] + hold
                xg_ref[pl.ds(s, nchunk, stride=stride), :] = tile
            return tile
        return [functools.partial(rows, s0) for s0 in range(0, cap, group)]

    def scatter(e, ye_ref):
        def rows(s0, hold):
            bases, news = [], []
            for s in range(s0, s0 + group):
                base = pl.multiple_of(idx_ref[0, e, 0, s], SUBLANES)
                bases.append(base)
                news.append(acc_ref[pl.ds(base, nchunk), :]
                            + (ye_ref[pl.ds(s, nchunk, stride=stride), :] + hold))
            for base, new in zip(bases, news):
                acc_ref[pl.ds(base, nchunk), :] = new
            return news[-1]
        return [functools.partial(rows, s0) for s0 in range(0, cap, group)]

    def expert(j, e, xg_ref, ye_ref):
        state = {}
        n_mid = D_EXPERT // MXU_TILE
        n_out = D_MODEL // MXU_TILE

        def load(zero):
            xb = jnp.concatenate([xg_ref[c * stride:c * stride + cap, :] for c in range(nchunk)], axis=1)
            state["x"] = xb.astype(BF16)
            state["gate"] = jnp.broadcast_to(gate_ref[0, e], (LANES, cap)).T[:, 0:1]
            state["h"] = []
            return xb[0:SUBLANES, 0:LANES]

        def mid(n, zero):
            cols = slice(n * MXU_TILE, (n + 1) * MXU_TILE)
            a = _dot(state["x"], wg_ref[j, :, cols]) + zero
            u = _dot(state["x"], wu_ref[j, :, cols])
            h = _silu(a) * u
            state["h"].append(h.astype(BF16))
            return h[0:SUBLANES, 0:LANES]

        def out(n, zero):
            cols = slice(n * MXU_TILE, (n + 1) * MXU_TILE)
            ye = _dot(state["h"][0], wd_ref[j, 0:MXU_TILE, cols])
            for m in range(1, n_mid):
                ye = ye + _dot(state["h"][m], wd_ref[j, m * MXU_TILE:(m + 1) * MXU_TILE, cols])
            ye = ye * state["gate"] + zero
            for c in range(MXU_TILE // LANES):
                cc = n * (MXU_TILE // LANES) + c
                ye_ref[cc * stride:cc * stride + cap, :] = ye[:, c * LANES:(c + 1) * LANES]
            return ye[0:SUBLANES, 0:LANES]

        return ([load] + [functools.partial(mid, n) for n in range(n_mid)]
                + [functools.partial(out, n) for n in range(n_out)])

    def paced(main, side):
        none = jnp.zeros((SUBLANES, LANES), F32)
        hold = none
        edges = [none, none, none]
        done = 0
        for i, piece in enumerate(main):
            zero = jnp.concatenate([edges[0][0:1, :]] * (MXU_TILE // LANES), axis=1)
            witness = piece(zero)
            upto = (len(side) * (i + 1)) // len(main)
            tiles = [s(hold) for s in side[done:upto]]
            done = upto
            hold = zero_of(witness)
            edges = edges[1:] + [zero_of(functools.reduce(jnp.add, tiles)) if tiles else none]

    def zipped(a, b):
        return [f for pair in zip(a, b) for f in pair]

    @pl.when(k == 0)
    def _():
        h2_ref[...] = h2_in[0].astype(F32)
        acc_ref[...] = jnp.zeros_like(acc_ref)
        ye_b[...] = jnp.zeros_like(ye_b)
        for piece in gather(0, xg_a):
            piece(jnp.zeros((SUBLANES, LANES), F32))

    n_experts = 2 * pl.num_programs(1)
    e0 = 2 * k
    e_prev = jnp.maximum(e0 - 1, 0)
    e_next = jnp.minimum(e0 + 2, n_experts - 1)
    paced(expert(0, e0, xg_a, ye_a), zipped(scatter(e_prev, ye_b), gather(e0 + 1, xg_b)))
    paced(expert(1, e0 + 1, xg_b, ye_b), zipped(scatter(e0, ye_a), gather(e_next, xg_a)))

    @pl.when(k == pl.num_programs(1) - 1)
    def _():
        for piece in scatter(n_experts - 1, ye_b):
            piece(jnp.zeros((SUBLANES, LANES), F32))
        o_ref[0] = acc_ref[...].astype(BF16)


def _moe_call(idx, gate, h2t, wg, wu, wd, T):
    B, E, cap = idx.shape
    nchunk = D_MODEL // LANES
    stride = cap + SUBLANES
    buf = pltpu.VMEM((nchunk * stride, LANES), F32)
    idx4 = idx.reshape(B, E, 1, cap)
    gate4 = gate.reshape(B, E, 1, cap)

    return pl.pallas_call(
        functools.partial(_moe_body, cap=cap, stride=stride),
        grid=(B, E // 2),
        in_specs=[
            pl.BlockSpec((1, E, 1, cap), lambda b, k: (b, 0, 0, 0), memory_space=pltpu.SMEM),
            pl.BlockSpec((1, E, 1, cap), lambda b, k: (b, 0, 0, 0)),
            pl.BlockSpec((1, T * nchunk, LANES), lambda b, k: (b, 0, 0)),
            pl.BlockSpec((2, D_MODEL, D_EXPERT), lambda b, k: (k, 0, 0)),
            pl.BlockSpec((2, D_MODEL, D_EXPERT), lambda b, k: (k, 0, 0)),
            pl.BlockSpec((2, D_EXPERT, D_MODEL), lambda b, k: (k, 0, 0)),
        ],
        out_specs=pl.BlockSpec((1, T * nchunk, LANES), lambda b, k: (b, 0, 0)),
        out_shape=jax.ShapeDtypeStruct((B, T * nchunk, LANES), BF16),
        scratch_shapes=[pltpu.VMEM((T * nchunk, LANES), F32), pltpu.VMEM((T * nchunk, LANES), F32),
                        buf, buf, buf, buf],
        compiler_params=_params(("parallel", "arbitrary")),
        name="moe",
    )(idx4, gate4, h2t, wg, wu, wd)


def _final_body(x1_ref, y_ref, g2_ref, o_ref, y32_ref):
    tm = x1_ref.shape[1]
    y32_ref[...] = y_ref[0].astype(F32)
    for c in range(D_MODEL // LANES):
        cols = slice(c * LANES, (c + 1) * LANES)
        y = y32_ref[pl.ds(c, tm, stride=D_MODEL // LANES), :]
        o_ref[0, :, cols] = x1_ref[0, :, cols] + g2_ref[0, :, cols] * y


def _final_call(x1, y, g2, tm):
    B, T, _ = x1.shape
    nchunk = D_MODEL // LANES
    tok = pl.BlockSpec((1, tm, D_MODEL), lambda b, j: (b, j, 0))
    return pl.pallas_call(
        _final_body,
        grid=(B, T // tm),
        in_specs=[tok,
                  pl.BlockSpec((1, tm * nchunk, LANES), lambda b, j: (b, j, 0)),
                  pl.BlockSpec((1, 1, D_MODEL), lambda b, j: (b, 0, 0))],
        out_specs=tok,
        out_shape=jax.ShapeDtypeStruct((B, T, D_MODEL), F32),
        scratch_shapes=[pltpu.VMEM((tm * nchunk, LANES), F32)],
        compiler_params=_params(("parallel", "parallel")),
        name="final",
    )(x1, y, g2)


def _slot_layout():
    half = QK_ROPE // 2
    n_freq = half // 2
    src = np.full((HEAD_SLOT,), -1, np.int64)
    for axis in range(2):
        first = QK_NOPE + axis * half
        src[axis * n_freq:(axis + 1) * n_freq] = np.arange(first, first + n_freq)
        src[LANES // 2 + axis * n_freq:LANES // 2 + (axis + 1) * n_freq] = np.arange(first + n_freq, first + half)
    src[half:half + QK_NOPE // 2] = np.arange(0, QK_NOPE // 2)
    src[LANES // 2 + half:LANES // 2 + half + QK_NOPE // 2] = np.arange(QK_NOPE // 2, QK_NOPE)
    assert src[SHIFT_LANE] == -1 and sorted(src[src >= 0]) == list(range(QK_DIM))
    return src


_SLOT_SRC = _slot_layout()


def _rope_tables(T):
    half = QK_ROPE // 2
    n_freq = half // 2
    inv_freq = 1.0 / (ROPE_THETA ** (np.arange(n_freq, dtype=np.float32) / n_freq))
    t = np.arange(T)
    cos = np.ones((T, HEAD_SLOT), np.float32)
    sin = np.zeros((T, HEAD_SLOT), np.float32)
    for axis, pos in enumerate((t // GRID_W, t % GRID_W)):
        ang = pos.astype(np.float32)[:, None] * inv_freq[None, :].astype(np.float32)
        c, s = np.cos(ang).astype(np.float32), np.sin(ang).astype(np.float32)
        lo1 = axis * n_freq
        lo2 = LANES // 2 + axis * n_freq
        cos[:, lo1:lo1 + n_freq] = c
        cos[:, lo2:lo2 + n_freq] = c
        sin[:, lo1:lo1 + n_freq] = -s
        sin[:, lo2:lo2 + n_freq] = s
    return jnp.asarray(cos), jnp.asarray(sin)


def _dft_tables(T):
    assert T == FFT_R * FFT_N
    n = np.arange(F_GROUP_DIM)
    ang_c = 2.0 * np.pi * ((n[:, None] * n[None, :]) % F_GROUP_DIM) / F_GROUP_DIM
    cc = np.concatenate([np.cos(ang_c), -np.sin(ang_c)], axis=1) / np.sqrt(F_GROUP_DIM)
    k1 = np.repeat(np.arange(FFT_R), FFT_N)
    n2 = np.tile(np.arange(FFT_N), FFT_R)
    ang_w = 2.0 * np.pi * (k1 * n2) / T
    twc = np.broadcast_to(np.cos(ang_w)[:, None], (T, F_GROUP_DIM))
    tws = np.broadcast_to(np.sin(ang_w)[:, None], (T, F_GROUP_DIM))
    m = np.arange(FFT_N)
    ang_2 = 2.0 * np.pi * ((m[:, None] * m[None, :]) % FFT_N) / FFT_N
    m2 = np.concatenate([np.cos(ang_2), np.sin(ang_2)], axis=1) / np.sqrt(T)
    return (jnp.asarray(cc, F32).astype(BF16), jnp.asarray(twc, F32), jnp.asarray(tws, F32),
            jnp.asarray(m2, F32).astype(BF16))


def kernel(x, c, ctx, c_ctx, w_mod, b_mod, norm1_g, w_in, q_a_norm_g, kv_a_norm_g, w_q_up, w_kv_up,
           q_norm_g, k_norm_g, w_o_attn, w_fourier, w_out, norm2_g, w_router, w_e_gate, w_e_up, w_e_down):
    B, T, D = x.shape
    assert w_mod.shape[0] == 1 and D == D_MODEL and T % 1024 == 0
    H = MLA_HEADS
    cap = (CAPACITY_FACTOR * T) // N_EXPERTS

    rows = -(-(B + 1) // SUBLANES) * SUBLANES
    cc_in = jnp.concatenate([c, c_ctx[None, :], jnp.zeros((rows - B - 1, D), F32)], axis=0)
    mod = _mod_call(cc_in, w_mod[0], b_mod)
    sh1, sc1, g1, sh2, sc2, g2 = [mod[:B, i * D:(i + 1) * D].reshape(B, 1, D) for i in range(6)]
    csh1 = mod[B:B + 1, 0:D].reshape(1, 1, D)
    csc1 = mod[B:B + 1, D:2 * D].reshape(1, 1, D)

    src = _SLOT_SRC
    used = jnp.asarray(src >= 0, F32)
    nope = jnp.asarray((src >= 0) & (src < QK_NOPE), F32)
    rope = jnp.asarray(src >= QK_NOPE, F32)
    lane_src = np.maximum(src, 0)
    wi = w_in[0]
    pe_cols = jnp.take(wi[:, OFF_KPE:OFF_F], np.maximum(src - QK_NOPE, 0), axis=1) * rope
    w_in_p = jnp.concatenate([wi[:, OFF_Q:OFF_KV], pe_cols, wi[:, OFF_KV:OFF_KPE], wi[:, OFF_F:N_IN]], axis=1)
    w_in_p = w_in_p.astype(BF16)
    wq_p = jnp.take(w_q_up[0].reshape(Q_LORA, H, QK_DIM), lane_src, axis=2) * used
    wq_p = wq_p.reshape(Q_LORA, H * HEAD_SLOT).astype(BF16)
    wkv = w_kv_up[0].reshape(KV_LORA, H, QK_NOPE + V_DIM)
    wk_p = jnp.take(wkv[:, :, :QK_NOPE], np.minimum(lane_src, QK_NOPE - 1), axis=2) * nope
    wk_p = wk_p.reshape(KV_LORA, H * HEAD_SLOT).astype(BF16)
    wv_p = wkv[:, :, QK_NOPE:].reshape(KV_LORA, H * V_DIM).astype(BF16)
    qg_p = (jnp.take(q_norm_g[0], lane_src) * used * (QK_DIM ** -0.5 * LOG2E)).reshape(1, HEAD_SLOT)
    kg_p = (jnp.take(k_norm_g[0], lane_src) * used).reshape(1, HEAD_SLOT)
    bound = (QK_DIM * BOUND_SLACK) * jnp.max(jnp.abs(qg_p)) * jnp.max(jnp.abs(kg_p))
    on_shift = jnp.asarray(np.arange(HEAD_SLOT) == SHIFT_LANE, F32).reshape(1, HEAD_SLOT)
    qadd = on_shift * (-bound)
    kadd = on_shift
    qag = q_a_norm_g[0].reshape(1, Q_LORA)
    kvag = kv_a_norm_g[0].reshape(1, KV_LORA)
    g1n = norm1_g[0].reshape(1, D)
    g2n = norm2_g[0].reshape(1, D)
    cos_t, sin_t = _rope_tables(T)

    q, k_lat, vt_lat, f, ga, gb = _proj_lat_call(x, g1n, sc1, sh1, w_in_p, qag, kvag, wq_p, wk_p, wv_p,
                                                 qg_p, kg_p, qadd, kadd, cos_t, sin_t, tm=512)
    k_ctx, vt_ctx = _proj_ctx_call(ctx, g1n, csc1, csh1, w_in_p, kvag, wk_p, wv_p, kg_p, kadd)
    attn = _attn_call(q, k_lat, k_ctx, vt_lat, vt_ctx, tq=1024)
    four = _fourier_call(f, *_dft_tables(T))

    wr = jnp.pad(w_router[0], ((0, 0), (0, LANES - N_EXPERTS)))
    wrh = wr.astype(BF16)
    wrl = wrh + jnp.roll((wr - wrh.astype(F32)).astype(BF16), N_EXPERTS, axis=1)
    x1, h2t, aff = _merge_call(attn, four, ga, gb, x, g1, g2n, sc2, sh2, w_o_attn[0].astype(BF16),
                               w_fourier[0].astype(BF16), w_out[0].astype(BF16), wrh, wrl, tm=512)

    tri = jnp.asarray(np.triu(np.ones((LANES, LANES), np.float32)), BF16)
    addr = np.arange(T) * (D // LANES)
    pos = jnp.asarray(np.stack([addr // LANES, addr % LANES]).astype(np.float32), BF16)
    idx, gate = _select_call(aff, tri, pos, cap)
    y = _moe_call(idx, gate, h2t, w_e_gate[0].astype(BF16), w_e_up[0].astype(BF16),
                  w_e_down[0].astype(BF16), T)
    return _final_call(x1, y, g2, tm=1024)
```

```python
import functools

import numpy as np
import jax
import jax.numpy as jnp
from jax import lax
from jax.experimental import pallas as pl
from jax.experimental.pallas import tpu as pltpu

F32 = jnp.float32
BF16 = jnp.bfloat16

D_MODEL = 1024
GRID_W = 64
MLA_HEADS = 8
QK_NOPE = 64
QK_ROPE = 32
QK_DIM = QK_NOPE + QK_ROPE
V_DIM = 64
Q_LORA = 384
KV_LORA = 256
ROPE_THETA = 10000.0
F_GROUPS = 4
F_GROUP_DIM = 128
D_F = F_GROUPS * F_GROUP_DIM
OFF_Q = 0
OFF_KV = OFF_Q + Q_LORA
OFF_KPE = OFF_KV + KV_LORA
OFF_F = OFF_KPE + QK_ROPE
OFF_GA = OFF_F + D_F
OFF_GB = OFF_GA + D_MODEL
N_IN = OFF_GB + D_MODEL
N_EXPERTS = 16
D_EXPERT = 512
CAPACITY_FACTOR = 2
EPS = 1e-6

LANES = 128
SUBLANES = 8
MXU_TILE = 256
HEAD_SLOT = LANES
SHIFT_LANE = 48
BOUND_SLACK = 1.02
DEN_FLOOR = 2.0 ** -60
LOG2E = 1.4426950408889634
FFT_R = 8
FFT_N = 256
SEARCH_BITS = 3
SEARCH_DONE = 2.0 ** -30
SEARCH_MAX_ROUNDS = 56
KEY_CHUNK = 256
VMEM_LIMIT = 56 * 1024 * 1024

PC_Q = 0
PC_PE = PC_Q + Q_LORA
PC_KV = PC_PE + HEAD_SLOT
PC_F = PC_KV + KV_LORA
PC_GA = PC_F + D_F
PC_GB = PC_GA + D_MODEL
PC_END = PC_GB + D_MODEL


def _dot(a, b):
    return jnp.dot(a, b, preferred_element_type=F32)


def _dot_nt(a, b):
    return lax.dot_general(a, b, (((1,), (1,)), ((), ())), preferred_element_type=F32)


def _split2(a):
    hi = a.astype(BF16)
    lo = (a - hi.astype(F32)).astype(BF16)
    return hi, lo


def _split3(a):
    hi = a.astype(BF16)
    r = a - hi.astype(F32)
    mid = r.astype(BF16)
    lo = (r - mid.astype(F32)).astype(BF16)
    return hi, mid, lo


def _dot3(a, b):
    ah, al = _split2(a)
    bh, bl = _split2(b)
    return _dot(ah, bh) + (_dot(ah, bl) + _dot(al, bh))


def _sigmoid(x):
    return 1.0 / (1.0 + jnp.exp(-x))


def _silu(x):
    return x * (0.5 * jnp.tanh(0.5 * x) + 0.5)


def _params(sem):
    return pltpu.CompilerParams(dimension_semantics=sem, vmem_limit_bytes=VMEM_LIMIT)


def _mod_body(c_ref, w_ref, b_ref, o_ref):
    c = c_ref[...]
    s = c * _sigmoid(c)
    o_ref[...] = _dot3(s, w_ref[...]) + b_ref[...]


def _mod_call(cc, w_mod, b_mod):
    rows = cc.shape[0]
    n = w_mod.shape[1]
    tn = 1024
    return pl.pallas_call(
        _mod_body,
        grid=(n // tn,),
        in_specs=[
            pl.BlockSpec((rows, D_MODEL), lambda j: (0, 0)),
            pl.BlockSpec((D_MODEL, tn), lambda j: (0, j)),
            pl.BlockSpec((1, tn), lambda j: (0, j)),
        ],
        out_specs=pl.BlockSpec((rows, tn), lambda j: (0, j)),
        out_shape=jax.ShapeDtypeStruct((rows, n), F32),
        compiler_params=_params(("arbitrary",)),
        name="mod",
    )(cc, w_mod, b_mod)


def _rms_rows(x, n):
    return x * lax.rsqrt(jnp.sum(x * x, axis=-1, keepdims=True) * (1.0 / n) + EPS)


def _rope(x, cos, sin):
    return x * cos + pltpu.roll(x, LANES // 2, 1) * sin


def _proj_body(*refs, latent):
    if latent:
        (x_ref, g_ref, sc_ref, sh_ref, win_ref, qag_ref, kvag_ref, wq_ref, wk_ref, wv_ref,
         qg_ref, kg_ref, qadd_ref, kadd_ref, cos_ref, sin_ref,
         q_out, k_out, v_out, f_out, ga_out, gb_out) = refs
    else:
        (x_ref, g_ref, sc_ref, sh_ref, win_ref, kvag_ref, wk_ref, wv_ref, kg_ref, kadd_ref,
         k_out, v_out) = refs
    n_s, rows = x_ref.shape[0], x_ref.shape[1]
    x = x_ref[...].reshape(n_s * rows, D_MODEL)
    h = _rms_rows(x, D_MODEL) * (g_ref[...] * (1.0 + sc_ref[0])) + sh_ref[0]
    hb = h.astype(BF16)

    if latent:
        cos = cos_ref[...]
        sin = sin_ref[...]

    pe = _dot(hb, win_ref[:, PC_PE:PC_KV])

    ckv = _rms_rows(_dot(hb, win_ref[:, PC_KV:PC_F]), KV_LORA) * kvag_ref[...]
    ckvb = ckv.astype(BF16)
    kall = _dot(ckvb, wk_ref[...])
    kg = kg_ref[...]
    kadd = kadd_ref[...]
    for hd in range(MLA_HEADS):
        kh = kall[:, hd * HEAD_SLOT:(hd + 1) * HEAD_SLOT] + pe
        kh = _rms_rows(kh, QK_DIM) * kg
        if latent:
            kh = _rope(kh, cos, sin)
        kh = (kh + kadd).astype(BF16)
        for i in range(n_s):
            k_out[i, hd] = kh[i * rows:(i + 1) * rows]
    v = _dot(ckvb, wv_ref[...])
    for hp in range(MLA_HEADS // 2):
        vt = v[:, hp * LANES:(hp + 1) * LANES].T.astype(BF16)
        for i in range(n_s):
            v_out[i, hp] = vt[:, i * rows:(i + 1) * rows]

    if latent:
        cq = _rms_rows(_dot(hb, win_ref[:, PC_Q:PC_PE]), Q_LORA) * qag_ref[...]
        qall = _dot(cq.astype(BF16), wq_ref[...])
        qg = qg_ref[...]
        qadd = qadd_ref[...]
        for hd in range(MLA_HEADS):
            qh = qall[:, hd * HEAD_SLOT:(hd + 1) * HEAD_SLOT]
            q_out[0, hd] = (_rope(_rms_rows(qh, QK_DIM) * qg, cos, sin) + qadd).astype(BF16)
        f_out[0] = _dot(hb, win_ref[:, PC_F:PC_GA]).astype(BF16)
        ga_out[0] = _sigmoid(_dot(hb, win_ref[:, PC_GA:PC_GB])).astype(BF16)
        gb_out[0] = _sigmoid(_dot(hb, win_ref[:, PC_GB:PC_END])).astype(BF16)


def _full(shape):
    nd = len(shape)
    return pl.BlockSpec(shape, lambda *_: (0,) * nd)


def _proj_lat_call(x, g1n, sc1, sh1, w_in_p, qag, kvag, wq_p, wk_p, wv_p, qg_p, kg_p, qadd, kadd, cos_t, sin_t, tm):
    B, T, _ = x.shape
    H = MLA_HEADS
    tok = lambda w: pl.BlockSpec((1, tm, w), lambda b, j: (b, j, 0))
    per_b = pl.BlockSpec((1, 1, D_MODEL), lambda b, j: (b, 0, 0))
    tab = pl.BlockSpec((tm, HEAD_SLOT), lambda b, j: (j, 0))
    return pl.pallas_call(
        functools.partial(_proj_body, latent=True),
        grid=(B, T // tm),
        in_specs=[tok(D_MODEL), _full(g1n.shape), per_b, per_b, _full(w_in_p.shape), _full(qag.shape),
                  _full(kvag.shape), _full(wq_p.shape), _full(wk_p.shape), _full(wv_p.shape),
                  _full(qg_p.shape), _full(kg_p.shape), _full(qadd.shape), _full(kadd.shape), tab, tab],
        out_specs=[
            pl.BlockSpec((1, H, tm, HEAD_SLOT), lambda b, j: (b, 0, j, 0)),
            pl.BlockSpec((1, H, tm, HEAD_SLOT), lambda b, j: (b, 0, j, 0)),
            pl.BlockSpec((1, H // 2, LANES, tm), lambda b, j: (b, 0, 0, j)),
            tok(D_F), tok(D_MODEL), tok(D_MODEL),
        ],
        out_shape=[
            jax.ShapeDtypeStruct((B, H, T, HEAD_SLOT), BF16),
            jax.ShapeDtypeStruct((B, H, T, HEAD_SLOT), BF16),
            jax.ShapeDtypeStruct((B, H // 2, LANES, T), BF16),
            jax.ShapeDtypeStruct((B, T, D_F), BF16),
            jax.ShapeDtypeStruct((B, T, D_MODEL), BF16),
            jax.ShapeDtypeStruct((B, T, D_MODEL), BF16),
        ],
        compiler_params=_params(("parallel", "parallel")),
        name="proj_lat",
    )(x, g1n, sc1, sh1, w_in_p, qag, kvag, wq_p, wk_p, wv_p, qg_p, kg_p, qadd, kadd, cos_t, sin_t)


def _proj_ctx_call(ctx, g1n, csc1, csh1, w_in_p, kvag, wk_p, wv_p, kg_p, kadd):
    B, TC, _ = ctx.shape
    H = MLA_HEADS
    shared = pl.BlockSpec((1, 1, D_MODEL), lambda b: (0, 0, 0))
    n_s = 4 if B % 4 == 0 else 1
    return pl.pallas_call(
        functools.partial(_proj_body, latent=False),
        grid=(B // n_s,),
        in_specs=[pl.BlockSpec((n_s, TC, D_MODEL), lambda b: (b, 0, 0)), _full(g1n.shape), shared, shared,
                  _full(w_in_p.shape), _full(kvag.shape), _full(wk_p.shape), _full(wv_p.shape),
                  _full(kg_p.shape), _full(kadd.shape)],
        out_specs=[
            pl.BlockSpec((n_s, H, TC, HEAD_SLOT), lambda b: (b, 0, 0, 0)),
            pl.BlockSpec((n_s, H // 2, LANES, TC), lambda b: (b, 0, 0, 0)),
        ],
        out_shape=[
            jax.ShapeDtypeStruct((B, H, TC, HEAD_SLOT), BF16),
            jax.ShapeDtypeStruct((B, H // 2, LANES, TC), BF16),
        ],
        compiler_params=_params(("parallel",)),
        name="proj_ctx",
    )(ctx, g1n, csc1, csh1, w_in_p, kvag, wk_p, wv_p, kg_p, kadd)


def _attn_body(q_ref, kl_ref, kc_ref, vl_ref, vc_ref, o_ref):
    tq = q_ref.shape[2]
    T = kl_ref.shape[2]
    nck = T // KEY_CHUNK
    row = lax.broadcasted_iota(jnp.int32, (LANES, tq), 0)

    def probs(hd):
        q = q_ref[0, hd]
        den = jnp.zeros((1, tq), F32)
        ps = []
        for c in range(nck + 1):
            k = kc_ref[0, hd] if c == nck else kl_ref[0, hd, c * KEY_CHUNK:(c + 1) * KEY_CHUNK, :]
            e = jnp.exp2(_dot_nt(k, q))
            den = den + jnp.sum(e, axis=0, keepdims=True)
            ps.append(e.astype(BF16))
        return ps, den

    def weighted_values(hd, ps):
        hp = hd // 2
        o = _dot(vc_ref[0, hp], ps[nck])
        for c in range(nck):
            o = o + _dot(vl_ref[0, hp, :, c * KEY_CHUNK:(c + 1) * KEY_CHUNK], ps[c])
        return o

    outs, dens, prev = [], [], None
    for hd in range(MLA_HEADS + 1):
        cur = probs(hd) if hd < MLA_HEADS else None
        if prev is not None:
            outs.append(weighted_values(hd - 1, prev[0]) / prev[1])
            dens.append(prev[1])
        prev = cur
    for hp in range(MLA_HEADS // 2):
        o_ref[0, hp] = jnp.where(row < V_DIM, outs[2 * hp], outs[2 * hp + 1]).astype(BF16)
    min_den = functools.reduce(jnp.minimum, dens)

    @pl.when(jnp.min(min_den) < DEN_FLOOR)
    def _():
        def pair(hp, carry):
            res = []
            for hd in (2 * hp, 2 * hp + 1):
                q = q_ref[0, hd]
                sl = _dot_nt(kl_ref[0, hd], q)
                sc = _dot_nt(kc_ref[0, hd], q)
                m = jnp.maximum(jnp.max(sl, axis=0, keepdims=True), jnp.max(sc, axis=0, keepdims=True))
                el = jnp.exp2(sl - m)
                ec = jnp.exp2(sc - m)
                den = jnp.sum(el, axis=0, keepdims=True) + jnp.sum(ec, axis=0, keepdims=True)
                o = _dot(vl_ref[0, hp], el.astype(BF16)) + _dot(vc_ref[0, hp], ec.astype(BF16))
                res.append(o / den)
            o_ref[0, hp] = jnp.where(row < V_DIM, res[0], res[1]).astype(BF16)
            return carry

        lax.fori_loop(0, MLA_HEADS // 2, pair, 0)


def _attn_call(q, k_lat, k_ctx, vt_lat, vt_ctx, tq):
    B, H, T, _ = q.shape
    TC = k_ctx.shape[2]
    return pl.pallas_call(
        _attn_body,
        grid=(B, T // tq),
        in_specs=[
            pl.BlockSpec((1, H, tq, HEAD_SLOT), lambda b, j: (b, 0, j, 0)),
            pl.BlockSpec((1, H, T, HEAD_SLOT), lambda b, j: (b, 0, 0, 0)),
            pl.BlockSpec((1, H, TC, HEAD_SLOT), lambda b, j: (b, 0, 0, 0)),
            pl.BlockSpec((1, H // 2, LANES, T), lambda b, j: (b, 0, 0, 0)),
            pl.BlockSpec((1, H // 2, LANES, TC), lambda b, j: (b, 0, 0, 0)),
        ],
        out_specs=pl.BlockSpec((1, H // 2, LANES, tq), lambda b, j: (b, 0, 0, j)),
        out_shape=jax.ShapeDtypeStruct((B, H // 2, LANES, T), BF16),
        compiler_params=_params(("parallel", "parallel")),
        name="attn",
    )(q, k_lat, k_ctx, vt_lat, vt_ctx)


def _cadd(a, b):
    return a[0] + b[0], a[1] + b[1]


def _csub(a, b):
    return a[0] - b[0], a[1] - b[1]


def _cmul_neg_i(a):
    return a[1], -a[0]


def _cmul_pos_i(a):
    return -a[1], a[0]


def _fft8(u):
    r = 0.7071067811865476
    a0, a1 = _cadd(u[0], u[4]), _csub(u[0], u[4])
    a2, a3 = _cadd(u[2], u[6]), _csub(u[2], u[6])
    a4, a5 = _cadd(u[1], u[5]), _csub(u[1], u[5])
    a6, a7 = _cadd(u[3], u[7]), _csub(u[3], u[7])
    e0, e2 = _cadd(a0, a2), _csub(a0, a2)
    e1, e3 = _cadd(a1, _cmul_neg_i(a3)), _cadd(a1, _cmul_pos_i(a3))
    o0, o2 = _cadd(a4, a6), _csub(a4, a6)
    o1, o3 = _cadd(a5, _cmul_neg_i(a7)), _cadd(a5, _cmul_pos_i(a7))
    wo1 = (r * (o1[0] + o1[1]), r * (o1[1] - o1[0]))
    wo3 = (r * (o3[1] - o3[0]), -r * (o3[0] + o3[1]))
    return [_cadd(e0, o0), _cadd(e1, wo1), _cadd(e2, _cmul_neg_i(o2)), _cadd(e3, wo3),
            _csub(e0, o0), _csub(e1, wo1), _cadd(e2, _cmul_pos_i(o2)), _csub(e3, wo3)]


def _fourier_body(f_ref, cc_ref, twc_ref, tws_ref, m2_ref, o_ref, b_ref, y_ref):
    for g in range(F_GROUPS):
        cols = slice(g * F_GROUP_DIM, (g + 1) * F_GROUP_DIM)
        y = _dot(f_ref[0, :, cols], cc_ref[...])
        u = [(y[j * FFT_N:(j + 1) * FFT_N, :F_GROUP_DIM], y[j * FFT_N:(j + 1) * FFT_N, F_GROUP_DIM:])
             for j in range(FFT_R)]
        a = _fft8(u)
        for k1 in range(FFT_R):
            ar, ai = a[k1]
            if k1 > 0:
                c = twc_ref[k1 * FFT_N:(k1 + 1) * FFT_N, :]
                s = tws_ref[k1 * FFT_N:(k1 + 1) * FFT_N, :]
                ar, ai = ar * c + ai * s, ai * c - ar * s
            b_ref[k1, 0:FFT_N, cols] = ar.astype(BF16)
            b_ref[k1, FFT_N:2 * FFT_N, cols] = ai.astype(BF16)
    for k1 in range(FFT_R):
        y = _dot(m2_ref[...], b_ref[k1])
        for g in range(F_GROUPS):
            y_ref[g, pl.ds(k1, FFT_N, stride=FFT_R), :] = y[:, g * F_GROUP_DIM:(g + 1) * F_GROUP_DIM]
    for g in range(F_GROUPS):
        o_ref[0, :, g * F_GROUP_DIM:(g + 1) * F_GROUP_DIM] = y_ref[g].astype(BF16)


def _fourier_call(f, cc, twc, tws, m2):
    B, T, _ = f.shape
    return pl.pallas_call(
        _fourier_body,
        grid=(B,),
        in_specs=[pl.BlockSpec((1, T, D_F), lambda b: (b, 0, 0)), _full(cc.shape), _full(twc.shape),
                  _full(tws.shape), _full(m2.shape)],
        out_specs=pl.BlockSpec((1, T, D_F), lambda b: (b, 0, 0)),
        out_shape=jax.ShapeDtypeStruct((B, T, D_F), BF16),
        scratch_shapes=[pltpu.VMEM((FFT_R, 2 * FFT_N, D_F), BF16), pltpu.VMEM((F_GROUPS, T, F_GROUP_DIM), F32)],
        compiler_params=_params(("parallel",)),
        name="fourier",
    )(f, cc, twc, tws, m2)


def _merge_body(at_ref, fo_ref, ga_ref, gb_ref, x_ref, g1_ref, g2n_ref, sc2_ref, sh2_ref,
                wo_ref, wf_ref, wout_ref, wrh_ref, wrl_ref, x1_out, h2_out, aff_out, h2_tm):
    tm = x_ref.shape[1]
    attn_t = jnp.concatenate([at_ref[0, hp] for hp in range(MLA_HEADS // 2)], axis=0)
    a = lax.dot_general(attn_t, wo_ref[...], (((0,), (0,)), ((), ())), preferred_element_type=F32)
    fo = _dot(fo_ref[0], wf_ref[...])
    mix = ga_ref[0].astype(F32) * a + gb_ref[0].astype(F32) * fo
    y = _dot(mix.astype(BF16), wout_ref[...])
    x1 = x_ref[0] + g1_ref[0] * y
    x1_out[0] = x1
    h2 = _rms_rows(x1, D_MODEL) * g2n_ref[...]
    h2 = h2 * (1.0 + sc2_ref[0]) + sh2_ref[0]
    for c in range(D_MODEL // LANES):
        h2_tm[pl.ds(c, tm, stride=D_MODEL // LANES), :] = h2[:, c * LANES:(c + 1) * LANES]
    h2_out[0] = h2_tm[...].astype(BF16)
    hh, hl = _split2(h2)
    p_hi = _dot(hh, wrl_ref[...])
    logits = p_hi + pltpu.roll(p_hi, LANES - N_EXPERTS, 1) + _dot(hl, wrh_ref[...])
    lane = lax.broadcasted_iota(jnp.int32, logits.shape, 1)
    logits = jnp.where(lane < N_EXPERTS, logits, -1e30)
    ex = jnp.exp(logits - jnp.max(logits, axis=-1, keepdims=True))
    aff_out[0] = ex / jnp.sum(ex, axis=-1, keepdims=True)


def _merge_call(attn, four, ga, gb, x, g1, g2n, sc2, sh2, wo, wf, wout, wrh, wrl, tm):
    B, T, _ = x.shape
    H = MLA_HEADS
    tok = lambda w: pl.BlockSpec((1, tm, w), lambda b, j: (b, j, 0))
    per_b = pl.BlockSpec((1, 1, D_MODEL), lambda b, j: (b, 0, 0))
    nchunk = D_MODEL // LANES
    return pl.pallas_call(
        _merge_body,
        grid=(B, T // tm),
        in_specs=[
            pl.BlockSpec((1, H // 2, LANES, tm), lambda b, j: (b, 0, 0, j)),
            tok(D_F), tok(D_MODEL), tok(D_MODEL), tok(D_MODEL), per_b, _full(g2n.shape), per_b, per_b,
            _full(wo.shape), _full(wf.shape), _full(wout.shape), _full(wrh.shape), _full(wrl.shape),
        ],
        out_specs=[
            tok(D_MODEL),
            pl.BlockSpec((1, tm * nchunk, LANES), lambda b, j: (b, j, 0)),
            tok(LANES),
        ],
        out_shape=[
            jax.ShapeDtypeStruct((B, T, D_MODEL), F32),
            jax.ShapeDtypeStruct((B, T * nchunk, LANES), BF16),
            jax.ShapeDtypeStruct((B, T, LANES), F32),
        ],
        scratch_shapes=[pltpu.VMEM((tm * nchunk, LANES), F32)],
        compiler_params=_params(("parallel", "parallel")),
        name="merge",
    )(attn, four, ga, gb, x, g1, g2n, sc2, sh2, wo, wf, wout, wrh, wrl)


def _prefix_excl(m, tri):
    outs = []
    run = jnp.zeros((m.shape[0], 1), F32)
    for blk in range(m.shape[1] // LANES):
        mb = m[:, blk * LANES:(blk + 1) * LANES]
        inc = _dot(mb.astype(BF16), tri)
        outs.append(inc - mb + run)
        run = run + inc[:, LANES - 1:LANES]
    return jnp.concatenate(outs, axis=1)


def _select_body(aff_ref, tri_ref, pos_ref, idx_out, gate_out, *, cap):
    n_s, T = aff_ref.shape[0], aff_ref.shape[1]
    n_rows = n_s * N_EXPERTS
    aff = jnp.concatenate([aff_ref[i].T[0:N_EXPERTS] for i in range(n_s)], axis=0)

    def search(carry):
        base, step, rounds, _ = carry
        thr = base
        for j in range(1, 2 ** SEARCH_BITS):
            cand = base + float(j) * step
            cnt = jnp.sum((aff >= cand).astype(F32), axis=-1, keepdims=True)
            thr = jnp.where(cnt >= float(cap), cand, thr)
        settled = jnp.where((thr > 0.0) & (step < thr * SEARCH_DONE), 1.0, 0.0)
        return thr, step * (0.5 ** SEARCH_BITS), rounds + 1, (jnp.min(settled) < 1.0).astype(jnp.int32)

    init = (jnp.zeros((n_rows, 1), F32), jnp.full((n_rows, 1), 2.0 * 0.5 ** SEARCH_BITS, F32),
            jnp.int32(0), jnp.int32(1))
    thr = lax.while_loop(lambda c: (c[3] > 0) & (c[2] < SEARCH_MAX_ROUNDS), search, init)[0]
    gt = (aff > thr).astype(F32)
    eq = (aff == thr).astype(F32)
    need = cap - jnp.sum(gt, axis=-1, keepdims=True)
    tri = tri_ref[...]
    sel = gt + eq * (_prefix_excl(eq, tri) < need).astype(F32)
    slot = jnp.where(sel > 0.0, _prefix_excl(sel, tri), -1.0)

    pos = pos_ref[...]
    srow = lax.broadcasted_iota(jnp.int32, (cap, T), 0).astype(F32)
    zero = jnp.zeros((3, T), BF16)
    for r in range(n_rows):
        i, e = divmod(r, N_EXPERTS)
        onehot = jnp.where(srow == slot[r:r + 1, :], 1.0, 0.0).astype(BF16)
        ah, am, al = _split3(aff[r:r + 1, :])
        vals = jnp.concatenate([pos, ah, am, al, zero], axis=0)
        res = _dot_nt(vals, onehot)
        idx_out[i, e:e + 1, :] = (res[0:1] * float(LANES) + res[1:2]).astype(jnp.int32)
        gate_out[i, e:e + 1, :] = res[2:3] + (res[3:4] + res[4:5])


def _select_call(aff, tri, pos, cap):
    B, T, _ = aff.shape
    n_s = 4 if B % 4 == 0 else (2 if B % 2 == 0 else 1)
    return pl.pallas_call(
        functools.partial(_select_body, cap=cap),
        grid=(B // n_s,),
        in_specs=[pl.BlockSpec((n_s, T, LANES), lambda b: (b, 0, 0)), _full(tri.shape), _full(pos.shape)],
        out_specs=[pl.BlockSpec((n_s, N_EXPERTS, cap), lambda b: (b, 0, 0)),
                   pl.BlockSpec((n_s, N_EXPERTS, cap), lambda b: (b, 0, 0))],
        out_shape=[jax.ShapeDtypeStruct((B, N_EXPERTS, cap), jnp.int32),
                   jax.ShapeDtypeStruct((B, N_EXPERTS, cap), F32)],
        compiler_params=_params(("parallel",)),
        name="select",
    )(aff, tri, pos)


def _moe_body(idx_ref, gate_ref, h2_in, wg_ref, wu_ref, wd_ref, o_ref, h2_ref, acc_ref, xg_a, xg_b, ye_a,
              ye_b, *, cap, stride):
    k = pl.program_id(1)
    nchunk = D_MODEL // LANES
    group = 8

    def zero_of(tile):
        bits = pltpu.bitcast(tile, jnp.uint32)
        return pltpu.bitcast(lax.shift_right_logical(bits, jnp.uint32(32)), F32)

    def gather(e, xg_ref):
        def rows(s0, hold):
            for s in range(s0, s0 + group):
                base = pl.multiple_of(idx_ref[0, e, 0, s], SUBLANES)
                tile = h2_ref[pl.ds(base, nchunk), :] + hold
                xg_ref[pl.ds(s, nchunk, stride=stride), :] = tile
            return tile
        return [functools.partial(rows, s0) for s0 in range(0, cap, group)]

    def scatter(e, ye_ref):
        def rows(s0, hold):
            bases, news = [], []
            for s in range(s0, s0 + group):
                base = pl.multiple_of(idx_ref[0, e, 0, s], SUBLANES)
                bases.append(base)
                news.append(acc_ref[pl.ds(base, nchunk), :]
                            + (ye_ref[pl.ds(s, nchunk, stride=stride), :] + hold))
            for base, new in zip(bases, news):
                acc_ref[pl.ds(base, nchunk), :] = new
            return news[-1]
        return [functools.partial(rows, s0) for s0 in range(0, cap, group)]

    def expert(j, e, xg_ref, ye_ref):
        state = {}
        n_mid = D_EXPERT // MXU_TILE
        n_out = D_MODEL // MXU_TILE

        def load(zero):
            xb = jnp.concatenate([xg_ref[c * stride:c * stride + cap, :] for c in range(nchunk)], axis=1)
            state["x"] = xb.astype(BF16)
            state["gate"] = jnp.broadcast_to(gate_ref[0, e], (LANES, cap)).T[:, 0:1]
            state["h"] = []
            return xb[0:SUBLANES, 0:LANES]

        def mid(n, zero):
            cols = slice(n * MXU_TILE, (n + 1) * MXU_TILE)
            a = _dot(state["x"], wg_ref[j, :, cols]) + zero
            u = _dot(state["x"], wu_ref[j, :, cols])
            h = _silu(a) * u
            state["h"].append(h.astype(BF16))
            return h[0:SUBLANES, 0:LANES]

        def out(n, zero):
            cols = slice(n * MXU_TILE, (n + 1) * MXU_TILE)
            ye = _dot(state["h"][0], wd_ref[j, 0:MXU_TILE, cols])
            for m in range(1, n_mid):
                ye = ye + _dot(state["h"][m], wd_ref[j, m * MXU_TILE:(m + 1) * MXU_TILE, cols])
            ye = ye * state["gate"] + zero
            for c in range(MXU_TILE // LANES):
                cc = n * (MXU_TILE // LANES) + c
                ye_ref[cc * stride:cc * stride + cap, :] = ye[:, c * LANES:(c + 1) * LANES]
            return ye[0:SUBLANES, 0:LANES]

        return ([load] + [functools.partial(mid, n) for n in range(n_mid)]
                + [functools.partial(out, n) for n in range(n_out)])

    def paced(main, side):
        none = jnp.zeros((SUBLANES, LANES), F32)
        hold = none
        edges = [none, none, none]
        done = 0
        for i, piece in enumerate(main):
            zero = jnp.concatenate([edges[0][0:1, :]] * (MXU_TILE // LANES), axis=1)
            witness = piece(zero)
            upto = (len(side) * (i + 1)) // len(main)
            tiles = [s(hold) for s in side[done:upto]]
            done = upto
            hold = zero_of(witness)
            edges = edges[1:] + [zero_of(functools.reduce(jnp.add, tiles)) if tiles else none]

    def zipped(a, b):
        return [f for pair in zip(a, b) for f in pair]

    @pl.when(k == 0)
    def _():
        h2_ref[...] = h2_in[0].astype(F32)
        acc_ref[...] = jnp.zeros_like(acc_ref)
        ye_b[...] = jnp.zeros_like(ye_b)
        for piece in gather(0, xg_a):
            piece(jnp.zeros((SUBLANES, LANES), F32))

    n_experts = 2 * pl.num_programs(1)
    e0 = 2 * k
    e_prev = jnp.maximum(e0 - 1, 0)
    e_next = jnp.minimum(e0 + 2, n_experts - 1)
    paced(expert(0, e0, xg_a, ye_a), zipped(scatter(e_prev, ye_b), gather(e0 + 1, xg_b)))
    paced(expert(1, e0 + 1, xg_b, ye_b), zipped(scatter(e0, ye_a), gather(e_next, xg_a)))

    @pl.when(k == pl.num_programs(1) - 1)
    def _():
        for piece in scatter(n_experts - 1, ye_b):
            piece(jnp.zeros((SUBLANES, LANES), F32))
        o_ref[0] = acc_ref[...].astype(BF16)


def _moe_call(idx, gate, h2t, wg, wu, wd, T):
    B, E, cap = idx.shape
    nchunk = D_MODEL // LANES
    stride = cap + SUBLANES
    buf = pltpu.VMEM((nchunk * stride, LANES), F32)
    idx4 = idx.reshape(B, E, 1, cap)
    gate4 = gate.reshape(B, E, 1, cap)

    return pl.pallas_call(
        functools.partial(_moe_body, cap=cap, stride=stride),
        grid=(B, E // 2),
        in_specs=[
            pl.BlockSpec((1, E, 1, cap), lambda b, k: (b, 0, 0, 0), memory_space=pltpu.SMEM),
            pl.BlockSpec((1, E, 1, cap), lambda b, k: (b, 0, 0, 0)),
            pl.BlockSpec((1, T * nchunk, LANES), lambda b, k: (b, 0, 0)),
            pl.BlockSpec((2, D_MODEL, D_EXPERT), lambda b, k: (k, 0, 0)),
            pl.BlockSpec((2, D_MODEL, D_EXPERT), lambda b, k: (k, 0, 0)),
            pl.BlockSpec((2, D_EXPERT, D_MODEL), lambda b, k: (k, 0, 0)),
        ],
        out_specs=pl.BlockSpec((1, T * nchunk, LANES), lambda b, k: (b, 0, 0)),
        out_shape=jax.ShapeDtypeStruct((B, T * nchunk, LANES), BF16),
        scratch_shapes=[pltpu.VMEM((T * nchunk, LANES), F32), pltpu.VMEM((T * nchunk, LANES), F32),
                        buf, buf, buf, buf],
        compiler_params=_params(("parallel", "arbitrary")),
        name="moe",
    )(idx4, gate4, h2t, wg, wu, wd)


def _final_body(x1_ref, y_ref, g2_ref, o_ref, y32_ref):
    tm = x1_ref.shape[1]
    y32_ref[...] = y_ref[0].astype(F32)
    for c in range(D_MODEL // LANES):
        cols = slice(c * LANES, (c + 1) * LANES)
        y = y32_ref[pl.ds(c, tm, stride=D_MODEL // LANES), :]
        o_ref[0, :, cols] = x1_ref[0, :, cols] + g2_ref[0, :, cols] * y


def _final_call(x1, y, g2, tm):
    B, T, _ = x1.shape
    nchunk = D_MODEL // LANES
    tok = pl.BlockSpec((1, tm, D_MODEL), lambda b, j: (b, j, 0))
    return pl.pallas_call(
        _final_body,
        grid=(B, T // tm),
        in_specs=[tok,
                  pl.BlockSpec((1, tm * nchunk, LANES), lambda b, j: (b, j, 0)),
                  pl.BlockSpec((1, 1, D_MODEL), lambda b, j: (b, 0, 0))],
        out_specs=tok,
        out_shape=jax.ShapeDtypeStruct((B, T, D_MODEL), F32),
        scratch_shapes=[pltpu.VMEM((tm * nchunk, LANES), F32)],
        compiler_params=_params(("parallel", "parallel")),
        name="final",
    )(x1, y, g2)


def _slot_layout():
    half = QK_ROPE // 2
    n_freq = half // 2
    src = np.full((HEAD_SLOT,), -1, np.int64)
    for axis in range(2):
        first = QK_NOPE + axis * half
        src[axis * n_freq:(axis + 1) * n_freq] = np.arange(first, first + n_freq)
        src[LANES // 2 + axis * n_freq:LANES // 2 + (axis + 1) * n_freq] = np.arange(first + n_freq, first + half)
    src[half:half + QK_NOPE // 2] = np.arange(0, QK_NOPE // 2)
    src[LANES // 2 + half:LANES // 2 + half + QK_NOPE // 2] = np.arange(QK_NOPE // 2, QK_NOPE)
    assert src[SHIFT_LANE] == -1 and sorted(src[src >= 0]) == list(range(QK_DIM))
    return src


_SLOT_SRC = _slot_layout()


def _rope_tables(T):
    half = QK_ROPE // 2
    n_freq = half // 2
    inv_freq = 1.0 / (ROPE_THETA ** (np.arange(n_freq, dtype=np.float32) / n_freq))
    t = np.arange(T)
    cos = np.ones((T, HEAD_SLOT), np.float32)
    sin = np.zeros((T, HEAD_SLOT), np.float32)
    for axis, pos in enumerate((t // GRID_W, t % GRID_W)):
        ang = pos.astype(np.float32)[:, None] * inv_freq[None, :].astype(np.float32)
        c, s = np.cos(ang).astype(np.float32), np.sin(ang).astype(np.float32)
        lo1 = axis * n_freq
        lo2 = LANES // 2 + axis * n_freq
        cos[:, lo1:lo1 + n_freq] = c
        cos[:, lo2:lo2 + n_freq] = c
        sin[:, lo1:lo1 + n_freq] = -s
        sin[:, lo2:lo2 + n_freq] = s
    return jnp.asarray(cos), jnp.asarray(sin)


def _dft_tables(T):
    assert T == FFT_R * FFT_N
    n = np.arange(F_GROUP_DIM)
    ang_c = 2.0 * np.pi * ((n[:, None] * n[None, :]) % F_GROUP_DIM) / F_GROUP_DIM
    cc = np.concatenate([np.cos(ang_c), -np.sin(ang_c)], axis=1) / np.sqrt(F_GROUP_DIM)
    k1 = np.repeat(np.arange(FFT_R), FFT_N)
    n2 = np.tile(np.arange(FFT_N), FFT_R)
    ang_w = 2.0 * np.pi * (k1 * n2) / T
    twc = np.broadcast_to(np.cos(ang_w)[:, None], (T, F_GROUP_DIM))
    tws = np.broadcast_to(np.sin(ang_w)[:, None], (T, F_GROUP_DIM))
    m = np.arange(FFT_N)
    ang_2 = 2.0 * np.pi * ((m[:, None] * m[None, :]) % FFT_N) / FFT_N
    m2 = np.concatenate([np.cos(ang_2), np.sin(ang_2)], axis=1) / np.sqrt(T)
    return (jnp.asarray(cc, F32).astype(BF16), jnp.asarray(twc, F32), jnp.asarray(tws, F32),
            jnp.asarray(m2, F32).astype(BF16))


def kernel(x, c, ctx, c_ctx, w_mod, b_mod, norm1_g, w_in, q_a_norm_g, kv_a_norm_g, w_q_up, w_kv_up,
           q_norm_g, k_norm_g, w_o_attn, w_fourier, w_out, norm2_g, w_router, w_e_gate, w_e_up, w_e_down):
    B, T, D = x.shape
    assert w_mod.shape[0] == 1 and D == D_MODEL and T % 1024 == 0
    H = MLA_HEADS
    cap = (CAPACITY_FACTOR * T) // N_EXPERTS

    rows = -(-(B + 1) // SUBLANES) * SUBLANES
    cc_in = jnp.concatenate([c, c_ctx[None, :], jnp.zeros((rows - B - 1, D), F32)], axis=0)
    mod = _mod_call(cc_in, w_mod[0], b_mod)
    sh1, sc1, g1, sh2, sc2, g2 = [mod[:B, i * D:(i + 1) * D].reshape(B, 1, D) for i in range(6)]
    csh1 = mod[B:B + 1, 0:D].reshape(1, 1, D)
    csc1 = mod[B:B + 1, D:2 * D].reshape(1, 1, D)

    src = _SLOT_SRC
    used = jnp.asarray(src >= 0, F32)
    nope = jnp.asarray((src >= 0) & (src < QK_NOPE), F32)
    rope = jnp.asarray(src >= QK_NOPE, F32)
    lane_src = np.maximum(src, 0)
    wi = w_in[0]
    pe_cols = jnp.take(wi[:, OFF_KPE:OFF_F], np.maximum(src - QK_NOPE, 0), axis=1) * rope
    w_in_p = jnp.concatenate([wi[:, OFF_Q:OFF_KV], pe_cols, wi[:, OFF_KV:OFF_KPE], wi[:, OFF_F:N_IN]], axis=1)
    w_in_p = w_in_p.astype(BF16)
    wq_p = jnp.take(w_q_up[0].reshape(Q_LORA, H, QK_DIM), lane_src, axis=2) * used
    wq_p = wq_p.reshape(Q_LORA, H * HEAD_SLOT).astype(BF16)
    wkv = w_kv_up[0].reshape(KV_LORA, H, QK_NOPE + V_DIM)
    wk_p = jnp.take(wkv[:, :, :QK_NOPE], np.minimum(lane_src, QK_NOPE - 1), axis=2) * nope
    wk_p = wk_p.reshape(KV_LORA, H * HEAD_SLOT).astype(BF16)
    wv_p = wkv[:, :, QK_NOPE:].reshape(KV_LORA, H * V_DIM).astype(BF16)
    qg_p = (jnp.take(q_norm_g[0], lane_src) * used * (QK_DIM ** -0.5 * LOG2E)).reshape(1, HEAD_SLOT)
    kg_p = (jnp.take(k_norm_g[0], lane_src) * used).reshape(1, HEAD_SLOT)
    bound = (QK_DIM * BOUND_SLACK) * jnp.max(jnp.abs(qg_p)) * jnp.max(jnp.abs(kg_p))
    on_shift = jnp.asarray(np.arange(HEAD_SLOT) == SHIFT_LANE, F32).reshape(1, HEAD_SLOT)
    qadd = on_shift * (-bound)
    kadd = on_shift
    qag = q_a_norm_g[0].reshape(1, Q_LORA)
    kvag = kv_a_norm_g[0].reshape(1, KV_LORA)
    g1n = norm1_g[0].reshape(1, D)
    g2n = norm2_g[0].reshape(1, D)
    cos_t, sin_t = _rope_tables(T)

    q, k_lat, vt_lat, f, ga, gb = _proj_lat_call(x, g1n, sc1, sh1, w_in_p, qag, kvag, wq_p, wk_p, wv_p,
                                                 qg_p, kg_p, qadd, kadd, cos_t, sin_t, tm=512)
    k_ctx, vt_ctx = _proj_ctx_call(ctx, g1n, csc1, csh1, w_in_p, kvag, wk_p, wv_p, kg_p, kadd)
    attn = _attn_call(q, k_lat, k_ctx, vt_lat, vt_ctx, tq=1024)
    four = _fourier_call(f, *_dft_tables(T))

    wr = jnp.pad(w_router[0], ((0, 0), (0, LANES - N_EXPERTS)))
    wrh = wr.astype(BF16)
    wrl = wrh + jnp.roll((wr - wrh.astype(F32)).astype(BF16), N_EXPERTS, axis=1)
    x1, h2t, aff = _merge_call(attn, four, ga, gb, x, g1, g2n, sc2, sh2, w_o_attn[0].astype(BF16),
                               w_fourier[0].astype(BF16), w_out[0].astype(BF16), wrh, wrl, tm=512)

    tri = jnp.asarray(np.triu(np.ones((LANES, LANES), np.float32)), BF16)
    addr = np.arange(T) * (D // LANES)
    pos = jnp.asarray(np.stack([addr // LANES, addr % LANES]).astype(np.float32), BF16)
    idx, gate = _select_call(aff, tri, pos, cap)
    y = _moe_call(idx, gate, h2t, w_e_gate[0].astype(BF16), w_e_up[0].astype(BF16),
                  w_e_down[0].astype(BF16), T)
    return _final_call(x1, y, g2, tm=1024)
```
